```python
import jax, jax.numpy as jnp
from jax import lax
import numpy as np

D_MODEL = 2048
BATCH = 8
SEQ = 4096
DEPTH = 4

HEAD_DIM = 64
N_MIX_HEADS = D_MODEL // HEAD_DIM
MIX_WIDTH = N_MIX_HEADS * HEAD_DIM
A_HEADS = N_MIX_HEADS // 4
A_KV_HEADS = 2
A_WINDOW = 128
A_BLOCK = 128
B_HEADS = N_MIX_HEADS // 4
MOBA_BLOCK = 256
MOBA_TOPK = 3
MOBA_Q_CHUNK = 32
C_HEADS = N_MIX_HEADS - A_HEADS - B_HEADS
C_PATTERNS = ((128, 1), (512, 4), (2048, 16))
MEM_LEN = 256
X_HEADS = 4
X_HEAD_DIM = 128
X_WIDTH = X_HEADS * X_HEAD_DIM
D_FF = -(-8 * D_MODEL // (3 * 256)) * 256
EPS = 1e-5

SPLIT_WIDTHS = (A_HEADS * HEAD_DIM, A_KV_HEADS * HEAD_DIM, A_KV_HEADS * HEAD_DIM,
                B_HEADS * HEAD_DIM, B_HEADS * HEAD_DIM, B_HEADS * HEAD_DIM,
                C_HEADS * HEAD_DIM, C_HEADS * HEAD_DIM, C_HEADS * HEAD_DIM)
IN_WIDTH = sum(SPLIT_WIDTHS)

kernel_name = 'hybrid_swa_moba_dilated_trunk'


def rmsnorm(x, g):
    xf = x.astype(jnp.float32)
    y = xf * lax.rsqrt(jnp.mean(xf * xf, axis=-1, keepdims=True) + EPS)
    return (y * g.astype(jnp.float32)).astype(x.dtype)


def alibi_slopes(n):
    return jnp.asarray(2.0 ** (-8.0 * np.arange(1, n + 1) / n), jnp.float32)


def with_prev_block(x):
    prev = jnp.concatenate([jnp.zeros_like(x[..., :1, :, :]), x[..., :-1, :, :]], axis=-3)
    return jnp.concatenate([prev, x], axis=-2)


def sliding_window_sink_attn(q, k, v, sinks):
    B, Hk, G, S, hd = q.shape
    f32 = jnp.float32
    nb = S // A_BLOCK
    qb = q.reshape(B, Hk, G, nb, A_BLOCK, hd)
    kb = with_prev_block(k.reshape(B, Hk, nb, A_BLOCK, hd))
    vb = with_prev_block(v.reshape(B, Hk, nb, A_BLOCK, hd))
    s = jnp.einsum('bkgnqd,bknsd->bkgnqs', qb, kb).astype(f32) * (hd ** -0.5)
    dist = jnp.arange(A_BLOCK)[:, None] + A_BLOCK - jnp.arange(2 * A_BLOCK)[None, :]
    kpos_ok = ((jnp.arange(nb)[:, None] - 1) * A_BLOCK + jnp.arange(2 * A_BLOCK)[None, :]) >= 0
    valid = ((dist >= 0) & (dist < A_WINDOW))[None] & kpos_ok[:, None, :]
    slopes = alibi_slopes(Hk * G).reshape(Hk, G)[:, :, None, None, None]
    s = jnp.where(valid, s - slopes * dist.astype(f32), -jnp.inf)
    sink = sinks.astype(f32).reshape(Hk, G)[:, :, None, None, None]
    m = jnp.maximum(jnp.max(s, axis=-1, keepdims=True), sink)
    p = jnp.exp(s - m)
    p = p / (jnp.sum(p, axis=-1, keepdims=True) + jnp.exp(sink - m))
    o = jnp.einsum('bkgnqs,bknsd->bkgnqd', p.astype(v.dtype), vb)
    return o.reshape(B, Hk, G, S, hd)


def moba_attn(q, k, v):
    B, H, S, hd = q.shape
    f32 = jnp.float32
    sp = -(-S // MOBA_BLOCK) * MOBA_BLOCK
    pad = ((0, 0), (0, 0), (0, sp - S), (0, 0))
    q, k, v = jnp.pad(q, pad), jnp.pad(k, pad), jnp.pad(v, pad)
    nb = sp // MOBA_BLOCK
    kb = k.reshape(B, H, nb, MOBA_BLOCK, hd)
    vb = v.reshape(B, H, nb, MOBA_BLOCK, hd)
    n_sel = min(MOBA_TOPK, nb - 1)
    scale = hd ** -0.5
    slopes = alibi_slopes(H)
    QC = MOBA_Q_CHUNK
    nc = sp // QC
    qc = q.reshape(B, H, nc, QC, hd).transpose(2, 0, 1, 3, 4)
    xs = (jnp.arange(nc), qc)
    if n_sel > 0:
        kmean = jnp.mean(kb.astype(f32), axis=3)
        gate = jnp.einsum('bhtd,bhnd->bhtn', q.astype(f32), kmean)
        past = jnp.arange(nb)[None, :] < (jnp.arange(sp) // MOBA_BLOCK)[:, None]
        gate = jnp.where(past, gate, -jnp.inf)
        _, sel = lax.top_k(gate, n_sel)
        xs = xs + (sel.reshape(B, H, nc, QC, n_sel).transpose(2, 0, 1, 3, 4),)

    def chunk(args):
        ci, qi = args[0], args[1]
        t = ci * QC + jnp.arange(QC)
        blk = (ci * QC) // MOBA_BLOCK
        k_own = lax.dynamic_index_in_dim(kb, blk, axis=2, keepdims=False)
        v_own = lax.dynamic_index_in_dim(vb, blk, axis=2, keepdims=False)
        dist_own = t[:, None] - (blk * MOBA_BLOCK + jnp.arange(MOBA_BLOCK))[None, :]
        s_own = jnp.einsum('bhqd,bhsd->bhqs', qi, k_own).astype(f32) * scale - slopes[:, None, None] * dist_own.astype(f32)
        s_own = jnp.where(dist_own >= 0, s_own, -jnp.inf)
        if n_sel == 0:
            p = jax.nn.softmax(s_own, axis=-1).astype(v.dtype)
            return jnp.einsum('bhqs,bhsd->bhqd', p, v_own)
        si = args[2]
        bi = jnp.arange(B)[:, None, None, None]
        hi = jnp.arange(H)[None, :, None, None]
        k_sel = kb[bi, hi, si]
        v_sel = vb[bi, hi, si]
        dist_sel = t[:, None, None] - (si[..., None] * MOBA_BLOCK + jnp.arange(MOBA_BLOCK))
        s_sel = jnp.einsum('bhqd,bhqrsd->bhqrs', qi, k_sel).astype(f32) * scale - slopes[:, None, None, None] * dist_sel.astype(f32)
        s_sel = jnp.where((si < blk)[..., None], s_sel, -jnp.inf)
        s_all = jnp.concatenate([s_sel.reshape(B, H, QC, n_sel * MOBA_BLOCK), s_own], axis=-1)
        p = jax.nn.softmax(s_all, axis=-1).astype(v.dtype)
        p_sel = p[..., :n_sel * MOBA_BLOCK].reshape(B, H, QC, n_sel, MOBA_BLOCK)
        return (jnp.einsum('bhqrs,bhqrsd->bhqd', p_sel, v_sel)
                + jnp.einsum('bhqs,bhsd->bhqd', p[..., n_sel * MOBA_BLOCK:], v_own))

    o = lax.map(chunk, xs)
    return o.transpose(1, 2, 0, 3, 4).reshape(B, H, sp, hd)[:, :, :S]


def _residue_major(x, d, sp):
    B, H, S, e = x.shape
    x = jnp.pad(x, ((0, 0), (0, 0), (0, sp - S), (0, 0)))
    return x.reshape(B, H, sp // d, d, e).transpose(0, 1, 3, 2, 4)


def _seq_major(x, sp):
    B, H, d, L, e = x.shape
    return x.transpose(0, 1, 3, 2, 4).reshape(B, H, sp, e)


def dilated_mixture_attn(q, k, v):
    B, H, S, hd = q.shape
    f32 = jnp.float32
    scale = hd ** -0.5
    slopes = alibi_slopes(H)[:, None, None, None, None]
    outs, lses = [], []
    for (w, d) in C_PATTERNS:
        n = w // d
        unit = d * n
        sp = -(-S // unit) * unit
        L = sp // d
        nb = L // n
        qs = _residue_major(q, d, sp).reshape(B, H, d, nb, n, hd)
        ks = with_prev_block(_residue_major(k, d, sp).reshape(B, H, d, nb, n, hd))
        vs = with_prev_block(_residue_major(v, d, sp).reshape(B, H, d, nb, n, hd))
        s = jnp.einsum('bhrnqd,bhrnsd->bhrnqs', qs, ks).astype(f32) * scale
        step = jnp.arange(n)[:, None] + n - jnp.arange(2 * n)[None, :]
        kpos_ok = ((jnp.arange(nb)[:, None] - 1) * n + jnp.arange(2 * n)[None, :]) >= 0
        valid = ((step >= 0) & (step <= n))[None] & kpos_ok[:, None, :]
        s = jnp.where(valid, s - slopes * (step * d).astype(f32), -jnp.inf)
        m = jnp.max(s, axis=-1, keepdims=True)
        p = jnp.exp(s - m)
        l = jnp.sum(p, axis=-1, keepdims=True)
        o = jnp.einsum('bhrnqs,bhrnsd->bhrnqd', (p / l).astype(v.dtype), vs)
        outs.append(_seq_major(o.reshape(B, H, d, L, hd), sp)[:, :, :S])
        lses.append(_seq_major((m + jnp.log(l)).reshape(B, H, d, L, 1), sp)[:, :, :S])
    wts = jax.nn.softmax(jnp.stack(lses), axis=0)
    o = jnp.sum(wts * jnp.stack(outs).astype(f32), axis=0)
    return o.astype(q.dtype)


def _heads(x, n):
    B, S, _ = x.shape
    return x.reshape(B, S, n, HEAD_DIM).transpose(0, 2, 1, 3)


def _merge(x):
    B, H, S, hd = x.shape
    return x.transpose(0, 2, 1, 3).reshape(B, S, H * hd)


def hybrid_mixer(a, w_in, sinks, gain_a, gain_b, gain_c, w_out):
    B, S, _ = a.shape
    proj = a @ w_in
    offsets = np.cumsum(SPLIT_WIDTHS)[:-1].tolist()
    qa, ka, va, qb, kb, vb, qc, kc, vc = jnp.split(proj, offsets, axis=-1)
    G = A_HEADS // A_KV_HEADS
    qa = qa.reshape(B, S, A_KV_HEADS, G, HEAD_DIM).transpose(0, 2, 3, 1, 4)
    oa = sliding_window_sink_attn(qa, _heads(ka, A_KV_HEADS), _heads(va, A_KV_HEADS), sinks)
    oa = oa.transpose(0, 3, 1, 2, 4).reshape(B, S, A_HEADS * HEAD_DIM)
    ob = _merge(moba_attn(_heads(qb, B_HEADS), _heads(kb, B_HEADS), _heads(vb, B_HEADS)))
    oc = _merge(dilated_mixture_attn(_heads(qc, C_HEADS), _heads(kc, C_HEADS), _heads(vc, C_HEADS)))
    y = jnp.concatenate([rmsnorm(oa, gain_a), rmsnorm(ob, gain_b), rmsnorm(oc, gain_c)], axis=-1)
    return y @ w_out


def memory_cross_attn(h, mem_n, wq, wk, wv, wo):
    B, S, _ = h.shape
    M = mem_n.shape[1]
    q = (h @ wq).reshape(B, S, X_HEADS, X_HEAD_DIM)
    k = (mem_n @ wk).reshape(B, M, X_HEADS, X_HEAD_DIM)
    v = (mem_n @ wv).reshape(B, M, X_HEADS, X_HEAD_DIM)
    s = jnp.einsum('bqhd,bmhd->bhqm', q, k).astype(jnp.float32) * (X_HEAD_DIM ** -0.5)
    p = jax.nn.softmax(s, axis=-1).astype(v.dtype)
    o = jnp.einsum('bhqm,bmhd->bqhd', p, v).reshape(B, S, X_WIDTH)
    return o @ wo


def swiglu(h, w_gate, w_up, w_down):
    return (jax.nn.silu(h @ w_gate) * (h @ w_up)) @ w_down


def setup_inputs(seed: int = 0) -> dict:
    key = jax.random.key(seed)
    ks = jax.random.split(key, 20)
    f32 = jnp.float32

    def w(k, fan_in, fan_out):
        return jax.random.normal(k, (DEPTH, fan_in, fan_out), f32) * fan_in ** -0.5

    def gain(k, n):
        return 1.0 + 0.02 * jax.random.normal(k, (DEPTH, n), f32)

    return {
        'x': jax.random.normal(ks[0], (BATCH, SEQ, D_MODEL), f32),
        'mem': jax.random.normal(ks[1], (BATCH, MEM_LEN, D_MODEL), f32),
        'norm_mix': gain(ks[2], D_MODEL),
        'w_in': w(ks[3], D_MODEL, IN_WIDTH),
        'sinks': 0.5 * jax.random.normal(ks[4], (DEPTH, A_HEADS), f32),
        'gain_a': gain(ks[5], A_HEADS * HEAD_DIM),
        'gain_b': gain(ks[6], B_HEADS * HEAD_DIM),
        'gain_c': gain(ks[7], C_HEADS * HEAD_DIM),
        'w_out': w(ks[8], MIX_WIDTH, D_MODEL),
        'norm_xattn': gain(ks[9], D_MODEL),
        'norm_mem': gain(ks[10], D_MODEL),
        'wq_x': w(ks[11], D_MODEL, X_WIDTH),
        'wk_x': w(ks[12], D_MODEL, X_WIDTH),
        'wv_x': w(ks[13], D_MODEL, X_WIDTH),
        'wo_x': w(ks[14], X_WIDTH, D_MODEL),
        'norm_ffn': gain(ks[15], D_MODEL),
        'w_gate': w(ks[16], D_MODEL, D_FF),
        'w_up': w(ks[17], D_MODEL, D_FF),
        'w_down': w(ks[18], D_FF, D_MODEL),
        'norm_final': 1.0 + 0.02 * jax.random.normal(ks[19], (D_MODEL,), f32),
    }


def reference(x, mem, norm_mix, w_in, sinks, gain_a, gain_b, gain_c, w_out,
              norm_xattn, norm_mem, wq_x, wk_x, wv_x, wo_x,
              norm_ffn, w_gate, w_up, w_down, norm_final):
    h = x
    for l in range(DEPTH):
        a = rmsnorm(h, norm_mix[l])
        h = h + hybrid_mixer(a, w_in[l], sinks[l], gain_a[l], gain_b[l], gain_c[l], w_out[l])
        mem_n = rmsnorm(mem, norm_mem[l])
        h = h + memory_cross_attn(rmsnorm(h, norm_xattn[l]), mem_n, wq_x[l], wk_x[l], wv_x[l], wo_x[l])
        h = h + swiglu(rmsnorm(h, norm_ffn[l]), w_gate[l], w_up[l], w_down[l])
    return rmsnorm(h, norm_final)
```

```python
import functools

import numpy as np
import jax
import jax.numpy as jnp
from jax import lax
from jax.experimental import pallas as pl
from jax.experimental.pallas import tpu as pltpu

F32 = jnp.float32
BF16 = jnp.bfloat16
NEG_INF = float("-inf")

EPS = 1e-5
HEAD_DIM = 64
LANES = 128
A_HEADS, A_KV_HEADS, A_WINDOW, A_BLOCK = 8, 2, 128, 128
B_HEADS, MOBA_BLOCK, MOBA_TOPK = 8, 256, 3
C_HEADS = 16
C_PATTERNS = ((128, 1), (512, 4), (2048, 16))
C_STEPS = 128
X_HEADS, X_HEAD_DIM = 4, 128
VMEM_LIMIT = 56 * 1024 * 1024


def _alibi_slopes(n):
    return jnp.asarray(2.0 ** (-8.0 * np.arange(1, n + 1) / n), F32)


def _params(*sem):
    return pltpu.CompilerParams(dimension_semantics=sem, vmem_limit_bytes=VMEM_LIMIT)


def _rms(x, g):
    return x * lax.rsqrt(jnp.mean(x * x, axis=-1, keepdims=True) + EPS) * g


def _dot_nt(a, b):
    return lax.dot_general(a, b, (((1,), (1,)), ((), ())), preferred_element_type=F32)


def _smem_spec():
    return pl.BlockSpec(memory_space=pltpu.SMEM)


def _norm_matmul_kernel(x_ref, g_ref, w_ref, o_ref, xn_ref):
    @pl.when(pl.program_id(1) == 0)
    def _():
        xn_ref[...] = _rms(x_ref[...], g_ref[...]).astype(BF16)

    o_ref[...] = jnp.dot(xn_ref[...], w_ref[...], preferred_element_type=F32).astype(o_ref.dtype)


def norm_matmul(x, g, w, *, tm, tn, out_dtype):
    M, D = x.shape
    N = w.shape[1]
    assert M % tm == 0 and N % tn == 0
    return pl.pallas_call(
        _norm_matmul_kernel,
        grid=(M // tm, N // tn),
        in_specs=[pl.BlockSpec((tm, D), lambda i, j: (i, 0)),
                  pl.BlockSpec((1, D), lambda i, j: (0, 0)),
                  pl.BlockSpec((D, tn), lambda i, j: (0, j))],
        out_specs=pl.BlockSpec((tm, tn), lambda i, j: (i, j)),
        out_shape=jax.ShapeDtypeStruct((M, N), out_dtype),
        scratch_shapes=[pltpu.VMEM((tm, D), BF16)],
        compiler_params=_params("parallel", "arbitrary"),
        name="norm_matmul",
    )(x, g, w)


def _attn_a_kernel(slopes_ref, sinks_ref, q_ref, k_ref, v_ref, o_ref, *, n_sub):
    blk = A_BLOCK
    qi = pl.program_id(1)
    scale = HEAD_DIM ** -0.5
    row = lax.broadcasted_iota(jnp.int32, (blk, 2 * blk), 0)
    col = lax.broadcasted_iota(jnp.int32, (blk, 2 * blk), 1)
    dist = row + blk - col
    in_window = (dist >= 0) & (dist < A_WINDOW)
    distf = dist.astype(F32)
    lo_half = lax.broadcasted_iota(jnp.int32, (blk, LANES), 1) < HEAD_DIM
    group = A_HEADS // A_KV_HEADS

    def sub_block(sb, carry):
        gb = qi * n_sub + sb
        own = pl.multiple_of(gb * blk, blk)
        prev = pl.multiple_of(jnp.maximum(gb - 1, 0) * blk, blk)
        rows = pl.ds(pl.multiple_of(sb * blk, blk), blk)
        kk = jnp.concatenate([k_ref[pl.ds(prev, blk), :], k_ref[pl.ds(own, blk), :]], axis=0)
        vv = jnp.concatenate([v_ref[pl.ds(prev, blk), :], v_ref[pl.ds(own, blk), :]], axis=0)
        k_by_half = (kk.astype(BF16), pltpu.roll(kk, HEAD_DIM, axis=1).astype(BF16))
        v_by_half = (vv.astype(BF16), pltpu.roll(vv, HEAD_DIM, axis=1).astype(BF16))
        valid = in_window & ((col >= blk) | (gb > 0))
        for pair in range(A_HEADS // 2):
            qp = q_ref[rows, pair * LANES:(pair + 1) * LANES] * scale
            outs = []
            for half in range(2):
                j = 2 * pair + half
                swapped = (j // group) != half
                qm = jnp.where(lo_half if half == 0 else ~lo_half, qp, 0.0).astype(BF16)
                s = _dot_nt(qm, k_by_half[swapped])
                s = jnp.where(valid, s - slopes_ref[j] * distf, NEG_INF)
                sink = sinks_ref[j]
                m = jnp.maximum(jnp.max(s, axis=-1, keepdims=True), sink)
                p = jnp.exp(s - m)
                denom = jnp.sum(p, axis=-1, keepdims=True) + jnp.exp(sink - m)
                o = jnp.dot(p.astype(BF16), v_by_half[swapped], preferred_element_type=F32)
                outs.append(o / denom)
            o_ref[rows, pair * LANES:(pair + 1) * LANES] = jnp.where(lo_half, outs[0], outs[1])
        return carry

    lax.fori_loop(0, n_sub, sub_block, 0)


def attn_a(proj, sinks, *, q_col, k_col, v_col, tq):
    B, S, _ = proj.shape
    qw = A_HEADS * HEAD_DIM
    assert S % tq == 0 and tq % A_BLOCK == 0
    assert q_col % qw == 0 and k_col % LANES == 0 and v_col % LANES == 0
    kern = functools.partial(_attn_a_kernel, n_sub=tq // A_BLOCK)
    return pl.pallas_call(
        kern,
        grid=(B, S // tq),
        in_specs=[_smem_spec(), _smem_spec(),
                  pl.BlockSpec((None, tq, qw), lambda b, i: (b, i, q_col // qw)),
                  pl.BlockSpec((None, S, LANES), lambda b, i: (b, 0, k_col // LANES)),
                  pl.BlockSpec((None, S, LANES), lambda b, i: (b, 0, v_col // LANES))],
        out_specs=pl.BlockSpec((None, tq, qw), lambda b, i: (b, i, 0)),
        out_shape=jax.ShapeDtypeStruct((B, S, qw), F32),
        compiler_params=_params("parallel", "arbitrary"),
        name="attn_swa",
    )(_alibi_slopes(A_HEADS), sinks, proj, proj, proj)


def _attn_b_kernel(slopes_ref, q_ref, k_ref, v_ref, o_ref,
                   kmean_ref, qm_ref, sel_ref, acc_ref, m_ref, l_ref, *, nb):
    mb = MOBA_BLOCK
    hp = pl.program_id(1)
    i = pl.program_id(2)
    scale = HEAD_DIM ** -0.5

    @pl.when(i == 0)
    def _():
        kmean_ref[...] = jnp.zeros_like(kmean_ref)
        for n in range(nb):
            kmean_ref[n:n + 1, :] = jnp.mean(k_ref[n * mb:(n + 1) * mb, :], axis=0, keepdims=True)

    lane = lax.broadcasted_iota(jnp.int32, (mb, LANES), 1)
    lo_half = lane < HEAD_DIM
    q = q_ref[...]
    rel = (lax.broadcasted_iota(jnp.int32, (mb, mb), 0)
           - lax.broadcasted_iota(jnp.int32, (mb, mb), 1))
    relf = rel.astype(F32)
    own = pl.multiple_of(i * mb, mb)
    k_own = k_ref[pl.ds(own, mb), :].astype(BF16)
    v_own = v_ref[pl.ds(own, mb), :].astype(BF16)

    for h in range(2):
        half_mask = lo_half if h == 0 else ~lo_half
        qh = jnp.where(half_mask, q, 0.0)
        gate = lax.dot_general(qh, kmean_ref[...], (((1,), (1,)), ((), ())),
                               precision=lax.Precision.HIGHEST, preferred_element_type=F32)
        past = lane < i
        g = jnp.where(past, gate, NEG_INF)
        avail = lane < nb
        sel = jnp.zeros((mb, LANES), F32)
        for _ in range(MOBA_TOPK):
            mx = jnp.max(jnp.where(avail, g, NEG_INF), axis=-1, keepdims=True)
            is_max = avail & (g == mx)
            idx = jnp.min(jnp.where(is_max, lane, LANES), axis=-1, keepdims=True)
            pick = lane == idx
            sel = jnp.where(pick & past, 1.0, sel)
            avail = avail & ~pick
        sel_ref[h] = sel.astype(BF16)
        qs = (qh * scale).astype(BF16)
        qm_ref[h] = qs
        s = _dot_nt(qs, k_own) - slopes_ref[2 * hp + h] * relf
        s = jnp.where(rel >= 0, s, NEG_INF)
        m = jnp.max(s, axis=-1, keepdims=True)
        p = jnp.exp(s - m)
        m_ref[h] = m
        l_ref[h] = jnp.sum(p, axis=-1, keepdims=True)
        acc_ref[h] = jnp.dot(p.astype(BF16), v_own, preferred_element_type=F32)

    def past_block(j, carry):
        ks = pl.multiple_of(j * mb, mb)
        kj = k_ref[pl.ds(ks, mb), :].astype(BF16)
        vj = v_ref[pl.ds(ks, mb), :].astype(BF16)
        expand = jnp.where(lax.broadcasted_iota(jnp.int32, (LANES, mb), 0) == j, 1.0, 0.0).astype(BF16)
        offset = ((i - j) * mb).astype(F32)
        for h in range(2):
            chosen = jnp.dot(sel_ref[h], expand, preferred_element_type=F32)
            s = _dot_nt(qm_ref[h], kj) - slopes_ref[2 * hp + h] * (relf + offset)
            s = jnp.where(chosen > 0.5, s, NEG_INF)
            m_old = m_ref[h]
            m_new = jnp.maximum(m_old, jnp.max(s, axis=-1, keepdims=True))
            alpha = jnp.exp(m_old - m_new)
            p = jnp.exp(s - m_new)
            m_ref[h] = m_new
            l_ref[h] = alpha * l_ref[h] + jnp.sum(p, axis=-1, keepdims=True)
            acc_ref[h] = alpha * acc_ref[h] + jnp.dot(p.astype(BF16), vj, preferred_element_type=F32)
        return carry

    lax.fori_loop(0, i, past_block, 0)
    o_ref[...] = jnp.where(lo_half, acc_ref[0] / l_ref[0], acc_ref[1] / l_ref[1])


def attn_b(proj, *, q_col, k_col, v_col):
    B, S, _ = proj.shape
    mb = MOBA_BLOCK
    assert S % mb == 0
    nb = S // mb
    assert MOBA_TOPK < nb <= LANES
    pairs = B_HEADS // 2
    kern = functools.partial(_attn_b_kernel, nb=nb)
    return pl.pallas_call(
        kern,
        grid=(B, pairs, nb),
        in_specs=[_smem_spec(),
                  pl.BlockSpec((None, mb, LANES), lambda b, p, i: (b, i, q_col // LANES + p)),
                  pl.BlockSpec((None, S, LANES), lambda b, p, i: (b, 0, k_col // LANES + p)),
                  pl.BlockSpec((None, S, LANES), lambda b, p, i: (b, 0, v_col // LANES + p))],
        out_specs=pl.BlockSpec((None, mb, LANES), lambda b, p, i: (b, i, p)),
        out_shape=jax.ShapeDtypeStruct((B, S, B_HEADS * HEAD_DIM), F32),
        scratch_shapes=[pltpu.VMEM((LANES, LANES), F32),
                        pltpu.VMEM((2, mb, LANES), BF16),
                        pltpu.VMEM((2, mb, LANES), BF16),
                        pltpu.VMEM((2, mb, LANES), F32),
                        pltpu.VMEM((2, mb, 1), F32),
                        pltpu.VMEM((2, mb, 1), F32)],
        compiler_params=_params("parallel", "parallel", "arbitrary"),
        name="attn_moba",
    )(_alibi_slopes(B_HEADS), proj, proj, proj)


def _attn_c_kernel(slopes_ref, q_ref, k_ref, v_ref, o_ref, acc_ref, m_ref, l_ref, *, seq):
    n = C_STEPS
    hp = pl.program_id(1)
    scale = HEAD_DIM ** -0.5
    row = lax.broadcasted_iota(jnp.int32, (n, 2 * n), 0)
    col = lax.broadcasted_iota(jnp.int32, (n, 2 * n), 1)
    step = row + n - col
    valid = (step >= 0) & (step <= n)
    own_half = col >= n
    stepf = step.astype(F32)
    lo_half = lax.broadcasted_iota(jnp.int32, (n, LANES), 1) < HEAD_DIM

    for bi, (w, d) in enumerate(C_PATTERNS):
        unit = w
        nblk = seq // unit
        first, last = bi == 0, bi == len(C_PATTERNS) - 1
        bias = [(-slopes_ref[2 * hp + h] * float(d)) * stepf for h in range(2)]

        def rows_at(start, d=d):
            return pl.ds(start, n) if d == 1 else pl.ds(start, n, stride=d)

        def tile(t, carry, d=d, unit=unit, nblk=nblk, first=first, last=last, bias=bias, rows_at=rows_at):
            r = t // nblk
            b = t % nblk
            own = b * unit + r
            prev = jnp.maximum(b - 1, 0) * unit + r
            q = q_ref[rows_at(own), :] * scale
            kk = jnp.concatenate([k_ref[rows_at(prev), :], k_ref[rows_at(own), :]], axis=0).astype(BF16)
            vv = jnp.concatenate([v_ref[rows_at(prev), :], v_ref[rows_at(own), :]], axis=0).astype(BF16)
            ok = valid & (own_half | (b > 0))
            ms, ls, os_ = [], [], []
            for h in range(2):
                qm = jnp.where(lo_half if h == 0 else ~lo_half, q, 0.0).astype(BF16)
                s = jnp.where(ok, _dot_nt(qm, kk) + bias[h], NEG_INF)
                m = jnp.max(s, axis=-1, keepdims=True)
                p = jnp.exp(s - m)
                ms.append(m)
                ls.append(jnp.sum(p, axis=-1, keepdims=True))
                os_.append(jnp.dot(p.astype(BF16), vv, preferred_element_type=F32))
            m_t = jnp.where(lo_half, ms[0], ms[1])
            l_t = jnp.where(lo_half, ls[0], ls[1])
            o_t = jnp.where(lo_half, os_[0], os_[1])
            idx = rows_at(own)
            if not first:
                m_old = m_ref[idx, :]
                m_new = jnp.maximum(m_old, m_t)
                a_old = jnp.exp(m_old - m_new)
                a_t = jnp.exp(m_t - m_new)
                l_t = a_old * l_ref[idx, :] + a_t * l_t
                o_t = a_old * acc_ref[idx, :] + a_t * o_t
                m_t = m_new
            if last:
                o_ref[idx, :] = o_t / l_t
            else:
                m_ref[idx, :] = m_t
                l_ref[idx, :] = l_t
                acc_ref[idx, :] = o_t
            return carry

        lax.fori_loop(0, d * nblk, tile, 0)


def attn_c(proj, *, q_col, k_col, v_col):
    B, S, _ = proj.shape
    assert all(S % w == 0 and w // d == C_STEPS for w, d in C_PATTERNS)
    pairs = C_HEADS // 2
    kern = functools.partial(_attn_c_kernel, seq=S)
    return pl.pallas_call(
        kern,
        grid=(B, pairs),
        in_specs=[_smem_spec(),
                  pl.BlockSpec((None, S, LANES), lambda b, p: (b, 0, q_col // LANES + p)),
                  pl.BlockSpec((None, S, LANES), lambda b, p: (b, 0, k_col // LANES + p)),
                  pl.BlockSpec((None, S, LANES), lambda b, p: (b, 0, v_col // LANES + p))],
        out_specs=pl.BlockSpec((None, S, LANES), lambda b, p: (b, 0, p)),
        out_shape=jax.ShapeDtypeStruct((B, S, C_HEADS * HEAD_DIM), F32),
        scratch_shapes=[pltpu.VMEM((S, LANES), F32),
                        pltpu.VMEM((S, LANES), F32),
                        pltpu.VMEM((S, LANES), F32)],
        compiler_params=_params("parallel", "parallel"),
        name="attn_dilated",
    )(_alibi_slopes(C_HEADS), proj, proj, proj)


def _mix_out_kernel(oa_ref, ob_ref, oc_ref, ga_ref, gb_ref, gc_ref, h_ref, w_ref, o_ref, y_ref):
    @pl.when(pl.program_id(1) == 0)
    def _():
        wa, wb = oa_ref.shape[1], ob_ref.shape[1]
        y_ref[:, :wa] = _rms(oa_ref[...], ga_ref[...]).astype(BF16)
        y_ref[:, wa:wa + wb] = _rms(ob_ref[...], gb_ref[...]).astype(BF16)
        y_ref[:, wa + wb:] = _rms(oc_ref[...], gc_ref[...]).astype(BF16)

    o_ref[...] = h_ref[...] + jnp.dot(y_ref[...], w_ref[...], preferred_element_type=F32)


def mix_out(oa, ob, oc, ga, gb, gc, h, w, *, tm, tn):
    M, D = h.shape
    wa, wb, wc = oa.shape[1], ob.shape[1], oc.shape[1]
    K = wa + wb + wc
    assert w.shape == (K, D) and M % tm == 0 and D % tn == 0
    row = lambda width: pl.BlockSpec((tm, width), lambda i, j: (i, 0))
    gain = lambda width: pl.BlockSpec((1, width), lambda i, j: (0, 0))
    return pl.pallas_call(
        _mix_out_kernel,
        grid=(M // tm, D // tn),
        in_specs=[row(wa), row(wb), row(wc), gain(wa), gain(wb), gain(wc),
                  pl.BlockSpec((tm, tn), lambda i, j: (i, j)),
                  pl.BlockSpec((K, tn), lambda i, j: (0, j))],
        out_specs=pl.BlockSpec((tm, tn), lambda i, j: (i, j)),
        out_shape=jax.ShapeDtypeStruct((M, D), F32),
        scratch_shapes=[pltpu.VMEM((tm, K), BF16)],
        compiler_params=_params("parallel", "arbitrary"),
        name="mix_out",
    )(oa, ob, oc, ga, gb, gc, h, w)


def _xattn_kernel(h_ref, g_ref, wq_ref, kv_ref, wo_ref, o_ref):
    xw = X_HEADS * X_HEAD_DIM
    h = h_ref[...]
    xn = _rms(h, g_ref[...]).astype(BF16)
    q = jnp.dot(xn, wq_ref[...], preferred_element_type=F32) * (X_HEAD_DIM ** -0.5)
    heads = []
    for hd in range(X_HEADS):
        lanes = slice(hd * X_HEAD_DIM, (hd + 1) * X_HEAD_DIM)
        s = _dot_nt(q[:, lanes].astype(BF16), kv_ref[:, lanes])
        m = jnp.max(s, axis=-1, keepdims=True)
        p = jnp.exp(s - m)
        l = jnp.sum(p, axis=-1, keepdims=True)
        v = kv_ref[:, xw + hd * X_HEAD_DIM:xw + (hd + 1) * X_HEAD_DIM]
        heads.append(jnp.dot(p.astype(BF16), v, preferred_element_type=F32) / l)
    o = jnp.concatenate(heads, axis=-1).astype(BF16)
    o_ref[...] = h + jnp.dot(o, wo_ref[...], preferred_element_type=F32)


def xattn(h, g, wq, kv, wo, *, tq):
    B, S, D = h.shape
    Mm = kv.shape[1]
    xw = X_HEADS * X_HEAD_DIM
    assert S % tq == 0
    return pl.pallas_call(
        _xattn_kernel,
        grid=(B, S // tq),
        in_specs=[pl.BlockSpec((None, tq, D), lambda b, i: (b, i, 0)),
                  pl.BlockSpec((1, D), lambda b, i: (0, 0)),
                  pl.BlockSpec((D, xw), lambda b, i: (0, 0)),
                  pl.BlockSpec((None, Mm, 2 * xw), lambda b, i: (b, 0, 0)),
                  pl.BlockSpec((xw, D), lambda b, i: (0, 0))],
        out_specs=pl.BlockSpec((None, tq, D), lambda b, i: (b, i, 0)),
        out_shape=jax.ShapeDtypeStruct((B, S, D), F32),
        compiler_params=_params("parallel", "parallel"),
        name="xattn",
    )(h, g, wq, kv, wo)


def _ffn_kernel(h_ref, g_ref, wg_ref, wu_ref, wd_ref, gf_ref, o_ref, xn_ref, acc_ref, *, final_norm):
    f = pl.program_id(1)

    @pl.when(f == 0)
    def _():
        xn_ref[...] = _rms(h_ref[...], g_ref[...]).astype(BF16)
        acc_ref[...] = jnp.zeros_like(acc_ref)

    xn = xn_ref[...]
    gate = jnp.dot(xn, wg_ref[...], preferred_element_type=F32)
    up = jnp.dot(xn, wu_ref[...], preferred_element_type=F32)
    act = (gate / (1.0 + jnp.exp(-gate))) * up
    acc_ref[...] += jnp.dot(act.astype(BF16), wd_ref[...], preferred_element_type=F32)

    @pl.when(f == pl.num_programs(1) - 1)
    def _():
        y = h_ref[...] + acc_ref[...]
        o_ref[...] = _rms(y, gf_ref[...]) if final_norm else y


def ffn(h, g, wg, wu, wd, gf, *, tm, tf, final_norm):
    M, D = h.shape
    Fd = wg.shape[1]
    assert M % tm == 0 and Fd % tf == 0
    kern = functools.partial(_ffn_kernel, final_norm=final_norm)
    return pl.pallas_call(
        kern,
        grid=(M // tm, Fd // tf),
        in_specs=[pl.BlockSpec((tm, D), lambda i, f: (i, 0)),
                  pl.BlockSpec((1, D), lambda i, f: (0, 0)),
                  pl.BlockSpec((D, tf), lambda i, f: (0, f)),
                  pl.BlockSpec((D, tf), lambda i, f: (0, f)),
                  pl.BlockSpec((tf, D), lambda i, f: (f, 0)),
                  pl.BlockSpec((1, D), lambda i, f: (0, 0))],
        out_specs=pl.BlockSpec((tm, D), lambda i, f: (i, 0)),
        out_shape=jax.ShapeDtypeStruct((M, D), F32),
        scratch_shapes=[pltpu.VMEM((tm, D), BF16), pltpu.VMEM((tm, D), F32)],
        compiler_params=_params("parallel", "arbitrary"),
        name="ffn",
    )(h, g, wg, wu, wd, gf)


def kernel(x, mem, norm_mix, w_in, sinks, gain_a, gain_b, gain_c, w_out, norm_xattn, norm_mem,
           wq_x, wk_x, wv_x, wo_x, norm_ffn, w_gate, w_up, w_down, norm_final):
    B, S, D = x.shape
    depth = w_in.shape[0]
    Mm = mem.shape[1]
    qa_w, kva_w = A_HEADS * HEAD_DIM, A_KV_HEADS * HEAD_DIM
    b_w, c_w = B_HEADS * HEAD_DIM, C_HEADS * HEAD_DIM
    widths = (qa_w, kva_w, kva_w, b_w, b_w, b_w, c_w, c_w, c_w)
    cols = [int(c) for c in np.concatenate([[0], np.cumsum(widths)[:-1]])]
    in_w = sum(widths)

    row = lambda v: v.reshape(1, -1)
    h = x.reshape(B * S, D)
    mem2 = mem.reshape(B * Mm, D)
    for l in range(depth):
        proj = norm_matmul(h, row(norm_mix[l]), w_in[l].astype(BF16), tm=512, tn=768, out_dtype=F32)
        proj = proj.reshape(B, S, in_w)
        oa = attn_a(proj, sinks[l], q_col=cols[0], k_col=cols[1], v_col=cols[2], tq=512)
        ob = attn_b(proj, q_col=cols[3], k_col=cols[4], v_col=cols[5])
        oc = attn_c(proj, q_col=cols[6], k_col=cols[7], v_col=cols[8])
        h = mix_out(oa.reshape(B * S, qa_w), ob.reshape(B * S, b_w), oc.reshape(B * S, c_w),
                    row(gain_a[l]), row(gain_b[l]), row(gain_c[l]), h, w_out[l].astype(BF16),
                    tm=512, tn=1024)
        w_kv = jnp.concatenate([wk_x[l], wv_x[l]], axis=1).astype(BF16)
        kv = norm_matmul(mem2, row(norm_mem[l]), w_kv, tm=512, tn=w_kv.shape[1], out_dtype=BF16)
        h = xattn(h.reshape(B, S, D), row(norm_xattn[l]), wq_x[l].astype(BF16),
                  kv.reshape(B, Mm, -1), wo_x[l].astype(BF16), tq=512).reshape(B * S, D)
        h = ffn(h, row(norm_ffn[l]), w_gate[l].astype(BF16), w_up[l].astype(BF16),
                w_down[l].astype(BF16), row(norm_final), tm=512, tf=512,
                final_norm=(l == depth - 1))
    return h.reshape(B, S, D)
```

```python
import functools

import numpy as np
import jax
import jax.numpy as jnp
from jax import lax
from jax.experimental import pallas as pl
from jax.experimental.pallas import tpu as pltpu

F32 = jnp.float32
BF16 = jnp.bfloat16
NEG_INF = float("-inf")
POS_INF = float("inf")
LOG2E = 1.4426950408889634

EPS = 1e-5
HEAD_DIM = 64
LANES = 128
BF16_ROWS = 16
A_HEADS, A_KV_HEADS, A_WINDOW, A_BLOCK = 8, 2, 128, 128
B_HEADS, MOBA_BLOCK, MOBA_TOPK = 8, 256, 3
C_HEADS = 16
C_PATTERNS = ((128, 1), (512, 4), (2048, 16))
C_STEPS = 128
X_HEADS, X_HEAD_DIM = 4, 128
VMEM_LIMIT = 56 * 1024 * 1024
VT_ROWS = HEAD_DIM + BF16_ROWS


def _alibi_slopes(n):
    return jnp.asarray(2.0 ** (-8.0 * np.arange(1, n + 1) / n), F32)


def _params(*sem):
    return pltpu.CompilerParams(dimension_semantics=sem, vmem_limit_bytes=VMEM_LIMIT)


def _rms(x, g):
    return x * lax.rsqrt(jnp.mean(x * x, axis=-1, keepdims=True) + EPS) * g


def _dot_nt(a, b, precision=None):
    return lax.dot_general(a, b, (((1,), (1,)), ((), ())), precision=precision,
                           preferred_element_type=F32)


def _smem_spec():
    return pl.BlockSpec(memory_space=pltpu.SMEM)


def _norm_matmul_kernel(x_ref, g_ref, w_ref, o_ref, xn_ref):
    @pl.when(pl.program_id(1) == 0)
    def _():
        xn_ref[...] = _rms(x_ref[...], g_ref[...]).astype(BF16)

    o_ref[...] = jnp.dot(xn_ref[...], w_ref[...], preferred_element_type=F32).astype(o_ref.dtype)


def norm_matmul(x, g, w, *, tm, tn, out_dtype):
    M, D = x.shape
    N = w.shape[1]
    assert M % tm == 0 and N % tn == 0
    return pl.pallas_call(
        _norm_matmul_kernel,
        grid=(M // tm, N // tn),
        in_specs=[pl.BlockSpec((tm, D), lambda i, j: (i, 0)),
                  pl.BlockSpec((1, D), lambda i, j: (0, 0)),
                  pl.BlockSpec((D, tn), lambda i, j: (0, j))],
        out_specs=pl.BlockSpec((tm, tn), lambda i, j: (i, j)),
        out_shape=jax.ShapeDtypeStruct((M, N), out_dtype),
        scratch_shapes=[pltpu.VMEM((tm, D), BF16)],
        compiler_params=_params("parallel", "arbitrary"),
        name="norm_matmul",
    )(x, g, w)


def _attn_a_kernel(slopes_ref, sinks_ref, q_ref, k_ref, v_ref, o_ref, *, n_sub):
    blk = A_BLOCK
    qi = pl.program_id(1)
    scale = HEAD_DIM ** -0.5
    row = lax.broadcasted_iota(jnp.int32, (blk, 2 * blk), 0)
    col = lax.broadcasted_iota(jnp.int32, (blk, 2 * blk), 1)
    dist = row + blk - col
    in_window = (dist >= 0) & (dist < A_WINDOW)
    distf = dist.astype(F32)
    lo_half = lax.broadcasted_iota(jnp.int32, (blk, LANES), 1) < HEAD_DIM
    group = A_HEADS // A_KV_HEADS

    def sub_block(sb, carry):
        gb = qi * n_sub + sb
        own = pl.multiple_of(gb * blk, blk)
        prev = pl.multiple_of(jnp.maximum(gb - 1, 0) * blk, blk)
        rows = pl.ds(pl.multiple_of(sb * blk, blk), blk)
        kk = jnp.concatenate([k_ref[pl.ds(prev, blk), :], k_ref[pl.ds(own, blk), :]], axis=0)
        vv = jnp.concatenate([v_ref[pl.ds(prev, blk), :], v_ref[pl.ds(own, blk), :]], axis=0)
        k_by_half = (kk.astype(BF16), pltpu.roll(kk, HEAD_DIM, axis=1).astype(BF16))
        v_by_half = (vv.astype(BF16), pltpu.roll(vv, HEAD_DIM, axis=1).astype(BF16))
        valid = in_window & ((col >= blk) | (gb > 0))
        for pair in range(A_HEADS // 2):
            qp = q_ref[rows, pair * LANES:(pair + 1) * LANES] * scale
            outs = []
            for half in range(2):
                j = 2 * pair + half
                swapped = (j // group) != half
                qm = jnp.where(lo_half if half == 0 else ~lo_half, qp, 0.0).astype(BF16)
                s = _dot_nt(qm, k_by_half[swapped])
                s = jnp.where(valid, s - slopes_ref[j] * distf, NEG_INF)
                sink = sinks_ref[j]
                m = jnp.maximum(jnp.max(s, axis=-1, keepdims=True), sink)
                p = jnp.exp(s - m)
                denom = jnp.sum(p, axis=-1, keepdims=True) + jnp.exp(sink - m)
                o = jnp.dot(p.astype(BF16), v_by_half[swapped], preferred_element_type=F32)
                outs.append(o / denom)
            o_ref[rows, pair * LANES:(pair + 1) * LANES] = jnp.where(lo_half, outs[0], outs[1])
        return carry

    lax.fori_loop(0, n_sub, sub_block, 0)


def attn_a(proj, sinks, *, q_col, k_col, v_col, tq):
    B, S, _ = proj.shape
    qw = A_HEADS * HEAD_DIM
    assert S % tq == 0 and tq % A_BLOCK == 0
    assert q_col % qw == 0 and k_col % LANES == 0 and v_col % LANES == 0
    kern = functools.partial(_attn_a_kernel, n_sub=tq // A_BLOCK)
    return pl.pallas_call(
        kern,
        grid=(B, S // tq),
        in_specs=[_smem_spec(), _smem_spec(),
                  pl.BlockSpec((None, tq, qw), lambda b, i: (b, i, q_col // qw)),
                  pl.BlockSpec((None, S, LANES), lambda b, i: (b, 0, k_col // LANES)),
                  pl.BlockSpec((None, S, LANES), lambda b, i: (b, 0, v_col // LANES))],
        out_specs=pl.BlockSpec((None, tq, qw), lambda b, i: (b, i, 0)),
        out_shape=jax.ShapeDtypeStruct((B, S, qw), F32),
        compiler_params=_params("parallel", "arbitrary"),
        name="attn_swa",
    )(_alibi_slopes(A_HEADS), sinks, proj, proj, proj)


def _attn_b_kernel(slopes_ref, q_ref, k_ref, v_ref, o_ref,
                   kb_ref, vt_ref, kmean_ref, bias_ref, qs_ref, sel_ref, acc_ref, m_ref, *, nb):
    mb = MOBA_BLOCK
    hp = pl.program_id(1)
    i = pl.program_id(2)
    kpos = lax.broadcasted_iota(jnp.int32, (mb, mb), 0)
    qpos = lax.broadcasted_iota(jnp.int32, (mb, mb), 1)
    slope2 = [slopes_ref[2 * hp + h] * LOG2E for h in range(2)]

    @pl.when(i == 0)
    def _():
        relf = (qpos - kpos).astype(F32)
        for h in range(2):
            bias_ref[h] = (-slope2[h]) * relf
        ones = jnp.ones((BF16_ROWS, mb), BF16)

        def prep(n, carry):
            rows = pl.ds(pl.multiple_of(n * mb, mb), mb)
            kblk = k_ref[rows, :]
            kb_ref[n] = kblk.astype(BF16)
            kmean_ref[pl.ds(n, 1), :] = jnp.mean(kblk, axis=0, keepdims=True)
            vt = v_ref[rows, :].T
            for h in range(2):
                vt_ref[h, n, :HEAD_DIM, :] = vt[h * HEAD_DIM:(h + 1) * HEAD_DIM].astype(BF16)
                vt_ref[h, n, HEAD_DIM:, :] = ones
            return carry

        lax.fori_loop(0, nb, prep, 0)

    q = q_ref[...]
    lo_half = lax.broadcasted_iota(jnp.int32, (mb, LANES), 1) < HEAD_DIM
    blk = lax.broadcasted_iota(jnp.int32, (nb, mb), 0)
    past = blk < i
    k_own = kb_ref[i]

    for h in range(2):
        qh = jnp.where(lo_half if h == 0 else ~lo_half, q, 0.0)
        g = _dot_nt(kmean_ref[...], qh, precision=lax.Precision.HIGHEST)
        g = jnp.where(past, g, NEG_INF)
        avail = blk >= 0
        sel = jnp.zeros((nb, mb), F32)
        for _ in range(MOBA_TOPK):
            mx = jnp.max(jnp.where(avail, g, NEG_INF), axis=0, keepdims=True)
            is_max = avail & (g == mx)
            first = jnp.min(jnp.where(is_max, blk, nb), axis=0, keepdims=True)
            pick = blk == first
            sel = jnp.where(pick & past, 1.0, sel)
            avail = avail & ~pick
        sel_ref[h] = sel
        qs = (qh * (HEAD_DIM ** -0.5 * LOG2E)).astype(BF16)
        qs_ref[h] = qs
        s = _dot_nt(k_own, qs) + bias_ref[h]
        s = jnp.where(qpos >= kpos, s, NEG_INF)
        m = jnp.max(s, axis=0, keepdims=True)
        p = jnp.exp2(s - m).astype(BF16)
        m_ref[h] = m
        acc_ref[h] = jnp.dot(vt_ref[h, i], p, preferred_element_type=F32)

    def past_block(j, carry):
        kj = kb_ref[j]
        dist0 = ((i - j) * mb).astype(F32)
        scores = [_dot_nt(kj, qs_ref[h]) for h in range(2)]
        for h in range(2):
            shift = -slope2[h] * dist0
            s = scores[h] + bias_ref[h]
            chosen = sel_ref[h, pl.ds(j, 1), :] > 0.5
            m_tile = jnp.where(chosen, jnp.max(s, axis=0, keepdims=True) + shift, NEG_INF)
            m_old = m_ref[h]
            m_new = jnp.maximum(m_old, m_tile)
            alpha = jnp.exp2(m_old - m_new)
            p = jnp.exp2(s - jnp.where(chosen, m_new - shift, POS_INF)).astype(BF16)
            acc_ref[h] = alpha * acc_ref[h] + jnp.dot(vt_ref[h, j], p, preferred_element_type=F32)
            m_ref[h] = m_new
        return carry

    lax.fori_loop(0, i, past_block, 0)
    outs = []
    for h in range(2):
        acc = acc_ref[h]
        outs.append(acc[:HEAD_DIM] * (1.0 / acc[HEAD_DIM:HEAD_DIM + 1]))
    o_ref[...] = jnp.concatenate(outs, axis=0).T


def attn_b(proj, *, q_col, k_col, v_col):
    B, S, _ = proj.shape
    mb = MOBA_BLOCK
    assert S % mb == 0
    nb = S // mb
    assert MOBA_TOPK < nb and nb % 8 == 0
    pairs = B_HEADS // 2
    kern = functools.partial(_attn_b_kernel, nb=nb)
    return pl.pallas_call(
        kern,
        grid=(B, pairs, nb),
        in_specs=[_smem_spec(),
                  pl.BlockSpec((None, mb, LANES), lambda b, p, i: (b, i, q_col // LANES + p)),
                  pl.BlockSpec((None, S, LANES), lambda b, p, i: (b, 0, k_col // LANES + p)),
                  pl.BlockSpec((None, S, LANES), lambda b, p, i: (b, 0, v_col // LANES + p))],
        out_specs=pl.BlockSpec((None, mb, LANES), lambda b, p, i: (b, i, p)),
        out_shape=jax.ShapeDtypeStruct((B, S, B_HEADS * HEAD_DIM), F32),
        scratch_shapes=[pltpu.VMEM((nb, mb, LANES), BF16),
                        pltpu.VMEM((2, nb, VT_ROWS, mb), BF16),
                        pltpu.VMEM((nb, LANES), F32),
                        pltpu.VMEM((2, mb, mb), F32),
                        pltpu.VMEM((2, mb, LANES), BF16),
                        pltpu.VMEM((2, nb, mb), F32),
                        pltpu.VMEM((2, VT_ROWS, mb), F32),
                        pltpu.VMEM((2, 1, mb), F32)],
        compiler_params=_params("parallel", "parallel", "arbitrary"),
        name="attn_moba",
    )(_alibi_slopes(B_HEADS), proj, proj, proj)


def _attn_c_kernel(slopes_ref, q_ref, k_ref, v_ref, o_ref,
                   kb_ref, vt_ref, bias_ref, orun_ref, lse_ref, *, seq, tiles_per_iter):
    n = C_STEPS
    hp = pl.program_id(1)
    ntile = seq // n
    kk = lax.broadcasted_iota(jnp.int32, (2 * n, n), 0)
    qq = lax.broadcasted_iota(jnp.int32, (2 * n, n), 1)
    step = qq + n - kk
    valid = (step >= 0) & (step <= n)
    stepf = step.astype(F32)
    lo_half = lax.broadcasted_iota(jnp.int32, (n, LANES), 1) < HEAD_DIM
    ones = jnp.ones((BF16_ROWS, 2 * n), BF16)

    for bi, (w, d) in enumerate(C_PATTERNS):
        nblk = seq // w
        first, last = bi == 0, bi == len(C_PATTERNS) - 1

        def rows_at(t, d=d, w=w, nblk=nblk):
            start = (t % nblk) * w + t // nblk
            return pl.ds(start, n) if d == 1 else pl.ds(start, n, stride=d)

        for h in range(2):
            full = jnp.where(valid, (-slopes_ref[2 * hp + h] * (LOG2E * d)) * stepf, NEG_INF)
            bias_ref[h, 1] = full
            bias_ref[h, 0] = jnp.where(kk >= n, full, NEG_INF)

        def prep(t, carry, rows_at=rows_at):
            kb_ref[t] = k_ref[rows_at(t), :].astype(BF16)
            vt_ref[t] = v_ref[rows_at(t), :].T.astype(BF16)
            return carry

        lax.fori_loop(0, ntile, prep, 0)

        def tile(t, rows_at=rows_at, nblk=nblk, first=first, last=last):
            has_prev = (t % nblk) > 0
            tp = jnp.where(has_prev, t - 1, t)
            variant = has_prev.astype(jnp.int32)
            idx = rows_at(t)
            q = q_ref[idx, :] * (HEAD_DIM ** -0.5 * LOG2E)
            kcat = jnp.concatenate([kb_ref[tp], kb_ref[t]], axis=0)
            vtcat = jnp.concatenate([vt_ref[tp], vt_ref[t]], axis=1)
            o_rows, lse_rows = [], []
            for h in range(2):
                qm = jnp.where(lo_half if h == 0 else ~lo_half, q, 0.0).astype(BF16)
                s = _dot_nt(kcat, qm) + bias_ref[h, variant]
                m = jnp.max(s, axis=0, keepdims=True)
                p = jnp.exp2(s - m).astype(BF16)
                vaug = jnp.concatenate([vtcat[h * HEAD_DIM:(h + 1) * HEAD_DIM], ones], axis=0)
                acc = jnp.dot(vaug, p, preferred_element_type=F32)
                l = acc[HEAD_DIM:HEAD_DIM + 1]
                o_rows.append(acc[:HEAD_DIM] * (1.0 / l))
                lse_rows.append(jnp.broadcast_to(m + jnp.log2(l), (HEAD_DIM, n)))
            o_t = jnp.concatenate(o_rows, axis=0).T
            lse_t = jnp.concatenate(lse_rows, axis=0).T
            if first:
                orun_ref[idx, :] = o_t
                lse_ref[idx, :] = lse_t
            else:
                lse_old = lse_ref[idx, :]
                top = jnp.maximum(lse_old, lse_t)
                w_old = jnp.exp2(lse_old - top)
                w_t = jnp.exp2(lse_t - top)
                den = w_old + w_t
                o_new = (w_old * orun_ref[idx, :] + w_t * o_t) * (1.0 / den)
                if last:
                    o_ref[idx, :] = o_new
                else:
                    orun_ref[idx, :] = o_new
                    lse_ref[idx, :] = top + jnp.log2(den)

        def tiles(u, carry, tile=tile):
            for sub in range(tiles_per_iter):
                tile(u * tiles_per_iter + sub)
            return carry

        lax.fori_loop(0, ntile // tiles_per_iter, tiles, 0)


def attn_c(proj, *, q_col, k_col, v_col):
    B, S, _ = proj.shape
    n = C_STEPS
    assert all(S % w == 0 and w // d == n for w, d in C_PATTERNS)
    tiles_per_iter = 2
    assert (S // n) % tiles_per_iter == 0
    pairs = C_HEADS // 2
    kern = functools.partial(_attn_c_kernel, seq=S, tiles_per_iter=tiles_per_iter)
    return pl.pallas_call(
        kern,
        grid=(B, pairs),
        in_specs=[_smem_spec(),
                  pl.BlockSpec((None, S, LANES), lambda b, p: (b, 0, q_col // LANES + p)),
                  pl.BlockSpec((None, S, LANES), lambda b, p: (b, 0, k_col // LANES + p)),
                  pl.BlockSpec((None, S, LANES), lambda b, p: (b, 0, v_col // LANES + p))],
        out_specs=pl.BlockSpec((None, S, LANES), lambda b, p: (b, 0, p)),
        out_shape=jax.ShapeDtypeStruct((B, S, C_HEADS * HEAD_DIM), F32),
        scratch_shapes=[pltpu.VMEM((S // n, n, LANES), BF16),
                        pltpu.VMEM((S // n, LANES, n), BF16),
                        pltpu.VMEM((2, 2, 2 * n, n), F32),
                        pltpu.VMEM((S, LANES), F32),
                        pltpu.VMEM((S, LANES), F32)],
        compiler_params=_params("parallel", "parallel"),
        name="attn_dilated",
    )(_alibi_slopes(C_HEADS), proj, proj, proj)


def _mix_out_kernel(oa_ref, ob_ref, oc_ref, ga_ref, gb_ref, gc_ref, h_ref, w_ref, o_ref, y_ref):
    @pl.when(pl.program_id(1) == 0)
    def _():
        wa, wb = oa_ref.shape[1], ob_ref.shape[1]
        y_ref[:, :wa] = _rms(oa_ref[...], ga_ref[...]).astype(BF16)
        y_ref[:, wa:wa + wb] = _rms(ob_ref[...], gb_ref[...]).astype(BF16)
        y_ref[:, wa + wb:] = _rms(oc_ref[...], gc_ref[...]).astype(BF16)

    o_ref[...] = h_ref[...] + jnp.dot(y_ref[...], w_ref[...], preferred_element_type=F32)


def mix_out(oa, ob, oc, ga, gb, gc, h, w, *, tm, tn):
    M, D = h.shape
    wa, wb, wc = oa.shape[1], ob.shape[1], oc.shape[1]
    K = wa + wb + wc
    assert w.shape == (K, D) and M % tm == 0 and D % tn == 0
    row = lambda width: pl.BlockSpec((tm, width), lambda i, j: (i, 0))
    gain = lambda width: pl.BlockSpec((1, width), lambda i, j: (0, 0))
    return pl.pallas_call(
        _mix_out_kernel,
        grid=(M // tm, D // tn),
        in_specs=[row(wa), row(wb), row(wc), gain(wa), gain(wb), gain(wc),
                  pl.BlockSpec((tm, tn), lambda i, j: (i, j)),
                  pl.BlockSpec((K, tn), lambda i, j: (0, j))],
        out_specs=pl.BlockSpec((tm, tn), lambda i, j: (i, j)),
        out_shape=jax.ShapeDtypeStruct((M, D), F32),
        scratch_shapes=[pltpu.VMEM((tm, K), BF16)],
        compiler_params=_params("parallel", "arbitrary"),
        name="mix_out",
    )(oa, ob, oc, ga, gb, gc, h, w)


def _xattn_kernel(h_ref, g_ref, wq_ref, kv_ref, wo_ref, o_ref):
    xw = X_HEADS * X_HEAD_DIM
    h = h_ref[...]
    xn = _rms(h, g_ref[...]).astype(BF16)
    q = jnp.dot(xn, wq_ref[...], preferred_element_type=F32) * (X_HEAD_DIM ** -0.5)
    heads = []
    for hd in range(X_HEADS):
        lanes = slice(hd * X_HEAD_DIM, (hd + 1) * X_HEAD_DIM)
        s = _dot_nt(q[:, lanes].astype(BF16), kv_ref[:, lanes])
        m = jnp.max(s, axis=-1, keepdims=True)
        p = jnp.exp(s - m)
        l = jnp.sum(p, axis=-1, keepdims=True)
        v = kv_ref[:, xw + hd * X_HEAD_DIM:xw + (hd + 1) * X_HEAD_DIM]
        heads.append(jnp.dot(p.astype(BF16), v, preferred_element_type=F32) / l)
    o = jnp.concatenate(heads, axis=-1).astype(BF16)
    o_ref[...] = h + jnp.dot(o, wo_ref[...], preferred_element_type=F32)


def xattn(h, g, wq, kv, wo, *, tq):
    B, S, D = h.shape
    Mm = kv.shape[1]
    xw = X_HEADS * X_HEAD_DIM
    assert S % tq == 0
    return pl.pallas_call(
        _xattn_kernel,
        grid=(B, S // tq),
        in_specs=[pl.BlockSpec((None, tq, D), lambda b, i: (b, i, 0)),
                  pl.BlockSpec((1, D), lambda b, i: (0, 0)),
                  pl.BlockSpec((D, xw), lambda b, i: (0, 0)),
                  pl.BlockSpec((None, Mm, 2 * xw), lambda b, i: (b, 0, 0)),
                  pl.BlockSpec((xw, D), lambda b, i: (0, 0))],
        out_specs=pl.BlockSpec((None, tq, D), lambda b, i: (b, i, 0)),
        out_shape=jax.ShapeDtypeStruct((B, S, D), F32),
        compiler_params=_params("parallel", "parallel"),
        name="xattn",
    )(h, g, wq, kv, wo)


def _ffn_kernel(h_ref, g_ref, wg_ref, wu_ref, wd_ref, gf_ref, o_ref, xn_ref, acc_ref, *, final_norm):
    f = pl.program_id(1)

    @pl.when(f == 0)
    def _():
        xn_ref[...] = _rms(h_ref[...], g_ref[...]).astype(BF16)
        acc_ref[...] = jnp.zeros_like(acc_ref)

    xn = xn_ref[...]
    gate = jnp.dot(xn, wg_ref[...], preferred_element_type=F32)
    up = jnp.dot(xn, wu_ref[...], preferred_element_type=F32)
    act = (gate / (1.0 + jnp.exp(-gate))) * up
    acc_ref[...] += jnp.dot(act.astype(BF16), wd_ref[...], preferred_element_type=F32)

    @pl.when(f == pl.num_programs(1) - 1)
    def _():
        y = h_ref[...] + acc_ref[...]
        o_ref[...] = _rms(y, gf_ref[...]) if final_norm else y


def ffn(h, g, wg, wu, wd, gf, *, tm, tf, final_norm):
    M, D = h.shape
    Fd = wg.shape[1]
    assert M % tm == 0 and Fd % tf == 0
    kern = functools.partial(_ffn_kernel, final_norm=final_norm)
    return pl.pallas_call(
        kern,
        grid=(M // tm, Fd // tf),
        in_specs=[pl.BlockSpec((tm, D), lambda i, f: (i, 0)),
                  pl.BlockSpec((1, D), lambda i, f: (0, 0)),
                  pl.BlockSpec((D, tf), lambda i, f: (0, f)),
                  pl.BlockSpec((D, tf), lambda i, f: (0, f)),
                  pl.BlockSpec((tf, D), lambda i, f: (f, 0)),
                  pl.BlockSpec((1, D), lambda i, f: (0, 0))],
        out_specs=pl.BlockSpec((tm, D), lambda i, f: (i, 0)),
        out_shape=jax.ShapeDtypeStruct((M, D), F32),
        scratch_shapes=[pltpu.VMEM((tm, D), BF16), pltpu.VMEM((tm, D), F32)],
        compiler_params=_params("parallel", "arbitrary"),
        name="ffn",
    )(h, g, wg, wu, wd, gf)


def kernel(x, mem, norm_mix, w_in, sinks, gain_a, gain_b, gain_c, w_out, norm_xattn, norm_mem,
           wq_x, wk_x, wv_x, wo_x, norm_ffn, w_gate, w_up, w_down, norm_final):
    B, S, D = x.shape
    depth = w_in.shape[0]
    Mm = mem.shape[1]
    qa_w, kva_w = A_HEADS * HEAD_DIM, A_KV_HEADS * HEAD_DIM
    b_w, c_w = B_HEADS * HEAD_DIM, C_HEADS * HEAD_DIM
    widths = (qa_w, kva_w, kva_w, b_w, b_w, b_w, c_w, c_w, c_w)
    cols = [int(c) for c in np.concatenate([[0], np.cumsum(widths)[:-1]])]
    in_w = sum(widths)

    row = lambda v: v.reshape(1, -1)
    h = x.reshape(B * S, D)
    mem2 = mem.reshape(B * Mm, D)
    for l in range(depth):
        proj = norm_matmul(h, row(norm_mix[l]), w_in[l].astype(BF16), tm=512, tn=768, out_dtype=F32)
        proj = proj.reshape(B, S, in_w)
        oa = attn_a(proj, sinks[l], q_col=cols[0], k_col=cols[1], v_col=cols[2], tq=512)
        ob = attn_b(proj, q_col=cols[3], k_col=cols[4], v_col=cols[5])
        oc = attn_c(proj, q_col=cols[6], k_col=cols[7], v_col=cols[8])
        h = mix_out(oa.reshape(B * S, qa_w), ob.reshape(B * S, b_w), oc.reshape(B * S, c_w),
                    row(gain_a[l]), row(gain_b[l]), row(gain_c[l]), h, w_out[l].astype(BF16),
                    tm=512, tn=1024)
        w_kv = jnp.concatenate([wk_x[l], wv_x[l]], axis=1).astype(BF16)
        kv = norm_matmul(mem2, row(norm_mem[l]), w_kv, tm=512, tn=w_kv.shape[1], out_dtype=BF16)
        h = xattn(h.reshape(B, S, D), row(norm_xattn[l]), wq_x[l].astype(BF16),
                  kv.reshape(B, Mm, -1), wo_x[l].astype(BF16), tq=512).reshape(B * S, D)
        h = ffn(h, row(norm_ffn[l]), w_gate[l].astype(BF16), w_up[l].astype(BF16),
                w_down[l].astype(BF16), row(norm_final), tm=512, tf=512,
                final_norm=(l == depth - 1))
    return h.reshape(B, S, D)
```

```python
import functools

import numpy as np
import jax
import jax.numpy as jnp
from jax import lax
from jax.experimental import pallas as pl
from jax.experimental.pallas import tpu as pltpu

F32 = jnp.float32
BF16 = jnp.bfloat16
NEG_INF = float("-inf")
POS_INF = float("inf")
LOG2E = 1.4426950408889634

EPS = 1e-5
HEAD_DIM = 64
LANES = 128
BF16_ROWS = 16
A_HEADS, A_KV_HEADS, A_WINDOW, A_BLOCK = 8, 2, 128, 128
B_HEADS, MOBA_BLOCK, MOBA_TOPK = 8, 256, 3
C_HEADS = 16
C_PATTERNS = ((128, 1), (512, 4), (2048, 16))
C_STEPS = 128
C_GROUP = 8
X_HEADS, X_HEAD_DIM = 4, 128
VMEM_LIMIT = 56 * 1024 * 1024
VT_ROWS = HEAD_DIM + BF16_ROWS


def _alibi_slopes(n):
    return jnp.asarray(2.0 ** (-8.0 * np.arange(1, n + 1) / n), F32)


def _params(*sem):
    return pltpu.CompilerParams(dimension_semantics=sem, vmem_limit_bytes=VMEM_LIMIT)


def _rms(x, g):
    return x * lax.rsqrt(jnp.mean(x * x, axis=-1, keepdims=True) + EPS) * g


def _dot_nt(a, b, precision=None):
    return lax.dot_general(a, b, (((1,), (1,)), ((), ())), precision=precision,
                           preferred_element_type=F32)


def _smem_spec():
    return pl.BlockSpec(memory_space=pltpu.SMEM)


def _zero_after(x):
    bits = pltpu.bitcast(x, jnp.uint32)
    return pltpu.bitcast(lax.shift_right_logical(bits, jnp.uint32(32)), F32)


def _norm_matmul_kernel(x_ref, g_ref, w_ref, o_ref, xn_ref):
    @pl.when(pl.program_id(1) == 0)
    def _():
        xn_ref[...] = _rms(x_ref[...], g_ref[...]).astype(BF16)

    o_ref[...] = jnp.dot(xn_ref[...], w_ref[...], preferred_element_type=F32).astype(o_ref.dtype)


def norm_matmul(x, g, w, *, tm, tn, out_dtype):
    M, D = x.shape
    N = w.shape[1]
    assert M % tm == 0 and N % tn == 0
    return pl.pallas_call(
        _norm_matmul_kernel,
        grid=(M // tm, N // tn),
        in_specs=[pl.BlockSpec((tm, D), lambda i, j: (i, 0)),
                  pl.BlockSpec((1, D), lambda i, j: (0, 0)),
                  pl.BlockSpec((D, tn), lambda i, j: (0, j))],
        out_specs=pl.BlockSpec((tm, tn), lambda i, j: (i, j)),
        out_shape=jax.ShapeDtypeStruct((M, N), out_dtype),
        scratch_shapes=[pltpu.VMEM((tm, D), BF16)],
        compiler_params=_params("parallel", "arbitrary"),
        name="norm_matmul",
    )(x, g, w)


def _attn_a_kernel(slopes_ref, sinks_ref, q_ref, k_ref, v_ref, o_ref, *, n_sub):
    blk = A_BLOCK
    qi = pl.program_id(1)
    scale = HEAD_DIM ** -0.5
    row = lax.broadcasted_iota(jnp.int32, (blk, 2 * blk), 0)
    col = lax.broadcasted_iota(jnp.int32, (blk, 2 * blk), 1)
    dist = row + blk - col
    in_window = (dist >= 0) & (dist < A_WINDOW)
    distf = dist.astype(F32)
    lo_half = lax.broadcasted_iota(jnp.int32, (blk, LANES), 1) < HEAD_DIM
    group = A_HEADS // A_KV_HEADS

    def sub_block(sb, carry):
        gb = qi * n_sub + sb
        own = pl.multiple_of(gb * blk, blk)
        prev = pl.multiple_of(jnp.maximum(gb - 1, 0) * blk, blk)
        rows = pl.ds(pl.multiple_of(sb * blk, blk), blk)
        kk = jnp.concatenate([k_ref[pl.ds(prev, blk), :], k_ref[pl.ds(own, blk), :]], axis=0)
        vv = jnp.concatenate([v_ref[pl.ds(prev, blk), :], v_ref[pl.ds(own, blk), :]], axis=0)
        k_by_half = (kk.astype(BF16), pltpu.roll(kk, HEAD_DIM, axis=1).astype(BF16))
        v_by_half = (vv.astype(BF16), pltpu.roll(vv, HEAD_DIM, axis=1).astype(BF16))
        valid = in_window & ((col >= blk) | (gb > 0))
        for pair in range(A_HEADS // 2):
            qp = q_ref[rows, pair * LANES:(pair + 1) * LANES] * scale
            outs = []
            for half in range(2):
                j = 2 * pair + half
                swapped = (j // group) != half
                qm = jnp.where(lo_half if half == 0 else ~lo_half, qp, 0.0).astype(BF16)
                s = _dot_nt(qm, k_by_half[swapped])
                s = jnp.where(valid, s - slopes_ref[j] * distf, NEG_INF)
                sink = sinks_ref[j]
                m = jnp.maximum(jnp.max(s, axis=-1, keepdims=True), sink)
                p = jnp.exp(s - m)
                denom = jnp.sum(p, axis=-1, keepdims=True) + jnp.exp(sink - m)
                o = jnp.dot(p.astype(BF16), v_by_half[swapped], preferred_element_type=F32)
                outs.append(o / denom)
            o_ref[rows, pair * LANES:(pair + 1) * LANES] = jnp.where(lo_half, outs[0], outs[1])
        return carry

    lax.fori_loop(0, n_sub, sub_block, 0)


def attn_a(proj, sinks, *, q_col, k_col, v_col, tq):
    B, S, _ = proj.shape
    qw = A_HEADS * HEAD_DIM
    assert S % tq == 0 and tq % A_BLOCK == 0
    assert q_col % qw == 0 and k_col % LANES == 0 and v_col % LANES == 0
    kern = functools.partial(_attn_a_kernel, n_sub=tq // A_BLOCK)
    return pl.pallas_call(
        kern,
        grid=(B, S // tq),
        in_specs=[_smem_spec(), _smem_spec(),
                  pl.BlockSpec((None, tq, qw), lambda b, i: (b, i, q_col // qw)),
                  pl.BlockSpec((None, S, LANES), lambda b, i: (b, 0, k_col // LANES)),
                  pl.BlockSpec((None, S, LANES), lambda b, i: (b, 0, v_col // LANES))],
        out_specs=pl.BlockSpec((None, tq, qw), lambda b, i: (b, i, 0)),
        out_shape=jax.ShapeDtypeStruct((B, S, qw), F32),
        compiler_params=_params("parallel", "arbitrary"),
        name="attn_swa",
    )(_alibi_slopes(A_HEADS), sinks, proj, proj, proj)


def _attn_b_kernel(slopes_ref, q_ref, k_ref, v_ref, o_ref,
                   kb_ref, vt_ref, kmean_ref, kcat_ref, bias_ref, qs_ref, sel_ref, acc_ref, m_ref, s_ref,
                   *, nb):
    mb = MOBA_BLOCK
    hp = pl.program_id(1)
    i = pl.program_id(2)
    slope2 = [slopes_ref[2 * hp + h] * LOG2E for h in range(2)]

    @pl.when(i == 0)
    def _():
        kpos = lax.broadcasted_iota(jnp.int32, (mb, mb), 0)
        qpos = lax.broadcasted_iota(jnp.int32, (mb, mb), 1)
        relf = (qpos - kpos).astype(F32)
        for h in range(2):
            alibi = (-slope2[h]) * relf
            bias_ref[h, 0] = alibi
            bias_ref[h, 1] = jnp.where(qpos >= kpos, alibi, NEG_INF)
        ones = jnp.ones((BF16_ROWS, mb), BF16)

        def prep(n, carry):
            rows = pl.ds(pl.multiple_of(n * mb, mb), mb)
            kblk = k_ref[rows, :]
            kb_ref[n] = kblk.astype(BF16)
            kmean_ref[pl.ds(n, 1), :] = jnp.mean(kblk, axis=0, keepdims=True)
            vt = v_ref[rows, :].T
            for h in range(2):
                vt_ref[h, n, :HEAD_DIM, :] = vt[h * HEAD_DIM:(h + 1) * HEAD_DIM].astype(BF16)
                vt_ref[h, n, HEAD_DIM:, :] = ones
            return carry

        lax.fori_loop(0, nb, prep, 0)
        km = kmean_ref[...]
        lo_lanes = lax.broadcasted_iota(jnp.int32, (nb, LANES), 1) < HEAD_DIM
        for h in range(2):
            kmh = jnp.where(lo_lanes if h == 0 else ~lo_lanes, km, 0.0)
            hi = kmh.astype(BF16)
            lo = (kmh - hi.astype(F32)).astype(BF16)
            kcat_ref[h * nb:(h + 1) * nb, :] = jnp.concatenate([hi, hi, lo], axis=1)

    q = q_ref[...]
    lo_half = lax.broadcasted_iota(jnp.int32, (mb, LANES), 1) < HEAD_DIM
    blk = lax.broadcasted_iota(jnp.int32, (nb, mb), 0)
    past = blk < i
    q_hi = q.astype(BF16)
    q_lo = (q - q_hi.astype(F32)).astype(BF16)
    gate = _dot_nt(kcat_ref[...], jnp.concatenate([q_hi, q_lo, q_hi], axis=1))

    for h in range(2):
        g = jnp.where(past, gate[h * nb:(h + 1) * nb], NEG_INF)
        avail = blk >= 0
        sel = jnp.where(blk == i, 1.0, 0.0)
        for _ in range(MOBA_TOPK):
            mx = jnp.max(jnp.where(avail, g, NEG_INF), axis=0, keepdims=True)
            is_max = avail & (g == mx)
            first = jnp.min(jnp.where(is_max, blk, nb), axis=0, keepdims=True)
            pick = blk == first
            sel = jnp.where(pick & past, 1.0, sel)
            avail = avail & ~pick
        sel_ref[h] = sel

    qc = q * (HEAD_DIM ** -0.5 * LOG2E)
    q_both = jnp.concatenate([jnp.where(lo_half, qc, 0.0), jnp.where(lo_half, 0.0, qc)], axis=0).astype(BF16)
    qs_ref[...] = q_both
    m_ref[...] = jnp.full(m_ref.shape, NEG_INF, F32)
    acc_ref[...] = jnp.zeros(acc_ref.shape, F32)
    s_ref[...] = _dot_nt(kb_ref[i], q_both)

    def step(t, carry):
        is_own = t == 0
        j = jnp.where(is_own, i, t - 1)
        variant = jnp.where(is_own, 1, 0)
        dist0 = ((i - j) * mb).astype(F32)
        s_next = _dot_nt(kb_ref[t], qs_ref[...])
        for h in range(2):
            shift = -slope2[h] * dist0
            s = s_ref[:, h * mb:(h + 1) * mb] + bias_ref[h, variant]
            chosen = sel_ref[h, pl.ds(j, 1), :] > 0.5
            m_tile = jnp.where(chosen, jnp.max(s, axis=0, keepdims=True) + shift, NEG_INF)
            m_old = m_ref[h]
            m_new = jnp.maximum(m_old, m_tile)
            alpha = jnp.exp2(m_old - m_new)
            p = jnp.exp2(s - jnp.where(chosen, m_new - shift, POS_INF)).astype(BF16)
            acc_ref[h] = alpha * acc_ref[h] + jnp.dot(vt_ref[h, j], p, preferred_element_type=F32)
            m_ref[h] = m_new
        s_ref[...] = s_next
        return carry

    lax.fori_loop(0, i + 1, step, 0)
    outs = []
    for h in range(2):
        acc = acc_ref[h]
        outs.append(acc[:HEAD_DIM] * (1.0 / acc[HEAD_DIM:HEAD_DIM + 1]))
    o_ref[...] = jnp.concatenate(outs, axis=0).T


def attn_b(proj, *, q_col, k_col, v_col):
    B, S, _ = proj.shape
    mb = MOBA_BLOCK
    assert S % mb == 0
    nb = S // mb
    assert MOBA_TOPK < nb and nb % 8 == 0
    pairs = B_HEADS // 2
    kern = functools.partial(_attn_b_kernel, nb=nb)
    return pl.pallas_call(
        kern,
        grid=(B, pairs, nb),
        in_specs=[_smem_spec(),
                  pl.BlockSpec((None, mb, LANES), lambda b, p, i: (b, i, q_col // LANES + p)),
                  pl.BlockSpec((None, S, LANES), lambda b, p, i: (b, 0, k_col // LANES + p)),
                  pl.BlockSpec((None, S, LANES), lambda b, p, i: (b, 0, v_col // LANES + p))],
        out_specs=pl.BlockSpec((None, mb, LANES), lambda b, p, i: (b, i, p)),
        out_shape=jax.ShapeDtypeStruct((B, S, B_HEADS * HEAD_DIM), F32),
        scratch_shapes=[pltpu.VMEM((nb, mb, LANES), BF16),
                        pltpu.VMEM((2, nb, VT_ROWS, mb), BF16),
                        pltpu.VMEM((nb, LANES), F32),
                        pltpu.VMEM((2 * nb, 3 * LANES), BF16),
                        pltpu.VMEM((2, 2, mb, mb), F32),
                        pltpu.VMEM((2 * mb, LANES), BF16),
                        pltpu.VMEM((2, nb, mb), F32),
                        pltpu.VMEM((2, VT_ROWS, mb), F32),
                        pltpu.VMEM((2, 1, mb), F32),
                        pltpu.VMEM((mb, 2 * mb), F32)],
        compiler_params=_params("parallel", "parallel", "arbitrary"),
        name="attn_moba",
    )(_alibi_slopes(B_HEADS), proj, proj, proj)


def _attn_c_kernel(slopes_ref, q_ref, k_ref, v_ref, o_ref,
                   kb_ref, vt_ref, bias_ref, orun_ref, lse_ref, *, seq, group):
    n = C_STEPS
    hp = pl.program_id(1)
    ntile = seq // n
    kk = lax.broadcasted_iota(jnp.int32, (2 * n, n), 0)
    qq = lax.broadcasted_iota(jnp.int32, (2 * n, n), 1)
    step_i = qq + n - kk
    valid = (step_i >= 0) & (step_i <= n)
    stepf = step_i.astype(F32)
    lo_half = lax.broadcasted_iota(jnp.int32, (n, LANES), 1) < HEAD_DIM
    ones = jnp.ones((BF16_ROWS, 2 * n), BF16)

    for bi, (w, d) in enumerate(C_PATTERNS):
        nblk = seq // w
        first, last = bi == 0, bi == len(C_PATTERNS) - 1

        def rows_at(t, d=d, w=w, nblk=nblk):
            start = (t % nblk) * w + t // nblk
            return pl.ds(start, n) if d == 1 else pl.ds(start, n, stride=d)

        for h in range(2):
            full = jnp.where(valid, (-slopes_ref[2 * hp + h] * (LOG2E * d)) * stepf, NEG_INF)
            bias_ref[h, 1] = full
            bias_ref[h, 0] = jnp.where(kk >= n, full, NEG_INF)

        def prep(u, carry, rows_at=rows_at):
            for sub in range(group):
                t = u * group + sub
                kb_ref[t] = k_ref[rows_at(t), :].astype(BF16)
                vt_ref[t] = v_ref[rows_at(t), :].T.astype(BF16)
            return carry

        lax.fori_loop(0, ntile // group, prep, 0)

        def tiles(u, carry, rows_at=rows_at, nblk=nblk, first=first, last=last):
            ts = [u * group + sub for sub in range(group)]
            prevs, variants, scores = [], [], []
            for t in ts:
                has_prev = (t % nblk) > 0
                tp = jnp.where(has_prev, t - 1, t)
                prevs.append(tp)
                variants.append(jnp.where(has_prev, 1, 0))
                q = q_ref[rows_at(t), :] * (HEAD_DIM ** -0.5 * LOG2E)
                q_both = jnp.concatenate([jnp.where(lo_half, q, 0.0), jnp.where(lo_half, 0.0, q)], axis=0)
                kcat = jnp.concatenate([kb_ref[tp], kb_ref[t]], axis=0)
                scores.append(_dot_nt(kcat, q_both.astype(BF16)))
            anchor = sum(_zero_after(sc[0:1, :n]) for sc in scores[1:])
            results = []
            for g, t in enumerate(ts):
                vtcat = jnp.concatenate([vt_ref[prevs[g]], vt_ref[t]], axis=1)
                o_rows, lse_rows = [], []
                for h in range(2):
                    s = scores[g][:, h * n:(h + 1) * n] + bias_ref[h, variants[g]]
                    m = jnp.max(s, axis=0, keepdims=True)
                    if g == 0:
                        m = m + anchor
                    p = jnp.exp2(s - m).astype(BF16)
                    vaug = jnp.concatenate([vtcat[h * HEAD_DIM:(h + 1) * HEAD_DIM], ones], axis=0)
                    acc = jnp.dot(vaug, p, preferred_element_type=F32)
                    l = acc[HEAD_DIM:HEAD_DIM + 1]
                    o_rows.append(acc[:HEAD_DIM] * (1.0 / l))
                    lse_rows.append(jnp.broadcast_to(m + jnp.log2(l), (HEAD_DIM, n)))
                results.append((jnp.concatenate(o_rows, axis=0), jnp.concatenate(lse_rows, axis=0)))
            for g, t in enumerate(ts):
                idx = rows_at(t)
                o_t = results[g][0].T
                lse_t = results[g][1].T
                if first:
                    orun_ref[idx, :] = o_t
                    lse_ref[idx, :] = lse_t
                else:
                    lse_old = lse_ref[idx, :]
                    top = jnp.maximum(lse_old, lse_t)
                    w_old = jnp.exp2(lse_old - top)
                    w_t = jnp.exp2(lse_t - top)
                    den = w_old + w_t
                    o_new = (w_old * orun_ref[idx, :] + w_t * o_t) * (1.0 / den)
                    if last:
                        o_ref[idx, :] = o_new
                    else:
                        orun_ref[idx, :] = o_new
                        lse_ref[idx, :] = top + jnp.log2(den)
            return carry

        lax.fori_loop(0, ntile // group, tiles, 0)


def attn_c(proj, *, q_col, k_col, v_col):
    B, S, _ = proj.shape
    n = C_STEPS
    assert all(S % w == 0 and w // d == n for w, d in C_PATTERNS)
    assert (S // n) % C_GROUP == 0
    pairs = C_HEADS // 2
    kern = functools.partial(_attn_c_kernel, seq=S, group=C_GROUP)
    return pl.pallas_call(
        kern,
        grid=(B, pairs),
        in_specs=[_smem_spec(),
                  pl.BlockSpec((None, S, LANES), lambda b, p: (b, 0, q_col // LANES + p)),
                  pl.BlockSpec((None, S, LANES), lambda b, p: (b, 0, k_col // LANES + p)),
                  pl.BlockSpec((None, S, LANES), lambda b, p: (b, 0, v_col // LANES + p))],
        out_specs=pl.BlockSpec((None, S, LANES), lambda b, p: (b, 0, p)),
        out_shape=jax.ShapeDtypeStruct((B, S, C_HEADS * HEAD_DIM), F32),
        scratch_shapes=[pltpu.VMEM((S // n, n, LANES), BF16),
                        pltpu.VMEM((S // n, LANES, n), BF16),
                        pltpu.VMEM((2, 2, 2 * n, n), F32),
                        pltpu.VMEM((S, LANES), F32),
                        pltpu.VMEM((S, LANES), F32)],
        compiler_params=_params("parallel", "parallel"),
        name="attn_dilated",
    )(_alibi_slopes(C_HEADS), proj, proj, proj)


def _mix_out_kernel(oa_ref, ob_ref, oc_ref, ga_ref, gb_ref, gc_ref, h_ref, w_ref, o_ref, y_ref):
    @pl.when(pl.program_id(1) == 0)
    def _():
        wa, wb = oa_ref.shape[1], ob_ref.shape[1]
        y_ref[:, :wa] = _rms(oa_ref[...], ga_ref[...]).astype(BF16)
        y_ref[:, wa:wa + wb] = _rms(ob_ref[...], gb_ref[...]).astype(BF16)
        y_ref[:, wa + wb:] = _rms(oc_ref[...], gc_ref[...]).astype(BF16)

    o_ref[...] = h_ref[...] + jnp.dot(y_ref[...], w_ref[...], preferred_element_type=F32)


def mix_out(oa, ob, oc, ga, gb, gc, h, w, *, tm, tn):
    M, D = h.shape
    wa, wb, wc = oa.shape[1], ob.shape[1], oc.shape[1]
    K = wa + wb + wc
    assert w.shape == (K, D) and M % tm == 0 and D % tn == 0
    row = lambda width: pl.BlockSpec((tm, width), lambda i, j: (i, 0))
    gain = lambda width: pl.BlockSpec((1, width), lambda i, j: (0, 0))
    return pl.pallas_call(
        _mix_out_kernel,
        grid=(M // tm, D // tn),
        in_specs=[row(wa), row(wb), row(wc), gain(wa), gain(wb), gain(wc),
                  pl.BlockSpec((tm, tn), lambda i, j: (i, j)),
                  pl.BlockSpec((K, tn), lambda i, j: (0, j))],
        out_specs=pl.BlockSpec((tm, tn), lambda i, j: (i, j)),
        out_shape=jax.ShapeDtypeStruct((M, D), F32),
        scratch_shapes=[pltpu.VMEM((tm, K), BF16)],
        compiler_params=_params("parallel", "arbitrary"),
        name="mix_out",
    )(oa, ob, oc, ga, gb, gc, h, w)


def _xattn_kernel(h_ref, g_ref, wq_ref, kv_ref, wo_ref, o_ref):
    xw = X_HEADS * X_HEAD_DIM
    h = h_ref[...]
    xn = _rms(h, g_ref[...]).astype(BF16)
    q = jnp.dot(xn, wq_ref[...], preferred_element_type=F32) * (X_HEAD_DIM ** -0.5)
    heads = []
    for hd in range(X_HEADS):
        lanes = slice(hd * X_HEAD_DIM, (hd + 1) * X_HEAD_DIM)
        s = _dot_nt(q[:, lanes].astype(BF16), kv_ref[:, lanes])
        m = jnp.max(s, axis=-1, keepdims=True)
        p = jnp.exp(s - m)
        l = jnp.sum(p, axis=-1, keepdims=True)
        v = kv_ref[:, xw + hd * X_HEAD_DIM:xw + (hd + 1) * X_HEAD_DIM]
        heads.append(jnp.dot(p.astype(BF16), v, preferred_element_type=F32) / l)
    o = jnp.concatenate(heads, axis=-1).astype(BF16)
    o_ref[...] = h + jnp.dot(o, wo_ref[...], preferred_element_type=F32)


def xattn(h, g, wq, kv, wo, *, tq):
    B, S, D = h.shape
    Mm = kv.shape[1]
    xw = X_HEADS * X_HEAD_DIM
    assert S % tq == 0
    return pl.pallas_call(
        _xattn_kernel,
        grid=(B, S // tq),
        in_specs=[pl.BlockSpec((None, tq, D), lambda b, i: (b, i, 0)),
                  pl.BlockSpec((1, D), lambda b, i: (0, 0)),
                  pl.BlockSpec((D, xw), lambda b, i: (0, 0)),
                  pl.BlockSpec((None, Mm, 2 * xw), lambda b, i: (b, 0, 0)),
                  pl.BlockSpec((xw, D), lambda b, i: (0, 0))],
        out_specs=pl.BlockSpec((None, tq, D), lambda b, i: (b, i, 0)),
        out_shape=jax.ShapeDtypeStruct((B, S, D), F32),
        compiler_params=_params("parallel", "parallel"),
        name="xattn",
    )(h, g, wq, kv, wo)


def _ffn_kernel(h_ref, g_ref, wg_ref, wu_ref, wd_ref, gf_ref, o_ref, xn_ref, acc_ref, *, final_norm):
    f = pl.program_id(1)

    @pl.when(f == 0)
    def _():
        xn_ref[...] = _rms(h_ref[...], g_ref[...]).astype(BF16)
        acc_ref[...] = jnp.zeros_like(acc_ref)

    xn = xn_ref[...]
    gate = jnp.dot(xn, wg_ref[...], preferred_element_type=F32)
    up = jnp.dot(xn, wu_ref[...], preferred_element_type=F32)
    act = (gate / (1.0 + jnp.exp(-gate))) * up
    acc_ref[...] += jnp.dot(act.astype(BF16), wd_ref[...], preferred_element_type=F32)

    @pl.when(f == pl.num_programs(1) - 1)
    def _():
        y = h_ref[...] + acc_ref[...]
        o_ref[...] = _rms(y, gf_ref[...]) if final_norm else y


def ffn(h, g, wg, wu, wd, gf, *, tm, tf, final_norm):
    M, D = h.shape
    Fd = wg.shape[1]
    assert M % tm == 0 and Fd % tf == 0
    kern = functools.partial(_ffn_kernel, final_norm=final_norm)
    return pl.pallas_call(
        kern,
        grid=(M // tm, Fd // tf),
        in_specs=[pl.BlockSpec((tm, D), lambda i, f: (i, 0)),
                  pl.BlockSpec((1, D), lambda i, f: (0, 0)),
                  pl.BlockSpec((D, tf), lambda i, f: (0, f)),
                  pl.BlockSpec((D, tf), lambda i, f: (0, f)),
                  pl.BlockSpec((tf, D), lambda i, f: (f, 0)),
                  pl.BlockSpec((1, D), lambda i, f: (0, 0))],
        out_specs=pl.BlockSpec((tm, D), lambda i, f: (i, 0)),
        out_shape=jax.ShapeDtypeStruct((M, D), F32),
        scratch_shapes=[pltpu.VMEM((tm, D), BF16), pltpu.VMEM((tm, D), F32)],
        compiler_params=_params("parallel", "arbitrary"),
        name="ffn",
    )(h, g, wg, wu, wd, gf)


def kernel(x, mem, norm_mix, w_in, sinks, gain_a, gain_b, gain_c, w_out, norm_xattn, norm_mem,
           wq_x, wk_x, wv_x, wo_x, norm_ffn, w_gate, w_up, w_down, norm_final):
    B, S, D = x.shape
    depth = w_in.shape[0]
    Mm = mem.shape[1]
    qa_w, kva_w = A_HEADS * HEAD_DIM, A_KV_HEADS * HEAD_DIM
    b_w, c_w = B_HEADS * HEAD_DIM, C_HEADS * HEAD_DIM
    widths = (qa_w, kva_w, kva_w, b_w, b_w, b_w, c_w, c_w, c_w)
    cols = [int(c) for c in np.concatenate([[0], np.cumsum(widths)[:-1]])]
    in_w = sum(widths)

    row = lambda v: v.reshape(1, -1)
    h = x.reshape(B * S, D)
    mem2 = mem.reshape(B * Mm, D)
    for l in range(depth):
        proj = norm_matmul(h, row(norm_mix[l]), w_in[l].astype(BF16), tm=512, tn=768, out_dtype=F32)
        proj = proj.reshape(B, S, in_w)
        oa = attn_a(proj, sinks[l], q_col=cols[0], k_col=cols[1], v_col=cols[2], tq=512)
        ob = attn_b(proj, q_col=cols[3], k_col=cols[4], v_col=cols[5])
        oc = attn_c(proj, q_col=cols[6], k_col=cols[7], v_col=cols[8])
        h = mix_out(oa.reshape(B * S, qa_w), ob.reshape(B * S, b_w), oc.reshape(B * S, c_w),
                    row(gain_a[l]), row(gain_b[l]), row(gain_c[l]), h, w_out[l].astype(BF16),
                    tm=512, tn=1024)
        w_kv = jnp.concatenate([wk_x[l], wv_x[l]], axis=1).astype(BF16)
        kv = norm_matmul(mem2, row(norm_mem[l]), w_kv, tm=512, tn=w_kv.shape[1], out_dtype=BF16)
        h = xattn(h.reshape(B, S, D), row(norm_xattn[l]), wq_x[l].astype(BF16),
                  kv.reshape(B, Mm, -1), wo_x[l].astype(BF16), tq=512).reshape(B * S, D)
        h = ffn(h, row(norm_ffn[l]), w_gate[l].astype(BF16), w_up[l].astype(BF16),
                w_down[l].astype(BF16), row(norm_final), tm=512, tf=512,
                final_norm=(l == depth - 1))
    return h.reshape(B, S, D)
```

```python
import functools

import numpy as np
import jax
import jax.numpy as jnp
from jax import lax
from jax.experimental import pallas as pl
from jax.experimental.pallas import tpu as pltpu

F32 = jnp.float32
BF16 = jnp.bfloat16
NEG_INF = float("-inf")
POS_INF = float("inf")
LOG2E = 1.4426950408889634

EPS = 1e-5
HEAD_DIM = 64
LANES = 128
BF16_ROWS = 16
A_HEADS, A_KV_HEADS, A_WINDOW, A_BLOCK = 8, 2, 128, 128
B_HEADS, MOBA_BLOCK, MOBA_TOPK = 8, 256, 3
C_HEADS = 16
C_PATTERNS = ((128, 1), (512, 4), (2048, 16))
C_STEPS = 128
C_GROUP = 8
X_HEADS, X_HEAD_DIM = 4, 128
VMEM_LIMIT = 56 * 1024 * 1024
VT_ROWS = HEAD_DIM + BF16_ROWS


def _alibi_slopes(n):
    return jnp.asarray(2.0 ** (-8.0 * np.arange(1, n + 1) / n), F32)


def _params(*sem):
    return pltpu.CompilerParams(dimension_semantics=sem, vmem_limit_bytes=VMEM_LIMIT)


def _rms(x, g):
    return x * lax.rsqrt(jnp.mean(x * x, axis=-1, keepdims=True) + EPS) * g


def _dot_nt(a, b, precision=None):
    return lax.dot_general(a, b, (((1,), (1,)), ((), ())), precision=precision,
                           preferred_element_type=F32)


def _smem_spec():
    return pl.BlockSpec(memory_space=pltpu.SMEM)


def _zero_after(x):
    bits = pltpu.bitcast(x, jnp.uint32)
    return pltpu.bitcast(lax.shift_right_logical(bits, jnp.uint32(32)), F32)


def _norm_matmul_kernel(x_ref, g_ref, w_ref, o_ref, xn_ref):
    @pl.when(pl.program_id(1) == 0)
    def _():
        xn_ref[...] = _rms(x_ref[...], g_ref[...]).astype(BF16)

    o_ref[...] = jnp.dot(xn_ref[...], w_ref[...], preferred_element_type=F32).astype(o_ref.dtype)


def norm_matmul(x, g, w, layer, *, tm, tn, out_dtype):
    M, D = x.shape
    N = w.shape[2]
    assert M % tm == 0 and N % tn == 0
    return pl.pallas_call(
        _norm_matmul_kernel,
        grid=(M // tm, N // tn),
        in_specs=[pl.BlockSpec((tm, D), lambda i, j: (i, 0)),
                  pl.BlockSpec((1, D), lambda i, j: (0, 0)),
                  pl.BlockSpec((None, D, tn), lambda i, j: (layer, 0, j))],
        out_specs=pl.BlockSpec((tm, tn), lambda i, j: (i, j)),
        out_shape=jax.ShapeDtypeStruct((M, N), out_dtype),
        scratch_shapes=[pltpu.VMEM((tm, D), BF16)],
        compiler_params=_params("parallel", "arbitrary"),
        name="norm_matmul",
    )(x, g, w)


def _attn_a_kernel(slopes_ref, sinks_ref, q_ref, k_ref, v_ref, o_ref, *, n_sub):
    blk = A_BLOCK
    qi = pl.program_id(1)
    scale = HEAD_DIM ** -0.5
    row = lax.broadcasted_iota(jnp.int32, (blk, 2 * blk), 0)
    col = lax.broadcasted_iota(jnp.int32, (blk, 2 * blk), 1)
    dist = row + blk - col
    in_window = (dist >= 0) & (dist < A_WINDOW)
    distf = dist.astype(F32)
    lo_half = lax.broadcasted_iota(jnp.int32, (blk, LANES), 1) < HEAD_DIM
    group = A_HEADS // A_KV_HEADS

    def sub_block(sb, carry):
        gb = qi * n_sub + sb
        own = pl.multiple_of(gb * blk, blk)
        prev = pl.multiple_of(jnp.maximum(gb - 1, 0) * blk, blk)
        rows = pl.ds(pl.multiple_of(sb * blk, blk), blk)
        kk = jnp.concatenate([k_ref[pl.ds(prev, blk), :], k_ref[pl.ds(own, blk), :]], axis=0)
        vv = jnp.concatenate([v_ref[pl.ds(prev, blk), :], v_ref[pl.ds(own, blk), :]], axis=0)
        k_by_half = (kk.astype(BF16), pltpu.roll(kk, HEAD_DIM, axis=1).astype(BF16))
        v_by_half = (vv.astype(BF16), pltpu.roll(vv, HEAD_DIM, axis=1).astype(BF16))
        valid = in_window & ((col >= blk) | (gb > 0))
        for pair in range(A_HEADS // 2):
            qp = q_ref[rows, pair * LANES:(pair + 1) * LANES] * scale
            outs = []
            for half in range(2):
                j = 2 * pair + half
                swapped = (j // group) != half
                qm = jnp.where(lo_half if half == 0 else ~lo_half, qp, 0.0).astype(BF16)
                s = _dot_nt(qm, k_by_half[swapped])
                s = jnp.where(valid, s - slopes_ref[j] * distf, NEG_INF)
                sink = sinks_ref[j]
                m = jnp.maximum(jnp.max(s, axis=-1, keepdims=True), sink)
                p = jnp.exp(s - m)
                denom = jnp.sum(p, axis=-1, keepdims=True) + jnp.exp(sink - m)
                o = jnp.dot(p.astype(BF16), v_by_half[swapped], preferred_element_type=F32)
                outs.append(o / denom)
            o_ref[rows, pair * LANES:(pair + 1) * LANES] = jnp.where(lo_half, outs[0], outs[1])
        return carry

    lax.fori_loop(0, n_sub, sub_block, 0)


def attn_a(proj, sinks, *, q_col, k_col, v_col, tq):
    B, S, _ = proj.shape
    qw = A_HEADS * HEAD_DIM
    assert S % tq == 0 and tq % A_BLOCK == 0
    assert q_col % qw == 0 and k_col % LANES == 0 and v_col % LANES == 0
    kern = functools.partial(_attn_a_kernel, n_sub=tq // A_BLOCK)
    return pl.pallas_call(
        kern,
        grid=(B, S // tq),
        in_specs=[_smem_spec(), _smem_spec(),
                  pl.BlockSpec((None, tq, qw), lambda b, i: (b, i, q_col // qw)),
                  pl.BlockSpec((None, S, LANES), lambda b, i: (b, 0, k_col // LANES)),
                  pl.BlockSpec((None, S, LANES), lambda b, i: (b, 0, v_col // LANES))],
        out_specs=pl.BlockSpec((None, tq, qw), lambda b, i: (b, i, 0)),
        out_shape=jax.ShapeDtypeStruct((B, S, qw), F32),
        compiler_params=_params("parallel", "arbitrary"),
        name="attn_swa",
    )(_alibi_slopes(A_HEADS), sinks, proj, proj, proj)


def _attn_b_kernel(slopes_ref, q_ref, k_ref, v_ref, o_ref,
                   kb_ref, vt_ref, kmean_ref, kcat_ref, bias_ref, qs_ref, sel_ref, acc_ref, m_ref, s_ref,
                   *, nb):
    mb = MOBA_BLOCK
    hp = pl.program_id(1)
    i = pl.program_id(2)
    slope2 = [slopes_ref[2 * hp + h] * LOG2E for h in range(2)]

    @pl.when(i == 0)
    def _():
        kpos = lax.broadcasted_iota(jnp.int32, (mb, mb), 0)
        qpos = lax.broadcasted_iota(jnp.int32, (mb, mb), 1)
        relf = (qpos - kpos).astype(F32)
        for h in range(2):
            alibi = (-slope2[h]) * relf
            bias_ref[h, 0] = alibi
            bias_ref[h, 1] = jnp.where(qpos >= kpos, alibi, NEG_INF)
        ones = jnp.ones((BF16_ROWS, mb), BF16)

        def prep(n, carry):
            rows = pl.ds(pl.multiple_of(n * mb, mb), mb)
            kblk = k_ref[rows, :]
            kb_ref[n] = kblk.astype(BF16)
            kmean_ref[pl.ds(n, 1), :] = jnp.mean(kblk, axis=0, keepdims=True)
            vt = v_ref[rows, :].T
            for h in range(2):
                vt_ref[h, n, :HEAD_DIM, :] = vt[h * HEAD_DIM:(h + 1) * HEAD_DIM].astype(BF16)
                vt_ref[h, n, HEAD_DIM:, :] = ones
            return carry

        lax.fori_loop(0, nb, prep, 0)
        km = kmean_ref[...]
        lo_lanes = lax.broadcasted_iota(jnp.int32, (nb, LANES), 1) < HEAD_DIM
        for h in range(2):
            kmh = jnp.where(lo_lanes if h == 0 else ~lo_lanes, km, 0.0)
            hi = kmh.astype(BF16)
            lo = (kmh - hi.astype(F32)).astype(BF16)
            kcat_ref[h * nb:(h + 1) * nb, :] = jnp.concatenate([hi, hi, lo], axis=1)

    q = q_ref[...]
    lo_half = lax.broadcasted_iota(jnp.int32, (mb, LANES), 1) < HEAD_DIM
    blk = lax.broadcasted_iota(jnp.int32, (nb, mb), 0)
    past = blk < i
    q_hi = q.astype(BF16)
    q_lo = (q - q_hi.astype(F32)).astype(BF16)
    gate = _dot_nt(kcat_ref[...], jnp.concatenate([q_hi, q_lo, q_hi], axis=1))

    for h in range(2):
        g = jnp.where(past, gate[h * nb:(h + 1) * nb], NEG_INF)
        avail = blk >= 0
        sel = jnp.where(blk == i, 1.0, 0.0)
        for _ in range(MOBA_TOPK):
            mx = jnp.max(jnp.where(avail, g, NEG_INF), axis=0, keepdims=True)
            is_max = avail & (g == mx)
            first = jnp.min(jnp.where(is_max, blk, nb), axis=0, keepdims=True)
            pick = blk == first
            sel = jnp.where(pick & past, 1.0, sel)
            avail = avail & ~pick
        sel_ref[h, :nb] = sel
        sel_ref[h, nb:] = jnp.zeros((8, mb), F32)

    qc = q * (HEAD_DIM ** -0.5 * LOG2E)
    q_both = jnp.concatenate([jnp.where(lo_half, qc, 0.0), jnp.where(lo_half, 0.0, qc)], axis=0).astype(BF16)
    qs_ref[...] = q_both
    m_ref[...] = jnp.full(m_ref.shape, NEG_INF, F32)
    acc_ref[...] = jnp.zeros(acc_ref.shape, F32)
    s_ref[...] = _dot_nt(jnp.concatenate([kb_ref[i], kb_ref[0]], axis=0), q_both)

    def step(p, carry):
        first = p == 0
        j0 = jnp.where(first, i, 2 * p - 1)
        j1 = jnp.minimum(2 * p, nb - 1)
        row1 = jnp.where(2 * p < i, j1, nb)
        variant0 = jnp.where(first, 1, 0)
        n0 = jnp.minimum(2 * p + 1, nb - 1)
        n1 = jnp.minimum(2 * p + 2, nb - 1)
        s_next = _dot_nt(jnp.concatenate([kb_ref[n0], kb_ref[n1]], axis=0), qs_ref[...])
        dist_a = ((i - j0) * mb).astype(F32)
        dist_b = ((i - j1) * mb).astype(F32)
        for h in range(2):
            cols = slice(h * mb, (h + 1) * mb)
            shift_a = -slope2[h] * dist_a
            shift_b = -slope2[h] * dist_b
            sa = s_ref[:mb, cols] + bias_ref[h, variant0]
            sb = s_ref[mb:, cols] + bias_ref[h, 0]
            chosen_a = sel_ref[h, pl.ds(j0, 1), :] > 0.5
            chosen_b = sel_ref[h, pl.ds(row1, 1), :] > 0.5
            m_a = jnp.where(chosen_a, jnp.max(sa, axis=0, keepdims=True) + shift_a, NEG_INF)
            m_b = jnp.where(chosen_b, jnp.max(sb, axis=0, keepdims=True) + shift_b, NEG_INF)
            m_old = m_ref[h]
            m_new = jnp.maximum(m_old, jnp.maximum(m_a, m_b))
            alpha = jnp.exp2(m_old - m_new)
            pa = jnp.exp2(sa - jnp.where(chosen_a, m_new - shift_a, POS_INF)).astype(BF16)
            pb = jnp.exp2(sb - jnp.where(chosen_b, m_new - shift_b, POS_INF)).astype(BF16)
            vt = jnp.concatenate([vt_ref[h, j0], vt_ref[h, j1]], axis=1)
            acc_ref[h] = alpha * acc_ref[h] + jnp.dot(vt, jnp.concatenate([pa, pb], axis=0),
                                                      preferred_element_type=F32)
            m_ref[h] = m_new
        s_ref[...] = s_next
        return carry

    lax.fori_loop(0, (i + 2) // 2, step, 0)
    outs = []
    for h in range(2):
        acc = acc_ref[h]
        outs.append(acc[:HEAD_DIM] * (1.0 / acc[HEAD_DIM:HEAD_DIM + 1]))
    o_ref[...] = jnp.concatenate(outs, axis=0).T


def attn_b(proj, *, q_col, k_col, v_col):
    B, S, _ = proj.shape
    mb = MOBA_BLOCK
    assert S % mb == 0
    nb = S // mb
    assert MOBA_TOPK < nb and nb % 8 == 0
    pairs = B_HEADS // 2
    kern = functools.partial(_attn_b_kernel, nb=nb)
    return pl.pallas_call(
        kern,
        grid=(B, pairs, nb),
        in_specs=[_smem_spec(),
                  pl.BlockSpec((None, mb, LANES), lambda b, p, i: (b, i, q_col // LANES + p)),
                  pl.BlockSpec((None, S, LANES), lambda b, p, i: (b, 0, k_col // LANES + p)),
                  pl.BlockSpec((None, S, LANES), lambda b, p, i: (b, 0, v_col // LANES + p))],
        out_specs=pl.BlockSpec((None, mb, LANES), lambda b, p, i: (b, i, p)),
        out_shape=jax.ShapeDtypeStruct((B, S, B_HEADS * HEAD_DIM), F32),
        scratch_shapes=[pltpu.VMEM((nb, mb, LANES), BF16),
                        pltpu.VMEM((2, nb, VT_ROWS, mb), BF16),
                        pltpu.VMEM((nb, LANES), F32),
                        pltpu.VMEM((2 * nb, 3 * LANES), BF16),
                        pltpu.VMEM((2, 2, mb, mb), F32),
                        pltpu.VMEM((2 * mb, LANES), BF16),
                        pltpu.VMEM((2, nb + 8, mb), F32),
                        pltpu.VMEM((2, VT_ROWS, mb), F32),
                        pltpu.VMEM((2, 1, mb), F32),
                        pltpu.VMEM((2 * mb, 2 * mb), F32)],
        compiler_params=_params("parallel", "parallel", "arbitrary"),
        name="attn_moba",
    )(_alibi_slopes(B_HEADS), proj, proj, proj)


def _attn_c_kernel(slopes_ref, q_ref, k_ref, v_ref, o_ref,
                   kb_ref, vt_ref, bias_ref, orun_ref, lse_ref, *, seq, group):
    n = C_STEPS
    hp = pl.program_id(1)
    ntile = seq // n
    kk = lax.broadcasted_iota(jnp.int32, (2 * n, n), 0)
    qq = lax.broadcasted_iota(jnp.int32, (2 * n, n), 1)
    step_i = qq + n - kk
    valid = (step_i >= 0) & (step_i <= n)
    stepf = step_i.astype(F32)
    lo_half = lax.broadcasted_iota(jnp.int32, (n, LANES), 1) < HEAD_DIM
    ones = jnp.ones((BF16_ROWS, 2 * n), BF16)

    for bi, (w, d) in enumerate(C_PATTERNS):
        nblk = seq // w
        first, last = bi == 0, bi == len(C_PATTERNS) - 1

        def rows_at(t, d=d, w=w, nblk=nblk):
            start = (t % nblk) * w + t // nblk
            return pl.ds(start, n) if d == 1 else pl.ds(start, n, stride=d)

        for h in range(2):
            full = jnp.where(valid, (-slopes_ref[2 * hp + h] * (LOG2E * d)) * stepf, NEG_INF)
            bias_ref[h, 1] = full
            bias_ref[h, 0] = jnp.where(kk >= n, full, NEG_INF)

        def prep(u, carry, rows_at=rows_at):
            for sub in range(group):
                t = u * group + sub
                kb_ref[t] = k_ref[rows_at(t), :].astype(BF16)
                vt_ref[t] = v_ref[rows_at(t), :].T.astype(BF16)
            return carry

        lax.fori_loop(0, ntile // group, prep, 0)

        def tiles(u, carry, rows_at=rows_at, nblk=nblk, first=first, last=last):
            ts = [u * group + sub for sub in range(group)]
            prevs, variants, scores = [], [], []
            for t in ts:
                has_prev = (t % nblk) > 0
                tp = jnp.where(has_prev, t - 1, t)
                prevs.append(tp)
                variants.append(jnp.where(has_prev, 1, 0))
                q = q_ref[rows_at(t), :] * (HEAD_DIM ** -0.5 * LOG2E)
                q_both = jnp.concatenate([jnp.where(lo_half, q, 0.0), jnp.where(lo_half, 0.0, q)], axis=0)
                kcat = jnp.concatenate([kb_ref[tp], kb_ref[t]], axis=0)
                scores.append(_dot_nt(kcat, q_both.astype(BF16)))
            anchor = sum(_zero_after(sc[0:1, :n]) for sc in scores[1:])
            results = []
            for g, t in enumerate(ts):
                vtcat = jnp.concatenate([vt_ref[prevs[g]], vt_ref[t]], axis=1)
                o_rows, lse_rows = [], []
                for h in range(2):
                    s = scores[g][:, h * n:(h + 1) * n] + bias_ref[h, variants[g]]
                    m = jnp.max(s, axis=0, keepdims=True)
                    if g == 0:
                        m = m + anchor
                    p = jnp.exp2(s - m).astype(BF16)
                    vaug = jnp.concatenate([vtcat[h * HEAD_DIM:(h + 1) * HEAD_DIM], ones], axis=0)
                    acc = jnp.dot(vaug, p, preferred_element_type=F32)
                    l = acc[HEAD_DIM:HEAD_DIM + 1]
                    o_rows.append(acc[:HEAD_DIM] * (1.0 / l))
                    lse_rows.append(jnp.broadcast_to(m + jnp.log2(l), (HEAD_DIM, n)))
                results.append((jnp.concatenate(o_rows, axis=0), jnp.concatenate(lse_rows, axis=0)))
            for g, t in enumerate(ts):
                idx = rows_at(t)
                o_t = results[g][0].T
                lse_t = results[g][1].T
                if first:
                    orun_ref[idx, :] = o_t
                    lse_ref[idx, :] = lse_t
                else:
                    lse_old = lse_ref[idx, :]
                    top = jnp.maximum(lse_old, lse_t)
                    w_old = jnp.exp2(lse_old - top)
                    w_t = jnp.exp2(lse_t - top)
                    den = w_old + w_t
                    o_new = (w_old * orun_ref[idx, :] + w_t * o_t) * (1.0 / den)
                    if last:
                        o_ref[idx, :] = o_new
                    else:
                        orun_ref[idx, :] = o_new
                        lse_ref[idx, :] = top + jnp.log2(den)
            return carry

        lax.fori_loop(0, ntile // group, tiles, 0)


def attn_c(proj, *, q_col, k_col, v_col):
    B, S, _ = proj.shape
    n = C_STEPS
    assert all(S % w == 0 and w // d == n for w, d in C_PATTERNS)
    assert (S // n) % C_GROUP == 0
    pairs = C_HEADS // 2
    kern = functools.partial(_attn_c_kernel, seq=S, group=C_GROUP)
    return pl.pallas_call(
        kern,
        grid=(B, pairs),
        in_specs=[_smem_spec(),
                  pl.BlockSpec((None, S, LANES), lambda b, p: (b, 0, q_col // LANES + p)),
                  pl.BlockSpec((None, S, LANES), lambda b, p: (b, 0, k_col // LANES + p)),
                  pl.BlockSpec((None, S, LANES), lambda b, p: (b, 0, v_col // LANES + p))],
        out_specs=pl.BlockSpec((None, S, LANES), lambda b, p: (b, 0, p)),
        out_shape=jax.ShapeDtypeStruct((B, S, C_HEADS * HEAD_DIM), F32),
        scratch_shapes=[pltpu.VMEM((S // n, n, LANES), BF16),
                        pltpu.VMEM((S // n, LANES, n), BF16),
                        pltpu.VMEM((2, 2, 2 * n, n), F32),
                        pltpu.VMEM((S, LANES), F32),
                        pltpu.VMEM((S, LANES), F32)],
        compiler_params=_params("parallel", "parallel"),
        name="attn_dilated",
    )(_alibi_slopes(C_HEADS), proj, proj, proj)


def _mix_out_kernel(oa_ref, ob_ref, oc_ref, ga_ref, gb_ref, gc_ref, h_ref, w_ref, o_ref):
    y = jnp.concatenate([_rms(oa_ref[...], ga_ref[...]).astype(BF16),
                         _rms(ob_ref[...], gb_ref[...]).astype(BF16),
                         _rms(oc_ref[...], gc_ref[...]).astype(BF16)], axis=-1)
    o_ref[...] = h_ref[...] + jnp.dot(y, w_ref[...], preferred_element_type=F32)


def mix_out(oa, ob, oc, ga, gb, gc, h, w, layer, *, tm):
    M, D = h.shape
    wa, wb, wc = oa.shape[1], ob.shape[1], oc.shape[1]
    K = wa + wb + wc
    assert w.shape[1:] == (K, D) and M % tm == 0
    row = lambda width: pl.BlockSpec((tm, width), lambda i: (i, 0))
    gain = lambda width: pl.BlockSpec((1, width), lambda i: (0, 0))
    return pl.pallas_call(
        _mix_out_kernel,
        grid=(M // tm,),
        in_specs=[row(wa), row(wb), row(wc), gain(wa), gain(wb), gain(wc), row(D),
                  pl.BlockSpec((None, K, D), lambda i: (layer, 0, 0))],
        out_specs=row(D),
        out_shape=jax.ShapeDtypeStruct((M, D), F32),
        compiler_params=_params("parallel"),
        name="mix_out",
    )(oa, ob, oc, ga, gb, gc, h, w)


def _xattn_kernel(h_ref, g_ref, wq_ref, kv_ref, wo_ref, o_ref):
    xw = X_HEADS * X_HEAD_DIM
    h = h_ref[...]
    xn = _rms(h, g_ref[...]).astype(BF16)
    q = jnp.dot(xn, wq_ref[...], preferred_element_type=F32) * (X_HEAD_DIM ** -0.5)
    heads = []
    for hd in range(X_HEADS):
        lanes = slice(hd * X_HEAD_DIM, (hd + 1) * X_HEAD_DIM)
        s = _dot_nt(q[:, lanes].astype(BF16), kv_ref[:, lanes])
        m = jnp.max(s, axis=-1, keepdims=True)
        p = jnp.exp(s - m)
        l = jnp.sum(p, axis=-1, keepdims=True)
        v = kv_ref[:, xw + hd * X_HEAD_DIM:xw + (hd + 1) * X_HEAD_DIM]
        heads.append(jnp.dot(p.astype(BF16), v, preferred_element_type=F32) / l)
    o = jnp.concatenate(heads, axis=-1).astype(BF16)
    o_ref[...] = h + jnp.dot(o, wo_ref[...], preferred_element_type=F32)


def xattn(h, g, wq, kv, wo, layer, *, tq):
    B, S, D = h.shape
    Mm = kv.shape[1]
    xw = X_HEADS * X_HEAD_DIM
    assert S % tq == 0
    return pl.pallas_call(
        _xattn_kernel,
        grid=(B, S // tq),
        in_specs=[pl.BlockSpec((None, tq, D), lambda b, i: (b, i, 0)),
                  pl.BlockSpec((1, D), lambda b, i: (0, 0)),
                  pl.BlockSpec((None, D, xw), lambda b, i: (layer, 0, 0)),
                  pl.BlockSpec((None, Mm, 2 * xw), lambda b, i: (b, 0, 0)),
                  pl.BlockSpec((None, xw, D), lambda b, i: (layer, 0, 0))],
        out_specs=pl.BlockSpec((None, tq, D), lambda b, i: (b, i, 0)),
        out_shape=jax.ShapeDtypeStruct((B, S, D), F32),
        compiler_params=_params("parallel", "parallel"),
        name="xattn",
    )(h, g, wq, kv, wo)


def _ffn_kernel(h_ref, g_ref, wg_ref, wu_ref, wd_ref, gf_ref, o_ref, xn_ref, *, final_norm):
    f = pl.program_id(1)

    @pl.when(f == 0)
    def _():
        h = h_ref[...]
        xn_ref[...] = _rms(h, g_ref[...]).astype(BF16)
        o_ref[...] = h

    xn = xn_ref[...]
    gate = jnp.dot(xn, wg_ref[...], preferred_element_type=F32)
    up = jnp.dot(xn, wu_ref[...], preferred_element_type=F32)
    act = (gate / (1.0 + jnp.exp(-gate))) * up
    o_ref[...] += jnp.dot(act.astype(BF16), wd_ref[...], preferred_element_type=F32)

    if final_norm:
        @pl.when(f == pl.num_programs(1) - 1)
        def _():
            o_ref[...] = _rms(o_ref[...], gf_ref[...])


def ffn(h, g, wg, wu, wd, gf, layer, *, tm, tf, final_norm):
    M, D = h.shape
    Fd = wg.shape[2]
    assert M % tm == 0 and Fd % tf == 0
    kern = functools.partial(_ffn_kernel, final_norm=final_norm)
    return pl.pallas_call(
        kern,
        grid=(M // tm, Fd // tf),
        in_specs=[pl.BlockSpec((tm, D), lambda i, f: (i, 0)),
                  pl.BlockSpec((1, D), lambda i, f: (0, 0)),
                  pl.BlockSpec((None, D, tf), lambda i, f: (layer, 0, f)),
                  pl.BlockSpec((None, D, tf), lambda i, f: (layer, 0, f)),
                  pl.BlockSpec((None, tf, D), lambda i, f: (layer, f, 0)),
                  pl.BlockSpec((1, D), lambda i, f: (0, 0))],
        out_specs=pl.BlockSpec((tm, D), lambda i, f: (i, 0)),
        out_shape=jax.ShapeDtypeStruct((M, D), F32),
        scratch_shapes=[pltpu.VMEM((tm, D), BF16)],
        compiler_params=_params("parallel", "arbitrary"),
        name="ffn",
    )(h, g, wg, wu, wd, gf)


def kernel(x, mem, norm_mix, w_in, sinks, gain_a, gain_b, gain_c, w_out, norm_xattn, norm_mem,
           wq_x, wk_x, wv_x, wo_x, norm_ffn, w_gate, w_up, w_down, norm_final):
    B, S, D = x.shape
    depth = w_in.shape[0]
    Mm = mem.shape[1]
    qa_w, kva_w = A_HEADS * HEAD_DIM, A_KV_HEADS * HEAD_DIM
    b_w, c_w = B_HEADS * HEAD_DIM, C_HEADS * HEAD_DIM
    widths = (qa_w, kva_w, kva_w, b_w, b_w, b_w, c_w, c_w, c_w)
    cols = [int(c) for c in np.concatenate([[0], np.cumsum(widths)[:-1]])]
    in_w = sum(widths)

    row = lambda v: v.reshape(1, -1)
    h = x.reshape(B * S, D)
    mem2 = mem.reshape(B * Mm, D)
    w_in_b, w_out_b = w_in.astype(BF16), w_out.astype(BF16)
    w_kv_b = jnp.concatenate([wk_x, wv_x], axis=2).astype(BF16)
    wq_b, wo_b = wq_x.astype(BF16), wo_x.astype(BF16)
    wg_b, wu_b, wd_b = w_gate.astype(BF16), w_up.astype(BF16), w_down.astype(BF16)
    for l in range(depth):
        proj = norm_matmul(h, row(norm_mix[l]), w_in_b, l, tm=1024, tn=768, out_dtype=F32)
        proj = proj.reshape(B, S, in_w)
        oa = attn_a(proj, sinks[l], q_col=cols[0], k_col=cols[1], v_col=cols[2], tq=512)
        ob = attn_b(proj, q_col=cols[3], k_col=cols[4], v_col=cols[5])
        oc = attn_c(proj, q_col=cols[6], k_col=cols[7], v_col=cols[8])
        h = mix_out(oa.reshape(B * S, qa_w), ob.reshape(B * S, b_w), oc.reshape(B * S, c_w),
                    row(gain_a[l]), row(gain_b[l]), row(gain_c[l]), h, w_out_b, l, tm=512)
        kv = norm_matmul(mem2, row(norm_mem[l]), w_kv_b, l, tm=512, tn=w_kv_b.shape[2], out_dtype=BF16)
        h = xattn(h.reshape(B, S, D), row(norm_xattn[l]), wq_b, kv.reshape(B, Mm, -1), wo_b, l,
                  tq=512).reshape(B * S, D)
        h = ffn(h, row(norm_ffn[l]), wg_b, wu_b, wd_b, row(norm_final), l, tm=1024, tf=512,
                final_norm=(l == depth - 1))
    return h.reshape(B, S, D)
```

```python
import functools

import numpy as np
import jax
import jax.numpy as jnp
from jax import lax
from jax.experimental import pallas as pl
from jax.experimental.pallas import tpu as pltpu

F32 = jnp.float32
BF16 = jnp.bfloat16
NEG_INF = float("-inf")
POS_INF = float("inf")
LOG2E = 1.4426950408889634

EPS = 1e-5
HEAD_DIM = 64
LANES = 128
BF16_ROWS = 16
A_HEADS, A_KV_HEADS, A_WINDOW, A_BLOCK = 8, 2, 128, 128
B_HEADS, MOBA_BLOCK, MOBA_TOPK = 8, 256, 3
C_HEADS = 16
C_PATTERNS = ((128, 1), (512, 4), (2048, 16))
C_STEPS = 128
C_GROUP = 16
X_HEADS, X_HEAD_DIM = 4, 128
VMEM_LIMIT = 56 * 1024 * 1024
VT_ROWS = HEAD_DIM + BF16_ROWS


def _alibi_slopes(n):
    return jnp.asarray(2.0 ** (-8.0 * np.arange(1, n + 1) / n), F32)


def _params(*sem):
    return pltpu.CompilerParams(dimension_semantics=sem, vmem_limit_bytes=VMEM_LIMIT)


def _rms(x, g):
    return x * lax.rsqrt(jnp.mean(x * x, axis=-1, keepdims=True) + EPS) * g


def _dot_nt(a, b, precision=None):
    return lax.dot_general(a, b, (((1,), (1,)), ((), ())), precision=precision,
                           preferred_element_type=F32)


def _smem_spec():
    return pl.BlockSpec(memory_space=pltpu.SMEM)


def _zero_after(x):
    bits = pltpu.bitcast(x, jnp.uint32)
    return pltpu.bitcast(lax.shift_right_logical(bits, jnp.uint32(32)), F32)


def _norm_matmul_kernel(x_ref, g_ref, w_ref, o_ref, xn_ref):
    @pl.when(pl.program_id(1) == 0)
    def _():
        xn_ref[...] = _rms(x_ref[...], g_ref[...]).astype(BF16)

    o_ref[...] = jnp.dot(xn_ref[...], w_ref[...], preferred_element_type=F32).astype(o_ref.dtype)


def norm_matmul(x, g, w, layer, *, tm, tn, out_dtype):
    M, D = x.shape
    N = w.shape[2]
    assert M % tm == 0 and N % tn == 0
    return pl.pallas_call(
        _norm_matmul_kernel,
        grid=(M // tm, N // tn),
        in_specs=[pl.BlockSpec((tm, D), lambda i, j: (i, 0)),
                  pl.BlockSpec((1, D), lambda i, j: (0, 0)),
                  pl.BlockSpec((None, D, tn), lambda i, j: (layer, 0, j))],
        out_specs=pl.BlockSpec((tm, tn), lambda i, j: (i, j)),
        out_shape=jax.ShapeDtypeStruct((M, N), out_dtype),
        scratch_shapes=[pltpu.VMEM((tm, D), BF16)],
        compiler_params=_params("parallel", "arbitrary"),
        name="norm_matmul",
    )(x, g, w)


def _attn_a_kernel(slopes_ref, sinks_ref, q_ref, k_ref, v_ref, o_ref, *, n_sub):
    blk = A_BLOCK
    qi = pl.program_id(1)
    scale = HEAD_DIM ** -0.5
    row = lax.broadcasted_iota(jnp.int32, (blk, 2 * blk), 0)
    col = lax.broadcasted_iota(jnp.int32, (blk, 2 * blk), 1)
    dist = row + blk - col
    in_window = (dist >= 0) & (dist < A_WINDOW)
    distf = dist.astype(F32)
    lo_half = lax.broadcasted_iota(jnp.int32, (blk, LANES), 1) < HEAD_DIM
    group = A_HEADS // A_KV_HEADS

    def sub_block(sb, carry):
        gb = qi * n_sub + sb
        own = pl.multiple_of(gb * blk, blk)
        prev = pl.multiple_of(jnp.maximum(gb - 1, 0) * blk, blk)
        rows = pl.ds(pl.multiple_of(sb * blk, blk), blk)
        kk = jnp.concatenate([k_ref[pl.ds(prev, blk), :], k_ref[pl.ds(own, blk), :]], axis=0)
        vv = jnp.concatenate([v_ref[pl.ds(prev, blk), :], v_ref[pl.ds(own, blk), :]], axis=0)
        k_by_half = (kk.astype(BF16), pltpu.roll(kk, HEAD_DIM, axis=1).astype(BF16))
        v_by_half = (vv.astype(BF16), pltpu.roll(vv, HEAD_DIM, axis=1).astype(BF16))
        valid = in_window & ((col >= blk) | (gb > 0))
        for pair in range(A_HEADS // 2):
            qp = q_ref[rows, pair * LANES:(pair + 1) * LANES] * scale
            outs = []
            for half in range(2):
                j = 2 * pair + half
                swapped = (j // group) != half
                qm = jnp.where(lo_half if half == 0 else ~lo_half, qp, 0.0).astype(BF16)
                s = _dot_nt(qm, k_by_half[swapped])
                s = jnp.where(valid, s - slopes_ref[j] * distf, NEG_INF)
                sink = sinks_ref[j]
                m = jnp.maximum(jnp.max(s, axis=-1, keepdims=True), sink)
                p = jnp.exp(s - m)
                denom = jnp.sum(p, axis=-1, keepdims=True) + jnp.exp(sink - m)
                o = jnp.dot(p.astype(BF16), v_by_half[swapped], preferred_element_type=F32)
                outs.append(o / denom)
            o_ref[rows, pair * LANES:(pair + 1) * LANES] = jnp.where(lo_half, outs[0], outs[1])
        return carry

    lax.fori_loop(0, n_sub, sub_block, 0)


def attn_a(proj, sinks, *, q_col, k_col, v_col, tq):
    B, S, _ = proj.shape
    qw = A_HEADS * HEAD_DIM
    assert S % tq == 0 and tq % A_BLOCK == 0
    assert q_col % qw == 0 and k_col % LANES == 0 and v_col % LANES == 0
    kern = functools.partial(_attn_a_kernel, n_sub=tq // A_BLOCK)
    return pl.pallas_call(
        kern,
        grid=(B, S // tq),
        in_specs=[_smem_spec(), _smem_spec(),
                  pl.BlockSpec((None, tq, qw), lambda b, i: (b, i, q_col // qw)),
                  pl.BlockSpec((None, S, LANES), lambda b, i: (b, 0, k_col // LANES)),
                  pl.BlockSpec((None, S, LANES), lambda b, i: (b, 0, v_col // LANES))],
        out_specs=pl.BlockSpec((None, tq, qw), lambda b, i: (b, i, 0)),
        out_shape=jax.ShapeDtypeStruct((B, S, qw), F32),
        compiler_params=_params("parallel", "arbitrary"),
        name="attn_swa",
    )(_alibi_slopes(A_HEADS), sinks, proj, proj, proj)


def _attn_b_kernel(slopes_ref, q_ref, k_ref, v_ref, o_ref,
                   kb_ref, vt_ref, kmean_ref, kcat_ref, bias_ref, qs_ref, sel_ref, acc_ref, m_ref, s_ref,
                   *, nb):
    mb = MOBA_BLOCK
    hp = pl.program_id(1)
    slope2 = [slopes_ref[2 * hp + h] * LOG2E for h in range(2)]

    def prepare():
        kpos = lax.broadcasted_iota(jnp.int32, (mb, mb), 0)
        qpos = lax.broadcasted_iota(jnp.int32, (mb, mb), 1)
        relf = (qpos - kpos).astype(F32)
        for h in range(2):
            alibi = (-slope2[h]) * relf
            bias_ref[h, 0] = alibi
            bias_ref[h, 1] = jnp.where(qpos >= kpos, alibi, NEG_INF)
        ones = jnp.ones((BF16_ROWS, mb), BF16)

        def prep(n, carry):
            rows = pl.ds(pl.multiple_of(n * mb, mb), mb)
            kblk = k_ref[rows, :]
            kb_ref[n] = kblk.astype(BF16)
            kmean_ref[pl.ds(n, 1), :] = jnp.mean(kblk, axis=0, keepdims=True)
            vt = v_ref[rows, :].T
            for h in range(2):
                vt_ref[h, n, :HEAD_DIM, :] = vt[h * HEAD_DIM:(h + 1) * HEAD_DIM].astype(BF16)
                vt_ref[h, n, HEAD_DIM:, :] = ones
            return carry

        lax.fori_loop(0, nb, prep, 0)
        km = kmean_ref[...]
        lo_lanes = lax.broadcasted_iota(jnp.int32, (nb, LANES), 1) < HEAD_DIM
        for h in range(2):
            kmh = jnp.where(lo_lanes if h == 0 else ~lo_lanes, km, 0.0)
            hi = kmh.astype(BF16)
            lo = (kmh - hi.astype(F32)).astype(BF16)
            kcat_ref[h * nb:(h + 1) * nb, :] = jnp.concatenate([hi, hi, lo], axis=1)

    prepare()

    def query_block(i, carry0):
        qrows = pl.ds(pl.multiple_of(i * mb, mb), mb)
        q = q_ref[qrows, :]
        lo_half = lax.broadcasted_iota(jnp.int32, (mb, LANES), 1) < HEAD_DIM
        blk = lax.broadcasted_iota(jnp.int32, (nb, mb), 0)
        past = blk < i
        q_hi = q.astype(BF16)
        q_lo = (q - q_hi.astype(F32)).astype(BF16)
        gate = _dot_nt(kcat_ref[...], jnp.concatenate([q_hi, q_lo, q_hi], axis=1))

        for h in range(2):
            g = jnp.where(past, gate[h * nb:(h + 1) * nb], NEG_INF)
            avail = blk >= 0
            sel = jnp.where(blk == i, 1.0, 0.0)
            for _ in range(MOBA_TOPK):
                mx = jnp.max(jnp.where(avail, g, NEG_INF), axis=0, keepdims=True)
                is_max = avail & (g == mx)
                first = jnp.min(jnp.where(is_max, blk, nb), axis=0, keepdims=True)
                pick = blk == first
                sel = jnp.where(pick & past, 1.0, sel)
                avail = avail & ~pick
            sel_ref[h, :nb] = sel
            sel_ref[h, nb:] = jnp.zeros((8, mb), F32)

        qc = q * (HEAD_DIM ** -0.5 * LOG2E)
        q_both = jnp.concatenate([jnp.where(lo_half, qc, 0.0), jnp.where(lo_half, 0.0, qc)], axis=0).astype(BF16)
        qs_ref[...] = q_both
        m_ref[...] = jnp.full(m_ref.shape, NEG_INF, F32)
        acc_ref[...] = jnp.zeros(acc_ref.shape, F32)
        s_ref[...] = _dot_nt(jnp.concatenate([kb_ref[i], kb_ref[0]], axis=0), q_both)

        def step(p, carry):
            first = p == 0
            j0 = jnp.where(first, i, 2 * p - 1)
            j1 = jnp.minimum(2 * p, nb - 1)
            row1 = jnp.where(2 * p < i, j1, nb)
            variant0 = jnp.where(first, 1, 0)
            n0 = jnp.minimum(2 * p + 1, nb - 1)
            n1 = jnp.minimum(2 * p + 2, nb - 1)
            s_next = _dot_nt(jnp.concatenate([kb_ref[n0], kb_ref[n1]], axis=0), qs_ref[...])
            dist_a = ((i - j0) * mb).astype(F32)
            dist_b = ((i - j1) * mb).astype(F32)
            for h in range(2):
                cols = slice(h * mb, (h + 1) * mb)
                shift_a = -slope2[h] * dist_a
                shift_b = -slope2[h] * dist_b
                sa = s_ref[:mb, cols] + bias_ref[h, variant0]
                sb = s_ref[mb:, cols] + bias_ref[h, 0]
                chosen_a = sel_ref[h, pl.ds(j0, 1), :] > 0.5
                chosen_b = sel_ref[h, pl.ds(row1, 1), :] > 0.5
                m_a = jnp.where(chosen_a, jnp.max(sa, axis=0, keepdims=True) + shift_a, NEG_INF)
                m_b = jnp.where(chosen_b, jnp.max(sb, axis=0, keepdims=True) + shift_b, NEG_INF)
                m_old = m_ref[h]
                m_new = jnp.maximum(m_old, jnp.maximum(m_a, m_b))
                alpha = jnp.exp2(m_old - m_new)
                pa = jnp.exp2(sa - jnp.where(chosen_a, m_new - shift_a, POS_INF)).astype(BF16)
                pb = jnp.exp2(sb - jnp.where(chosen_b, m_new - shift_b, POS_INF)).astype(BF16)
                vt = jnp.concatenate([vt_ref[h, j0], vt_ref[h, j1]], axis=1)
                acc_ref[h] = alpha * acc_ref[h] + jnp.dot(vt, jnp.concatenate([pa, pb], axis=0),
                                                          preferred_element_type=F32)
                m_ref[h] = m_new
            s_ref[...] = s_next
            return carry

        lax.fori_loop(0, (i + 2) // 2, step, 0)
        outs = []
        for h in range(2):
            acc = acc_ref[h]
            outs.append(acc[:HEAD_DIM] * (1.0 / acc[HEAD_DIM:HEAD_DIM + 1]))
        o_ref[qrows, :] = jnp.concatenate(outs, axis=0).T
        return carry0

    lax.fori_loop(0, nb, query_block, 0)


def attn_b(proj, *, q_col, k_col, v_col):
    B, S, _ = proj.shape
    mb = MOBA_BLOCK
    assert S % mb == 0
    nb = S // mb
    assert MOBA_TOPK < nb and nb % 8 == 0
    pairs = B_HEADS // 2
    kern = functools.partial(_attn_b_kernel, nb=nb)
    return pl.pallas_call(
        kern,
        grid=(B, pairs),
        in_specs=[_smem_spec(),
                  pl.BlockSpec((None, S, LANES), lambda b, p: (b, 0, q_col // LANES + p)),
                  pl.BlockSpec((None, S, LANES), lambda b, p: (b, 0, k_col // LANES + p)),
                  pl.BlockSpec((None, S, LANES), lambda b, p: (b, 0, v_col // LANES + p))],
        out_specs=pl.BlockSpec((None, S, LANES), lambda b, p: (b, 0, p)),
        out_shape=jax.ShapeDtypeStruct((B, S, B_HEADS * HEAD_DIM), F32),
        scratch_shapes=[pltpu.VMEM((nb, mb, LANES), BF16),
                        pltpu.VMEM((2, nb, VT_ROWS, mb), BF16),
                        pltpu.VMEM((nb, LANES), F32),
                        pltpu.VMEM((2 * nb, 3 * LANES), BF16),
                        pltpu.VMEM((2, 2, mb, mb), F32),
                        pltpu.VMEM((2 * mb, LANES), BF16),
                        pltpu.VMEM((2, nb + 8, mb), F32),
                        pltpu.VMEM((2, VT_ROWS, mb), F32),
                        pltpu.VMEM((2, 1, mb), F32),
                        pltpu.VMEM((2 * mb, 2 * mb), F32)],
        compiler_params=_params("parallel", "parallel"),
        name="attn_moba",
    )(_alibi_slopes(B_HEADS), proj, proj, proj)


def _attn_c_kernel(slopes_ref, q_ref, k_ref, v_ref, o_ref,
                   kb_ref, vt_ref, bias_ref, orun_ref, lse_ref, *, seq, group):
    n = C_STEPS
    hp = pl.program_id(1)
    ntile = seq // n
    kk = lax.broadcasted_iota(jnp.int32, (2 * n, n), 0)
    qq = lax.broadcasted_iota(jnp.int32, (2 * n, n), 1)
    step_i = qq + n - kk
    valid = (step_i >= 0) & (step_i <= n)
    stepf = step_i.astype(F32)
    lo_half = lax.broadcasted_iota(jnp.int32, (n, LANES), 1) < HEAD_DIM
    ones = jnp.ones((BF16_ROWS, 2 * n), BF16)

    for bi, (w, d) in enumerate(C_PATTERNS):
        nblk = seq // w
        first, last = bi == 0, bi == len(C_PATTERNS) - 1

        def rows_at(t, d=d, w=w, nblk=nblk):
            start = (t % nblk) * w + t // nblk
            return pl.ds(start, n) if d == 1 else pl.ds(start, n, stride=d)

        for h in range(2):
            full = jnp.where(valid, (-slopes_ref[2 * hp + h] * (LOG2E * d)) * stepf, NEG_INF)
            bias_ref[h, 1] = full
            bias_ref[h, 0] = jnp.where(kk >= n, full, NEG_INF)

        def prep(u, carry, rows_at=rows_at):
            for sub in range(group):
                t = u * group + sub
                kb_ref[t] = k_ref[rows_at(t), :].astype(BF16)
                vt_ref[t] = v_ref[rows_at(t), :].T.astype(BF16)
            return carry

        lax.fori_loop(0, ntile // group, prep, 0)

        def tiles(u, carry, rows_at=rows_at, nblk=nblk, first=first, last=last):
            ts = [u * group + sub for sub in range(group)]
            prevs, variants, scores = [], [], []
            for t in ts:
                has_prev = (t % nblk) > 0
                tp = jnp.where(has_prev, t - 1, t)
                prevs.append(tp)
                variants.append(jnp.where(has_prev, 1, 0))
                q = q_ref[rows_at(t), :] * (HEAD_DIM ** -0.5 * LOG2E)
                q_both = jnp.concatenate([jnp.where(lo_half, q, 0.0), jnp.where(lo_half, 0.0, q)], axis=0)
                kcat = jnp.concatenate([kb_ref[tp], kb_ref[t]], axis=0)
                scores.append(_dot_nt(kcat, q_both.astype(BF16)))
            anchor = sum(_zero_after(sc[0:1, :n]) for sc in scores[1:])
            results = []
            for g, t in enumerate(ts):
                vtcat = jnp.concatenate([vt_ref[prevs[g]], vt_ref[t]], axis=1)
                o_rows, lse_rows = [], []
                for h in range(2):
                    s = scores[g][:, h * n:(h + 1) * n] + bias_ref[h, variants[g]]
                    m = jnp.max(s, axis=0, keepdims=True)
                    if g == 0:
                        m = m + anchor
                    p = jnp.exp2(s - m).astype(BF16)
                    vaug = jnp.concatenate([vtcat[h * HEAD_DIM:(h + 1) * HEAD_DIM], ones], axis=0)
                    acc = jnp.dot(vaug, p, preferred_element_type=F32)
                    l = acc[HEAD_DIM:HEAD_DIM + 1]
                    o_rows.append(acc[:HEAD_DIM] * (1.0 / l))
                    lse_rows.append(jnp.broadcast_to(m + jnp.log2(l), (HEAD_DIM, n)))
                results.append((jnp.concatenate(o_rows, axis=0), jnp.concatenate(lse_rows, axis=0)))
            for g, t in enumerate(ts):
                idx = rows_at(t)
                o_t = results[g][0].T
                lse_t = results[g][1].T
                if first:
                    orun_ref[idx, :] = o_t
                    lse_ref[idx, :] = lse_t
                else:
                    lse_old = lse_ref[idx, :]
                    top = jnp.maximum(lse_old, lse_t)
                    w_old = jnp.exp2(lse_old - top)
                    w_t = jnp.exp2(lse_t - top)
                    den = w_old + w_t
                    o_new = (w_old * orun_ref[idx, :] + w_t * o_t) * (1.0 / den)
                    if last:
                        o_ref[idx, :] = o_new
                    else:
                        orun_ref[idx, :] = o_new
                        lse_ref[idx, :] = top + jnp.log2(den)
            return carry

        lax.fori_loop(0, ntile // group, tiles, 0)


def attn_c(proj, *, q_col, k_col, v_col):
    B, S, _ = proj.shape
    n = C_STEPS
    assert all(S % w == 0 and w // d == n for w, d in C_PATTERNS)
    assert (S // n) % C_GROUP == 0
    pairs = C_HEADS // 2
    kern = functools.partial(_attn_c_kernel, seq=S, group=C_GROUP)
    return pl.pallas_call(
        kern,
        grid=(B, pairs),
        in_specs=[_smem_spec(),
                  pl.BlockSpec((None, S, LANES), lambda b, p: (b, 0, q_col // LANES + p)),
                  pl.BlockSpec((None, S, LANES), lambda b, p: (b, 0, k_col // LANES + p)),
                  pl.BlockSpec((None, S, LANES), lambda b, p: (b, 0, v_col // LANES + p))],
        out_specs=pl.BlockSpec((None, S, LANES), lambda b, p: (b, 0, p)),
        out_shape=jax.ShapeDtypeStruct((B, S, C_HEADS * HEAD_DIM), F32),
        scratch_shapes=[pltpu.VMEM((S // n, n, LANES), BF16),
                        pltpu.VMEM((S // n, LANES, n), BF16),
                        pltpu.VMEM((2, 2, 2 * n, n), F32),
                        pltpu.VMEM((S, LANES), F32),
                        pltpu.VMEM((S, LANES), F32)],
        compiler_params=_params("parallel", "parallel"),
        name="attn_dilated",
    )(_alibi_slopes(C_HEADS), proj, proj, proj)


def _mix_out_kernel(oa_ref, ob_ref, oc_ref, ga_ref, gb_ref, gc_ref, h_ref, w_ref, o_ref):
    y = jnp.concatenate([_rms(oa_ref[...], ga_ref[...]).astype(BF16),
                         _rms(ob_ref[...], gb_ref[...]).astype(BF16),
                         _rms(oc_ref[...], gc_ref[...]).astype(BF16)], axis=-1)
    o_ref[...] = h_ref[...] + jnp.dot(y, w_ref[...], preferred_element_type=F32)


def mix_out(oa, ob, oc, ga, gb, gc, h, w, layer, *, tm):
    M, D = h.shape
    wa, wb, wc = oa.shape[1], ob.shape[1], oc.shape[1]
    K = wa + wb + wc
    assert w.shape[1:] == (K, D) and M % tm == 0
    row = lambda width: pl.BlockSpec((tm, width), lambda i: (i, 0))
    gain = lambda width: pl.BlockSpec((1, width), lambda i: (0, 0))
    return pl.pallas_call(
        _mix_out_kernel,
        grid=(M // tm,),
        in_specs=[row(wa), row(wb), row(wc), gain(wa), gain(wb), gain(wc), row(D),
                  pl.BlockSpec((None, K, D), lambda i: (layer, 0, 0))],
        out_specs=row(D),
        out_shape=jax.ShapeDtypeStruct((M, D), F32),
        compiler_params=_params("parallel"),
        name="mix_out",
    )(oa, ob, oc, ga, gb, gc, h, w)


def _xattn_kernel(h_ref, g_ref, wq_ref, kv_ref, wo_ref, o_ref):
    xw = X_HEADS * X_HEAD_DIM
    h = h_ref[...]
    xn = _rms(h, g_ref[...]).astype(BF16)
    q = jnp.dot(xn, wq_ref[...], preferred_element_type=F32) * (X_HEAD_DIM ** -0.5)
    heads = []
    for hd in range(X_HEADS):
        lanes = slice(hd * X_HEAD_DIM, (hd + 1) * X_HEAD_DIM)
        s = _dot_nt(q[:, lanes].astype(BF16), kv_ref[:, lanes])
        m = jnp.max(s, axis=-1, keepdims=True)
        p = jnp.exp(s - m)
        l = jnp.sum(p, axis=-1, keepdims=True)
        v = kv_ref[:, xw + hd * X_HEAD_DIM:xw + (hd + 1) * X_HEAD_DIM]
        heads.append(jnp.dot(p.astype(BF16), v, preferred_element_type=F32) / l)
    o = jnp.concatenate(heads, axis=-1).astype(BF16)
    o_ref[...] = h + jnp.dot(o, wo_ref[...], preferred_element_type=F32)


def xattn(h, g, wq, kv, wo, layer, *, tq):
    B, S, D = h.shape
    Mm = kv.shape[1]
    xw = X_HEADS * X_HEAD_DIM
    assert S % tq == 0
    return pl.pallas_call(
        _xattn_kernel,
        grid=(B, S // tq),
        in_specs=[pl.BlockSpec((None, tq, D), lambda b, i: (b, i, 0)),
                  pl.BlockSpec((1, D), lambda b, i: (0, 0)),
                  pl.BlockSpec((None, D, xw), lambda b, i: (layer, 0, 0)),
                  pl.BlockSpec((None, Mm, 2 * xw), lambda b, i: (b, 0, 0)),
                  pl.BlockSpec((None, xw, D), lambda b, i: (layer, 0, 0))],
        out_specs=pl.BlockSpec((None, tq, D), lambda b, i: (b, i, 0)),
        out_shape=jax.ShapeDtypeStruct((B, S, D), F32),
        compiler_params=_params("parallel", "parallel"),
        name="xattn",
    )(h, g, wq, kv, wo)


def _ffn_kernel(h_ref, g_ref, wg_ref, wu_ref, wd_ref, gf_ref, o_ref, xn_ref, *, final_norm):
    f = pl.program_id(1)

    @pl.when(f == 0)
    def _():
        h = h_ref[...]
        xn_ref[...] = _rms(h, g_ref[...]).astype(BF16)
        o_ref[...] = h

    xn = xn_ref[...]
    gate = jnp.dot(xn, wg_ref[...], preferred_element_type=F32)
    up = jnp.dot(xn, wu_ref[...], preferred_element_type=F32)
    act = (gate / (1.0 + jnp.exp(-gate))) * up
    o_ref[...] += jnp.dot(act.astype(BF16), wd_ref[...], preferred_element_type=F32)

    if final_norm:
        @pl.when(f == pl.num_programs(1) - 1)
        def _():
            o_ref[...] = _rms(o_ref[...], gf_ref[...])


def ffn(h, g, wg, wu, wd, gf, layer, *, tm, tf, final_norm):
    M, D = h.shape
    Fd = wg.shape[2]
    assert M % tm == 0 and Fd % tf == 0
    kern = functools.partial(_ffn_kernel, final_norm=final_norm)
    return pl.pallas_call(
        kern,
        grid=(M // tm, Fd // tf),
        in_specs=[pl.BlockSpec((tm, D), lambda i, f: (i, 0)),
                  pl.BlockSpec((1, D), lambda i, f: (0, 0)),
                  pl.BlockSpec((None, D, tf), lambda i, f: (layer, 0, f)),
                  pl.BlockSpec((None, D, tf), lambda i, f: (layer, 0, f)),
                  pl.BlockSpec((None, tf, D), lambda i, f: (layer, f, 0)),
                  pl.BlockSpec((1, D), lambda i, f: (0, 0))],
        out_specs=pl.BlockSpec((tm, D), lambda i, f: (i, 0)),
        out_shape=jax.ShapeDtypeStruct((M, D), F32),
        scratch_shapes=[pltpu.VMEM((tm, D), BF16)],
        compiler_params=_params("parallel", "arbitrary"),
        name="ffn",
    )(h, g, wg, wu, wd, gf)


def kernel(x, mem, norm_mix, w_in, sinks, gain_a, gain_b, gain_c, w_out, norm_xattn, norm_mem,
           wq_x, wk_x, wv_x, wo_x, norm_ffn, w_gate, w_up, w_down, norm_final):
    B, S, D = x.shape
    depth = w_in.shape[0]
    Mm = mem.shape[1]
    qa_w, kva_w = A_HEADS * HEAD_DIM, A_KV_HEADS * HEAD_DIM
    b_w, c_w = B_HEADS * HEAD_DIM, C_HEADS * HEAD_DIM
    widths = (qa_w, kva_w, kva_w, b_w, b_w, b_w, c_w, c_w, c_w)
    cols = [int(c) for c in np.concatenate([[0], np.cumsum(widths)[:-1]])]
    in_w = sum(widths)

    row = lambda v: v.reshape(1, -1)
    h = x.reshape(B * S, D)
    mem2 = mem.reshape(B * Mm, D)
    w_in_b, w_out_b = w_in.astype(BF16), w_out.astype(BF16)
    w_kv_b = jnp.concatenate([wk_x, wv_x], axis=2).astype(BF16)
    wq_b, wo_b = wq_x.astype(BF16), wo_x.astype(BF16)
    wg_b, wu_b, wd_b = w_gate.astype(BF16), w_up.astype(BF16), w_down.astype(BF16)
    for l in range(depth):
        proj = norm_matmul(h, row(norm_mix[l]), w_in_b, l, tm=1024, tn=1792, out_dtype=F32)
        proj = proj.reshape(B, S, in_w)
        oa = attn_a(proj, sinks[l], q_col=cols[0], k_col=cols[1], v_col=cols[2], tq=512)
        ob = attn_b(proj, q_col=cols[3], k_col=cols[4], v_col=cols[5])
        oc = attn_c(proj, q_col=cols[6], k_col=cols[7], v_col=cols[8])
        h = mix_out(oa.reshape(B * S, qa_w), ob.reshape(B * S, b_w), oc.reshape(B * S, c_w),
                    row(gain_a[l]), row(gain_b[l]), row(gain_c[l]), h, w_out_b, l, tm=512)
        kv = norm_matmul(mem2, row(norm_mem[l]), w_kv_b, l, tm=512, tn=w_kv_b.shape[2], out_dtype=BF16)
        h = xattn(h.reshape(B, S, D), row(norm_xattn[l]), wq_b, kv.reshape(B, Mm, -1), wo_b, l,
                  tq=512).reshape(B * S, D)
        h = ffn(h, row(norm_ffn[l]), wg_b, wu_b, wd_b, row(norm_final), l, tm=1024, tf=512,
                final_norm=(l == depth - 1))
    return h.reshape(B, S, D)
```

```python
import functools

import numpy as np
import jax
import jax.numpy as jnp
from jax import lax
from jax.experimental import pallas as pl
from jax.experimental.pallas import tpu as pltpu

F32 = jnp.float32
BF16 = jnp.bfloat16
NEG_INF = float("-inf")
POS_INF = float("inf")
LOG2E = 1.4426950408889634

EPS = 1e-5
HEAD_DIM = 64
LANES = 128
BF16_ROWS = 16
A_HEADS, A_KV_HEADS, A_WINDOW, A_BLOCK = 8, 2, 128, 128
B_HEADS, MOBA_BLOCK, MOBA_TOPK = 8, 256, 3
C_HEADS = 16
C_PATTERNS = ((128, 1), (512, 4), (2048, 16))
C_STEPS = 128
C_GROUP = 16
X_HEADS, X_HEAD_DIM = 4, 128
VMEM_LIMIT = 56 * 1024 * 1024
VT_ROWS = HEAD_DIM + BF16_ROWS


def _alibi_slopes(n):
    return jnp.asarray(2.0 ** (-8.0 * np.arange(1, n + 1) / n), F32)


def _params(*sem):
    return pltpu.CompilerParams(dimension_semantics=sem, vmem_limit_bytes=VMEM_LIMIT)


def _rms(x, g):
    return x * lax.rsqrt(jnp.mean(x * x, axis=-1, keepdims=True) + EPS) * g


def _dot_nt(a, b, precision=None):
    return lax.dot_general(a, b, (((1,), (1,)), ((), ())), precision=precision,
                           preferred_element_type=F32)


def _smem_spec():
    return pl.BlockSpec(memory_space=pltpu.SMEM)


def _zero_after(x):
    bits = pltpu.bitcast(x, jnp.uint32)
    return pltpu.bitcast(lax.shift_right_logical(bits, jnp.uint32(32)), F32)


def _norm_matmul_kernel(x_ref, g_ref, w_ref, o_ref, xn_ref):
    @pl.when(pl.program_id(1) == 0)
    def _():
        xn_ref[...] = _rms(x_ref[...], g_ref[...]).astype(BF16)

    o_ref[...] = jnp.dot(xn_ref[...], w_ref[...], preferred_element_type=F32).astype(o_ref.dtype)


def norm_matmul(x, g, w, layer, *, tm, tn, out_dtype):
    M, D = x.shape
    N = w.shape[2]
    assert M % tm == 0 and N % tn == 0
    return pl.pallas_call(
        _norm_matmul_kernel,
        grid=(M // tm, N // tn),
        in_specs=[pl.BlockSpec((tm, D), lambda i, j: (i, 0)),
                  pl.BlockSpec((1, D), lambda i, j: (0, 0)),
                  pl.BlockSpec((None, D, tn), lambda i, j: (layer, 0, j))],
        out_specs=pl.BlockSpec((tm, tn), lambda i, j: (i, j)),
        out_shape=jax.ShapeDtypeStruct((M, N), out_dtype),
        scratch_shapes=[pltpu.VMEM((tm, D), BF16)],
        compiler_params=_params("parallel", "arbitrary"),
        name="norm_matmul",
    )(x, g, w)


def _attn_a_kernel(slopes_ref, sinks_ref, q_ref, k_ref, v_ref, o_ref, kb_ref, vt_ref, bias_ref,
                   *, seq, n_sub, group):
    n = A_BLOCK
    qi = pl.program_id(1)
    heads_per_kv = A_HEADS // A_KV_HEADS
    lo_half = lax.broadcasted_iota(jnp.int32, (n, LANES), 1) < HEAD_DIM
    ones = jnp.ones((BF16_ROWS, 2 * n), BF16)

    @pl.when(qi == 0)
    def _():
        kk = lax.broadcasted_iota(jnp.int32, (2 * n, n), 0)
        qq = lax.broadcasted_iota(jnp.int32, (2 * n, n), 1)
        dist = qq + n - kk
        valid = (dist >= 0) & (dist < A_WINDOW)
        distf = dist.astype(F32)
        for j in range(A_HEADS):
            full = jnp.where(valid, (-slopes_ref[j] * LOG2E) * distf, NEG_INF)
            bias_ref[j, 1] = full
            bias_ref[j, 0] = jnp.where(kk >= n, full, NEG_INF)

        def prep(u, carry):
            for sub in range(4):
                rows = pl.ds(pl.multiple_of((4 * u + sub) * n, n), n)
                kb_ref[4 * u + sub] = k_ref[rows, :].astype(BF16)
                vt_ref[4 * u + sub] = v_ref[rows, :].T.astype(BF16)
            return carry

        lax.fori_loop(0, seq // (4 * n), prep, 0)

    def blocks(u, carry):
        subs = [u * group + g for g in range(group)]
        ts, prevs, variants, scores = [], [], [], []
        for sb in subs:
            t = qi * n_sub + sb
            ts.append(t)
            prevs.append(jnp.maximum(t - 1, 0))
            variants.append(jnp.where(t > 0, 1, 0))
            rows = pl.ds(pl.multiple_of(sb * n, n), n)
            stack = []
            for j in range(A_HEADS):
                qj = q_ref[rows, (j // 2) * LANES:(j // 2 + 1) * LANES] * (HEAD_DIM ** -0.5 * LOG2E)
                kv = j // heads_per_kv
                if j % 2 != kv:
                    qj = pltpu.roll(qj, HEAD_DIM, axis=1)
                stack.append(jnp.where(lo_half if kv == 0 else ~lo_half, qj, 0.0))
            q_all = jnp.concatenate(stack, axis=0).astype(BF16)
            kcat = jnp.concatenate([kb_ref[prevs[-1]], kb_ref[t]], axis=0)
            scores.append(_dot_nt(kcat, q_all))
        anchor = sum(_zero_after(sc[0:1, :n]) for sc in scores[1:]) if group > 1 else None
        for g, sb in enumerate(subs):
            rows = pl.ds(pl.multiple_of(sb * n, n), n)
            vtcat = jnp.concatenate([vt_ref[prevs[g]], vt_ref[ts[g]]], axis=1)
            outs = []
            for kv in range(A_KV_HEADS):
                probs, maxes = [], []
                for j in range(kv * heads_per_kv, (kv + 1) * heads_per_kv):
                    s = scores[g][:, j * n:(j + 1) * n] + bias_ref[j, variants[g]]
                    m = jnp.maximum(jnp.max(s, axis=0, keepdims=True), sinks_ref[j] * LOG2E)
                    if g == 0 and j == 0 and anchor is not None:
                        m = m + anchor
                    probs.append(jnp.exp2(s - m).astype(BF16))
                    maxes.append(m)
                vaug = jnp.concatenate([vtcat[kv * HEAD_DIM:(kv + 1) * HEAD_DIM], ones], axis=0)
                acc = jnp.dot(vaug, jnp.concatenate(probs, axis=1), preferred_element_type=F32)
                for c, j in enumerate(range(kv * heads_per_kv, (kv + 1) * heads_per_kv)):
                    cols = slice(c * n, (c + 1) * n)
                    denom = acc[HEAD_DIM:HEAD_DIM + 1, cols] + jnp.exp2(sinks_ref[j] * LOG2E - maxes[c])
                    outs.append(acc[:HEAD_DIM, cols] * (1.0 / denom))
            for pair in range(A_HEADS // 2):
                o_ref[rows, pair * LANES:(pair + 1) * LANES] = jnp.concatenate(outs[2 * pair:2 * pair + 2], axis=0).T
        return carry

    lax.fori_loop(0, n_sub // group, blocks, 0)


def attn_a(proj, sinks, *, q_col, k_col, v_col, tq, group):
    B, S, _ = proj.shape
    n = A_BLOCK
    qw = A_HEADS * HEAD_DIM
    assert S % tq == 0 and tq % (n * group) == 0 and S % (4 * n) == 0
    assert q_col % qw == 0 and k_col % LANES == 0 and v_col % LANES == 0
    assert A_WINDOW == n and (A_HEADS // A_KV_HEADS) % 2 == 0
    kern = functools.partial(_attn_a_kernel, seq=S, n_sub=tq // n, group=group)
    return pl.pallas_call(
        kern,
        grid=(B, S // tq),
        in_specs=[_smem_spec(), _smem_spec(),
                  pl.BlockSpec((None, tq, qw), lambda b, i: (b, i, q_col // qw)),
                  pl.BlockSpec((None, S, LANES), lambda b, i: (b, 0, k_col // LANES)),
                  pl.BlockSpec((None, S, LANES), lambda b, i: (b, 0, v_col // LANES))],
        out_specs=pl.BlockSpec((None, tq, qw), lambda b, i: (b, i, 0)),
        out_shape=jax.ShapeDtypeStruct((B, S, qw), F32),
        scratch_shapes=[pltpu.VMEM((S // n, n, LANES), BF16),
                        pltpu.VMEM((S // n, LANES, n), BF16),
                        pltpu.VMEM((A_HEADS, 2, 2 * n, n), F32)],
        compiler_params=_params("parallel", "arbitrary"),
        name="attn_swa",
    )(_alibi_slopes(A_HEADS), sinks, proj, proj, proj)


def _attn_b_kernel(slopes_ref, q_ref, k_ref, v_ref, o_ref,
                   kb_ref, vt_ref, kmean_ref, kcat_ref, bias_ref, qs_ref, sel_ref, acc_ref, m_ref, s_ref,
                   *, nb):
    mb = MOBA_BLOCK
    hp = pl.program_id(1)
    slope2 = [slopes_ref[2 * hp + h] * LOG2E for h in range(2)]

    def prepare():
        kpos = lax.broadcasted_iota(jnp.int32, (mb, mb), 0)
        qpos = lax.broadcasted_iota(jnp.int32, (mb, mb), 1)
        relf = (qpos - kpos).astype(F32)
        for h in range(2):
            alibi = (-slope2[h]) * relf
            bias_ref[h, 0] = alibi
            bias_ref[h, 1] = jnp.where(qpos >= kpos, alibi, NEG_INF)
        ones = jnp.ones((BF16_ROWS, mb), BF16)

        def prep(n, carry):
            rows = pl.ds(pl.multiple_of(n * mb, mb), mb)
            kblk = k_ref[rows, :]
            kb_ref[n] = kblk.astype(BF16)
            kmean_ref[pl.ds(n, 1), :] = jnp.mean(kblk, axis=0, keepdims=True)
            vt = v_ref[rows, :].T
            for h in range(2):
                vt_ref[h, n, :HEAD_DIM, :] = vt[h * HEAD_DIM:(h + 1) * HEAD_DIM].astype(BF16)
                vt_ref[h, n, HEAD_DIM:, :] = ones
            return carry

        lax.fori_loop(0, nb, prep, 0)
        km = kmean_ref[...]
        lo_lanes = lax.broadcasted_iota(jnp.int32, (nb, LANES), 1) < HEAD_DIM
        for h in range(2):
            kmh = jnp.where(lo_lanes if h == 0 else ~lo_lanes, km, 0.0)
            hi = kmh.astype(BF16)
            lo = (kmh - hi.astype(F32)).astype(BF16)
            kcat_ref[h * nb:(h + 1) * nb, :] = jnp.concatenate([hi, hi, lo], axis=1)

    prepare()
    lo_half = lax.broadcasted_iota(jnp.int32, (mb, LANES), 1) < HEAD_DIM

    def scaled_queries(i):
        qc = q_ref[pl.ds(pl.multiple_of(i * mb, mb), mb), :] * (HEAD_DIM ** -0.5 * LOG2E)
        return jnp.concatenate([jnp.where(lo_half, qc, 0.0), jnp.where(lo_half, 0.0, qc)], axis=0).astype(BF16)

    q_first = scaled_queries(0)
    qs_ref[0] = q_first
    s_ref[...] = _dot_nt(jnp.concatenate([kb_ref[0], kb_ref[0]], axis=0), q_first)

    def query_block(i, carry0):
        qrows = pl.ds(pl.multiple_of(i * mb, mb), mb)
        q = q_ref[qrows, :]
        slot = i % 2
        i_next = jnp.minimum(i + 1, nb - 1)
        qs_ref[1 - slot] = scaled_queries(i_next)
        blk = lax.broadcasted_iota(jnp.int32, (nb, mb), 0)
        past = blk < i
        q_hi = q.astype(BF16)
        q_lo = (q - q_hi.astype(F32)).astype(BF16)
        gate = _dot_nt(kcat_ref[...], jnp.concatenate([q_hi, q_lo, q_hi], axis=1))

        for h in range(2):
            g = jnp.where(past, gate[h * nb:(h + 1) * nb], NEG_INF)
            avail = blk >= 0
            sel = jnp.where(blk == i, 1.0, 0.0)
            for _ in range(MOBA_TOPK):
                mx = jnp.max(jnp.where(avail, g, NEG_INF), axis=0, keepdims=True)
                is_max = avail & (g == mx)
                first = jnp.min(jnp.where(is_max, blk, nb), axis=0, keepdims=True)
                pick = blk == first
                sel = jnp.where(pick & past, 1.0, sel)
                avail = avail & ~pick
            sel_ref[h, :nb] = sel
            sel_ref[h, nb:] = jnp.zeros((8, mb), F32)

        m_ref[...] = jnp.full(m_ref.shape, NEG_INF, F32)
        acc_ref[...] = jnp.zeros(acc_ref.shape, F32)
        nsteps = (i + 2) // 2

        def step(p, carry):
            first = p == 0
            j0 = jnp.where(first, i, 2 * p - 1)
            j1 = jnp.minimum(2 * p, nb - 1)
            row1 = jnp.where(2 * p < i, j1, nb)
            variant0 = jnp.where(first, 1, 0)
            last = p == nsteps - 1
            n0 = jnp.where(last, i_next, jnp.minimum(2 * p + 1, nb - 1))
            n1 = jnp.where(last, 0, jnp.minimum(2 * p + 2, nb - 1))
            q_slot = jnp.where(last, 1 - slot, slot)
            s_next = _dot_nt(jnp.concatenate([kb_ref[n0], kb_ref[n1]], axis=0), qs_ref[q_slot])
            dist_a = ((i - j0) * mb).astype(F32)
            dist_b = ((i - j1) * mb).astype(F32)
            for h in range(2):
                cols = slice(h * mb, (h + 1) * mb)
                shift_a = -slope2[h] * dist_a
                shift_b = -slope2[h] * dist_b
                sa = s_ref[:mb, cols] + bias_ref[h, variant0]
                sb = s_ref[mb:, cols] + bias_ref[h, 0]
                chosen_a = sel_ref[h, pl.ds(j0, 1), :] > 0.5
                chosen_b = sel_ref[h, pl.ds(row1, 1), :] > 0.5
                m_a = jnp.where(chosen_a, jnp.max(sa, axis=0, keepdims=True) + shift_a, NEG_INF)
                m_b = jnp.where(chosen_b, jnp.max(sb, axis=0, keepdims=True) + shift_b, NEG_INF)
                m_old = m_ref[h]
                m_new = jnp.maximum(m_old, jnp.maximum(m_a, m_b))
                alpha = jnp.exp2(m_old - m_new)
                pa = jnp.exp2(sa - jnp.where(chosen_a, m_new - shift_a, POS_INF)).astype(BF16)
                pb = jnp.exp2(sb - jnp.where(chosen_b, m_new - shift_b, POS_INF)).astype(BF16)
                vt = jnp.concatenate([vt_ref[h, j0], vt_ref[h, j1]], axis=1)
                acc_ref[h] = alpha * acc_ref[h] + jnp.dot(vt, jnp.concatenate([pa, pb], axis=0),
                                                          preferred_element_type=F32)
                m_ref[h] = m_new
            s_ref[...] = s_next
            return carry

        lax.fori_loop(0, nsteps, step, 0)
        outs = []
        for h in range(2):
            acc = acc_ref[h]
            outs.append(acc[:HEAD_DIM] * (1.0 / acc[HEAD_DIM:HEAD_DIM + 1]))
        o_ref[qrows, :] = jnp.concatenate(outs, axis=0).T
        return carry0

    lax.fori_loop(0, nb, query_block, 0)


def attn_b(proj, *, q_col, k_col, v_col):
    B, S, _ = proj.shape
    mb = MOBA_BLOCK
    assert S % mb == 0
    nb = S // mb
    assert MOBA_TOPK < nb and nb % 8 == 0
    pairs = B_HEADS // 2
    kern = functools.partial(_attn_b_kernel, nb=nb)
    return pl.pallas_call(
        kern,
        grid=(B, pairs),
        in_specs=[_smem_spec(),
                  pl.BlockSpec((None, S, LANES), lambda b, p: (b, 0, q_col // LANES + p)),
                  pl.BlockSpec((None, S, LANES), lambda b, p: (b, 0, k_col // LANES + p)),
                  pl.BlockSpec((None, S, LANES), lambda b, p: (b, 0, v_col // LANES + p))],
        out_specs=pl.BlockSpec((None, S, LANES), lambda b, p: (b, 0, p)),
        out_shape=jax.ShapeDtypeStruct((B, S, B_HEADS * HEAD_DIM), F32),
        scratch_shapes=[pltpu.VMEM((nb, mb, LANES), BF16),
                        pltpu.VMEM((2, nb, VT_ROWS, mb), BF16),
                        pltpu.VMEM((nb, LANES), F32),
                        pltpu.VMEM((2 * nb, 3 * LANES), BF16),
                        pltpu.VMEM((2, 2, mb, mb), F32),
                        pltpu.VMEM((2, 2 * mb, LANES), BF16),
                        pltpu.VMEM((2, nb + 8, mb), F32),
                        pltpu.VMEM((2, VT_ROWS, mb), F32),
                        pltpu.VMEM((2, 1, mb), F32),
                        pltpu.VMEM((2 * mb, 2 * mb), F32)],
        compiler_params=_params("parallel", "parallel"),
        name="attn_moba",
    )(_alibi_slopes(B_HEADS), proj, proj, proj)


def _attn_c_kernel(slopes_ref, q_ref, k_ref, v_ref, o_ref,
                   kb_ref, vt_ref, bias_ref, orun_ref, lse_ref, *, seq, group):
    n = C_STEPS
    hp = pl.program_id(1)
    ntile = seq // n
    kk = lax.broadcasted_iota(jnp.int32, (2 * n, n), 0)
    qq = lax.broadcasted_iota(jnp.int32, (2 * n, n), 1)
    step_i = qq + n - kk
    valid = (step_i >= 0) & (step_i <= n)
    stepf = step_i.astype(F32)
    lo_half = lax.broadcasted_iota(jnp.int32, (n, LANES), 1) < HEAD_DIM
    ones = jnp.ones((BF16_ROWS, 2 * n), BF16)

    for bi, (w, d) in enumerate(C_PATTERNS):
        nblk = seq // w
        first, last = bi == 0, bi == len(C_PATTERNS) - 1

        def rows_at(t, d=d, w=w, nblk=nblk):
            start = (t % nblk) * w + t // nblk
            return pl.ds(start, n) if d == 1 else pl.ds(start, n, stride=d)

        for h in range(2):
            full = jnp.where(valid, (-slopes_ref[2 * hp + h] * (LOG2E * d)) * stepf, NEG_INF)
            bias_ref[h, 1] = full
            bias_ref[h, 0] = jnp.where(kk >= n, full, NEG_INF)

        def prep(u, carry, rows_at=rows_at):
            for sub in range(group):
                t = u * group + sub
                kb_ref[t] = k_ref[rows_at(t), :].astype(BF16)
                vt_ref[t] = v_ref[rows_at(t), :].T.astype(BF16)
            return carry

        lax.fori_loop(0, ntile // group, prep, 0)

        def tiles(u, carry, rows_at=rows_at, nblk=nblk, first=first, last=last):
            ts = [u * group + sub for sub in range(group)]
            prevs, variants, scores = [], [], []
            for t in ts:
                has_prev = (t % nblk) > 0
                tp = jnp.where(has_prev, t - 1, t)
                prevs.append(tp)
                variants.append(jnp.where(has_prev, 1, 0))
                q = q_ref[rows_at(t), :] * (HEAD_DIM ** -0.5 * LOG2E)
                q_both = jnp.concatenate([jnp.where(lo_half, q, 0.0), jnp.where(lo_half, 0.0, q)], axis=0)
                kcat = jnp.concatenate([kb_ref[tp], kb_ref[t]], axis=0)
                scores.append(_dot_nt(kcat, q_both.astype(BF16)))
            anchor = sum(_zero_after(sc[0:1, :n]) for sc in scores[1:])
            results = []
            for g, t in enumerate(ts):
                vtcat = jnp.concatenate([vt_ref[prevs[g]], vt_ref[t]], axis=1)
                o_rows, lse_rows = [], []
                for h in range(2):
                    s = scores[g][:, h * n:(h + 1) * n] + bias_ref[h, variants[g]]
                    m = jnp.max(s, axis=0, keepdims=True)
                    if g == 0:
                        m = m + anchor
                    p = jnp.exp2(s - m).astype(BF16)
                    vaug = jnp.concatenate([vtcat[h * HEAD_DIM:(h + 1) * HEAD_DIM], ones], axis=0)
                    acc = jnp.dot(vaug, p, preferred_element_type=F32)
                    l = acc[HEAD_DIM:HEAD_DIM + 1]
                    o_rows.append(acc[:HEAD_DIM] * (1.0 / l))
                    lse_rows.append(jnp.broadcast_to(m + jnp.log2(l), (HEAD_DIM, n)))
                results.append((jnp.concatenate(o_rows, axis=0), jnp.concatenate(lse_rows, axis=0)))
            for g, t in enumerate(ts):
                idx = rows_at(t)
                o_t = results[g][0].T
                lse_t = results[g][1].T
                if first:
                    orun_ref[idx, :] = o_t
                    lse_ref[idx, :] = lse_t
                else:
                    lse_old = lse_ref[idx, :]
                    top = jnp.maximum(lse_old, lse_t)
                    w_old = jnp.exp2(lse_old - top)
                    w_t = jnp.exp2(lse_t - top)
                    den = w_old + w_t
                    o_new = (w_old * orun_ref[idx, :] + w_t * o_t) * (1.0 / den)
                    if last:
                        o_ref[idx, :] = o_new
                    else:
                        orun_ref[idx, :] = o_new
                        lse_ref[idx, :] = top + jnp.log2(den)
            return carry

        lax.fori_loop(0, ntile // group, tiles, 0)


def attn_c(proj, *, q_col, k_col, v_col):
    B, S, _ = proj.shape
    n = C_STEPS
    assert all(S % w == 0 and w // d == n for w, d in C_PATTERNS)
    assert (S // n) % C_GROUP == 0
    pairs = C_HEADS // 2
    kern = functools.partial(_attn_c_kernel, seq=S, group=C_GROUP)
    return pl.pallas_call(
        kern,
        grid=(B, pairs),
        in_specs=[_smem_spec(),
                  pl.BlockSpec((None, S, LANES), lambda b, p: (b, 0, q_col // LANES + p)),
                  pl.BlockSpec((None, S, LANES), lambda b, p: (b, 0, k_col // LANES + p)),
                  pl.BlockSpec((None, S, LANES), lambda b, p: (b, 0, v_col // LANES + p))],
        out_specs=pl.BlockSpec((None, S, LANES), lambda b, p: (b, 0, p)),
        out_shape=jax.ShapeDtypeStruct((B, S, C_HEADS * HEAD_DIM), F32),
        scratch_shapes=[pltpu.VMEM((S // n, n, LANES), BF16),
                        pltpu.VMEM((S // n, LANES, n), BF16),
                        pltpu.VMEM((2, 2, 2 * n, n), F32),
                        pltpu.VMEM((S, LANES), F32),
                        pltpu.VMEM((S, LANES), F32)],
        compiler_params=_params("parallel", "parallel"),
        name="attn_dilated",
    )(_alibi_slopes(C_HEADS), proj, proj, proj)


def _mix_out_kernel(oa_ref, ob_ref, oc_ref, ga_ref, gb_ref, gc_ref, h_ref, w_ref, o_ref):
    y = jnp.concatenate([_rms(oa_ref[...], ga_ref[...]).astype(BF16),
                         _rms(ob_ref[...], gb_ref[...]).astype(BF16),
                         _rms(oc_ref[...], gc_ref[...]).astype(BF16)], axis=-1)
    o_ref[...] = h_ref[...] + jnp.dot(y, w_ref[...], preferred_element_type=F32)


def mix_out(oa, ob, oc, ga, gb, gc, h, w, layer, *, tm):
    M, D = h.shape
    wa, wb, wc = oa.shape[1], ob.shape[1], oc.shape[1]
    K = wa + wb + wc
    assert w.shape[1:] == (K, D) and M % tm == 0
    row = lambda width: pl.BlockSpec((tm, width), lambda i: (i, 0))
    gain = lambda width: pl.BlockSpec((1, width), lambda i: (0, 0))
    return pl.pallas_call(
        _mix_out_kernel,
        grid=(M // tm,),
        in_specs=[row(wa), row(wb), row(wc), gain(wa), gain(wb), gain(wc), row(D),
                  pl.BlockSpec((None, K, D), lambda i: (layer, 0, 0))],
        out_specs=row(D),
        out_shape=jax.ShapeDtypeStruct((M, D), F32),
        compiler_params=_params("parallel"),
        name="mix_out",
    )(oa, ob, oc, ga, gb, gc, h, w)


def _xattn_kernel(h_ref, g_ref, wq_ref, kv_ref, wo_ref, o_ref):
    xw = X_HEADS * X_HEAD_DIM
    h = h_ref[...]
    xn = _rms(h, g_ref[...]).astype(BF16)
    q = jnp.dot(xn, wq_ref[...], preferred_element_type=F32) * (X_HEAD_DIM ** -0.5)
    heads = []
    for hd in range(X_HEADS):
        lanes = slice(hd * X_HEAD_DIM, (hd + 1) * X_HEAD_DIM)
        s = _dot_nt(q[:, lanes].astype(BF16), kv_ref[:, lanes])
        m = jnp.max(s, axis=-1, keepdims=True)
        p = jnp.exp(s - m)
        l = jnp.sum(p, axis=-1, keepdims=True)
        v = kv_ref[:, xw + hd * X_HEAD_DIM:xw + (hd + 1) * X_HEAD_DIM]
        heads.append(jnp.dot(p.astype(BF16), v, preferred_element_type=F32) / l)
    o = jnp.concatenate(heads, axis=-1).astype(BF16)
    o_ref[...] = h + jnp.dot(o, wo_ref[...], preferred_element_type=F32)


def xattn(h, g, wq, kv, wo, layer, *, tq):
    B, S, D = h.shape
    Mm = kv.shape[1]
    xw = X_HEADS * X_HEAD_DIM
    assert S % tq == 0
    return pl.pallas_call(
        _xattn_kernel,
        grid=(B, S // tq),
        in_specs=[pl.BlockSpec((None, tq, D), lambda b, i: (b, i, 0)),
                  pl.BlockSpec((1, D), lambda b, i: (0, 0)),
                  pl.BlockSpec((None, D, xw), lambda b, i: (layer, 0, 0)),
                  pl.BlockSpec((None, Mm, 2 * xw), lambda b, i: (b, 0, 0)),
                  pl.BlockSpec((None, xw, D), lambda b, i: (layer, 0, 0))],
        out_specs=pl.BlockSpec((None, tq, D), lambda b, i: (b, i, 0)),
        out_shape=jax.ShapeDtypeStruct((B, S, D), F32),
        compiler_params=_params("parallel", "parallel"),
        name="xattn",
    )(h, g, wq, kv, wo)


def _ffn_kernel(h_ref, g_ref, wg_ref, wu_ref, wd_ref, gf_ref, o_ref, xn_ref, *, final_norm):
    f = pl.program_id(1)

    @pl.when(f == 0)
    def _():
        h = h_ref[...]
        xn_ref[...] = _rms(h, g_ref[...]).astype(BF16)
        o_ref[...] = h

    xn = xn_ref[...]
    gate = jnp.dot(xn, wg_ref[...], preferred_element_type=F32)
    up = jnp.dot(xn, wu_ref[...], preferred_element_type=F32)
    act = (gate / (1.0 + jnp.exp(-gate))) * up
    o_ref[...] += jnp.dot(act.astype(BF16), wd_ref[...], preferred_element_type=F32)

    if final_norm:
        @pl.when(f == pl.num_programs(1) - 1)
        def _():
            o_ref[...] = _rms(o_ref[...], gf_ref[...])


def ffn(h, g, wg, wu, wd, gf, layer, *, tm, tf, final_norm):
    M, D = h.shape
    Fd = wg.shape[2]
    assert M % tm == 0 and Fd % tf == 0
    kern = functools.partial(_ffn_kernel, final_norm=final_norm)
    return pl.pallas_call(
        kern,
        grid=(M // tm, Fd // tf),
        in_specs=[pl.BlockSpec((tm, D), lambda i, f: (i, 0)),
                  pl.BlockSpec((1, D), lambda i, f: (0, 0)),
                  pl.BlockSpec((None, D, tf), lambda i, f: (layer, 0, f)),
                  pl.BlockSpec((None, D, tf), lambda i, f: (layer, 0, f)),
                  pl.BlockSpec((None, tf, D), lambda i, f: (layer, f, 0)),
                  pl.BlockSpec((1, D), lambda i, f: (0, 0))],
        out_specs=pl.BlockSpec((tm, D), lambda i, f: (i, 0)),
        out_shape=jax.ShapeDtypeStruct((M, D), F32),
        scratch_shapes=[pltpu.VMEM((tm, D), BF16)],
        compiler_params=_params("parallel", "arbitrary"),
        name="ffn",
    )(h, g, wg, wu, wd, gf)


def kernel(x, mem, norm_mix, w_in, sinks, gain_a, gain_b, gain_c, w_out, norm_xattn, norm_mem,
           wq_x, wk_x, wv_x, wo_x, norm_ffn, w_gate, w_up, w_down, norm_final):
    B, S, D = x.shape
    depth = w_in.shape[0]
    Mm = mem.shape[1]
    qa_w, kva_w = A_HEADS * HEAD_DIM, A_KV_HEADS * HEAD_DIM
    b_w, c_w = B_HEADS * HEAD_DIM, C_HEADS * HEAD_DIM
    widths = (qa_w, kva_w, kva_w, b_w, b_w, b_w, c_w, c_w, c_w)
    cols = [int(c) for c in np.concatenate([[0], np.cumsum(widths)[:-1]])]
    in_w = sum(widths)

    row = lambda v: v.reshape(1, -1)
    h = x.reshape(B * S, D)
    mem2 = mem.reshape(B * Mm, D)
    w_in_b, w_out_b = w_in.astype(BF16), w_out.astype(BF16)
    w_kv_b = jnp.concatenate([wk_x, wv_x], axis=2).astype(BF16)
    wq_b, wo_b = wq_x.astype(BF16), wo_x.astype(BF16)
    wg_b, wu_b, wd_b = w_gate.astype(BF16), w_up.astype(BF16), w_down.astype(BF16)
    for l in range(depth):
        proj = norm_matmul(h, row(norm_mix[l]), w_in_b, l, tm=1024, tn=1792, out_dtype=F32)
        proj = proj.reshape(B, S, in_w)
        oa = attn_a(proj, sinks[l], q_col=cols[0], k_col=cols[1], v_col=cols[2], tq=1024, group=4)
        ob = attn_b(proj, q_col=cols[3], k_col=cols[4], v_col=cols[5])
        oc = attn_c(proj, q_col=cols[6], k_col=cols[7], v_col=cols[8])
        h = mix_out(oa.reshape(B * S, qa_w), ob.reshape(B * S, b_w), oc.reshape(B * S, c_w),
                    row(gain_a[l]), row(gain_b[l]), row(gain_c[l]), h, w_out_b, l, tm=512)
        kv = norm_matmul(mem2, row(norm_mem[l]), w_kv_b, l, tm=512, tn=w_kv_b.shape[2], out_dtype=BF16)
        h = xattn(h.reshape(B, S, D), row(norm_xattn[l]), wq_b, kv.reshape(B, Mm, -1), wo_b, l,
                  tq=512).reshape(B * S, D)
        h = ffn(h, row(norm_ffn[l]), wg_b, wu_b, wd_b, row(norm_final), l, tm=1024, tf=512,
                final_norm=(l == depth - 1))
    return h.reshape(B, S, D)
```

```python
import functools

import numpy as np
import jax
import jax.numpy as jnp
from jax import lax
from jax.experimental import pallas as pl
from jax.experimental.pallas import tpu as pltpu

F32 = jnp.float32
BF16 = jnp.bfloat16
NEG_INF = float("-inf")
POS_INF = float("inf")
LOG2E = 1.4426950408889634

EPS = 1e-5
HEAD_DIM = 64
LANES = 128
BF16_ROWS = 16
A_HEADS, A_KV_HEADS, A_WINDOW, A_BLOCK = 8, 2, 128, 128
B_HEADS, MOBA_BLOCK, MOBA_TOPK = 8, 256, 3
C_HEADS = 16
C_PATTERNS = ((128, 1), (512, 4), (2048, 16))
C_STEPS = 128
C_GROUP = 16
X_HEADS, X_HEAD_DIM = 4, 128
VMEM_LIMIT = 56 * 1024 * 1024
VT_ROWS = HEAD_DIM + BF16_ROWS


def _alibi_slopes(n):
    return jnp.asarray(2.0 ** (-8.0 * np.arange(1, n + 1) / n), F32)


def _params(*sem):
    return pltpu.CompilerParams(dimension_semantics=sem, vmem_limit_bytes=VMEM_LIMIT)


def _rms(x, g):
    return x * lax.rsqrt(jnp.mean(x * x, axis=-1, keepdims=True) + EPS) * g


def _dot_nt(a, b, precision=None):
    return lax.dot_general(a, b, (((1,), (1,)), ((), ())), precision=precision,
                           preferred_element_type=F32)


def _smem_spec():
    return pl.BlockSpec(memory_space=pltpu.SMEM)


def _zero_after(x):
    bits = pltpu.bitcast(x, jnp.uint32)
    return pltpu.bitcast(lax.shift_right_logical(bits, jnp.uint32(32)), F32)


def _norm_matmul_kernel(x_ref, g_ref, w_ref, o_ref, xn_ref):
    @pl.when(pl.program_id(1) == 0)
    def _():
        xn_ref[...] = _rms(x_ref[...], g_ref[...]).astype(BF16)

    o_ref[...] = jnp.dot(xn_ref[...], w_ref[...], preferred_element_type=F32).astype(o_ref.dtype)


def norm_matmul(x, g, w, layer, *, tm, tn, out_dtype):
    M, D = x.shape
    N = w.shape[2]
    assert M % tm == 0 and N % tn == 0
    return pl.pallas_call(
        _norm_matmul_kernel,
        grid=(M // tm, N // tn),
        in_specs=[pl.BlockSpec((tm, D), lambda i, j: (i, 0)),
                  pl.BlockSpec((1, D), lambda i, j: (0, 0)),
                  pl.BlockSpec((None, D, tn), lambda i, j: (layer, 0, j))],
        out_specs=pl.BlockSpec((tm, tn), lambda i, j: (i, j)),
        out_shape=jax.ShapeDtypeStruct((M, N), out_dtype),
        scratch_shapes=[pltpu.VMEM((tm, D), BF16)],
        compiler_params=_params("parallel", "arbitrary"),
        name="norm_matmul",
    )(x, g, w)


def _attn_a_kernel(slopes_ref, sinks_ref, q_ref, k_ref, v_ref, o_ref, kb_ref, vt_ref, bias_ref,
                   *, seq, n_sub, group):
    n = A_BLOCK
    qi = pl.program_id(1)
    heads_per_kv = A_HEADS // A_KV_HEADS
    lo_half = lax.broadcasted_iota(jnp.int32, (n, LANES), 1) < HEAD_DIM
    ones = jnp.ones((BF16_ROWS, 2 * n), BF16)

    @pl.when(qi == 0)
    def _():
        kk = lax.broadcasted_iota(jnp.int32, (2 * n, n), 0)
        qq = lax.broadcasted_iota(jnp.int32, (2 * n, n), 1)
        dist = qq + n - kk
        valid = (dist >= 0) & (dist < A_WINDOW)
        distf = dist.astype(F32)
        for j in range(A_HEADS):
            full = jnp.where(valid, (-slopes_ref[j] * LOG2E) * distf, NEG_INF)
            bias_ref[j, 1] = full
            bias_ref[j, 0] = jnp.where(kk >= n, full, NEG_INF)

        def prep(u, carry):
            for sub in range(4):
                rows = pl.ds(pl.multiple_of((4 * u + sub) * n, n), n)
                kb_ref[4 * u + sub] = k_ref[rows, :].astype(BF16)
                vt_ref[4 * u + sub] = v_ref[rows, :].T.astype(BF16)
            return carry

        lax.fori_loop(0, seq // (4 * n), prep, 0)

    def blocks(u, carry):
        subs = [u * group + g for g in range(group)]
        ts, prevs, variants, scores = [], [], [], []
        for sb in subs:
            t = qi * n_sub + sb
            ts.append(t)
            prevs.append(jnp.maximum(t - 1, 0))
            variants.append(jnp.where(t > 0, 1, 0))
            rows = pl.ds(pl.multiple_of(sb * n, n), n)
            stack = []
            for j in range(A_HEADS):
                qj = q_ref[rows, (j // 2) * LANES:(j // 2 + 1) * LANES] * (HEAD_DIM ** -0.5 * LOG2E)
                kv = j // heads_per_kv
                if j % 2 != kv:
                    qj = pltpu.roll(qj, HEAD_DIM, axis=1)
                stack.append(jnp.where(lo_half if kv == 0 else ~lo_half, qj, 0.0))
            q_all = jnp.concatenate(stack, axis=0).astype(BF16)
            kcat = jnp.concatenate([kb_ref[prevs[-1]], kb_ref[t]], axis=0)
            scores.append(_dot_nt(kcat, q_all))
        anchor = sum(_zero_after(sc[0:1, :n]) for sc in scores[1:]) if group > 1 else None
        for g, sb in enumerate(subs):
            rows = pl.ds(pl.multiple_of(sb * n, n), n)
            vtcat = jnp.concatenate([vt_ref[prevs[g]], vt_ref[ts[g]]], axis=1)
            outs = []
            for kv in range(A_KV_HEADS):
                probs, maxes = [], []
                for j in range(kv * heads_per_kv, (kv + 1) * heads_per_kv):
                    s = scores[g][:, j * n:(j + 1) * n] + bias_ref[j, variants[g]]
                    m = jnp.maximum(jnp.max(s, axis=0, keepdims=True), sinks_ref[j] * LOG2E)
                    if g == 0 and j == 0 and anchor is not None:
                        m = m + anchor
                    probs.append(jnp.exp2(s - m).astype(BF16))
                    maxes.append(m)
                vaug = jnp.concatenate([vtcat[kv * HEAD_DIM:(kv + 1) * HEAD_DIM], ones], axis=0)
                acc = jnp.dot(vaug, jnp.concatenate(probs, axis=1), preferred_element_type=F32)
                for c, j in enumerate(range(kv * heads_per_kv, (kv + 1) * heads_per_kv)):
                    cols = slice(c * n, (c + 1) * n)
                    denom = acc[HEAD_DIM:HEAD_DIM + 1, cols] + jnp.exp2(sinks_ref[j] * LOG2E - maxes[c])
                    outs.append(acc[:HEAD_DIM, cols] * (1.0 / denom))
            for pair in range(A_HEADS // 2):
                o_ref[rows, pair * LANES:(pair + 1) * LANES] = jnp.concatenate(outs[2 * pair:2 * pair + 2], axis=0).T
        return carry

    lax.fori_loop(0, n_sub // group, blocks, 0)


def attn_a(proj, sinks, *, q_col, k_col, v_col, tq, group):
    B, S, _ = proj.shape
    n = A_BLOCK
    qw = A_HEADS * HEAD_DIM
    assert S % tq == 0 and tq % (n * group) == 0 and S % (4 * n) == 0
    assert q_col % qw == 0 and k_col % LANES == 0 and v_col % LANES == 0
    assert A_WINDOW == n and (A_HEADS // A_KV_HEADS) % 2 == 0
    kern = functools.partial(_attn_a_kernel, seq=S, n_sub=tq // n, group=group)
    return pl.pallas_call(
        kern,
        grid=(B, S // tq),
        in_specs=[_smem_spec(), _smem_spec(),
                  pl.BlockSpec((None, tq, qw), lambda b, i: (b, i, q_col // qw)),
                  pl.BlockSpec((None, S, LANES), lambda b, i: (b, 0, k_col // LANES)),
                  pl.BlockSpec((None, S, LANES), lambda b, i: (b, 0, v_col // LANES))],
        out_specs=pl.BlockSpec((None, tq, qw), lambda b, i: (b, i, 0)),
        out_shape=jax.ShapeDtypeStruct((B, S, qw), F32),
        scratch_shapes=[pltpu.VMEM((S // n, n, LANES), BF16),
                        pltpu.VMEM((S // n, LANES, n), BF16),
                        pltpu.VMEM((A_HEADS, 2, 2 * n, n), F32)],
        compiler_params=_params("parallel", "arbitrary"),
        name="attn_swa",
    )(_alibi_slopes(A_HEADS), sinks, proj, proj, proj)


def _attn_b_kernel(slopes_ref, q_ref, k_ref, v_ref, o_ref,
                   kb_ref, vt_ref, kmean_ref, kcat_ref, bias_ref, qs_ref, sel_ref, acc_ref, m_ref, s_ref,
                   *, nb):
    mb = MOBA_BLOCK
    hp = pl.program_id(1)
    slope2 = [slopes_ref[2 * hp + h] * LOG2E for h in range(2)]

    def prepare():
        kpos = lax.broadcasted_iota(jnp.int32, (mb, mb), 0)
        qpos = lax.broadcasted_iota(jnp.int32, (mb, mb), 1)
        relf = (qpos - kpos).astype(F32)
        for h in range(2):
            alibi = (-slope2[h]) * relf
            bias_ref[h, 0] = alibi
            bias_ref[h, 1] = jnp.where(qpos >= kpos, alibi, NEG_INF)
        ones = jnp.ones((BF16_ROWS, mb), BF16)

        def prep(n, carry):
            rows = pl.ds(pl.multiple_of(n * mb, mb), mb)
            kblk = k_ref[rows, :]
            kb_ref[n] = kblk.astype(BF16)
            kmean_ref[pl.ds(n, 1), :] = jnp.mean(kblk, axis=0, keepdims=True)
            vt = v_ref[rows, :].T
            for h in range(2):
                vt_ref[h, n, :HEAD_DIM, :] = vt[h * HEAD_DIM:(h + 1) * HEAD_DIM].astype(BF16)
                vt_ref[h, n, HEAD_DIM:, :] = ones
            return carry

        lax.fori_loop(0, nb, prep, 0)
        km = kmean_ref[...]
        lo_lanes = lax.broadcasted_iota(jnp.int32, (nb, LANES), 1) < HEAD_DIM
        for h in range(2):
            kmh = jnp.where(lo_lanes if h == 0 else ~lo_lanes, km, 0.0)
            hi = kmh.astype(BF16)
            lo = (kmh - hi.astype(F32)).astype(BF16)
            kcat_ref[h * nb:(h + 1) * nb, :] = jnp.concatenate([hi, hi, lo], axis=1)

    prepare()
    lo_half = lax.broadcasted_iota(jnp.int32, (mb, LANES), 1) < HEAD_DIM

    def scaled_queries(i):
        qc = q_ref[pl.ds(pl.multiple_of(i * mb, mb), mb), :] * (HEAD_DIM ** -0.5 * LOG2E)
        return jnp.concatenate([jnp.where(lo_half, qc, 0.0), jnp.where(lo_half, 0.0, qc)], axis=0).astype(BF16)

    q_first = scaled_queries(0)
    qs_ref[0] = q_first
    s_ref[...] = _dot_nt(jnp.concatenate([kb_ref[0], kb_ref[0]], axis=0), q_first)

    def query_block(i, carry0):
        qrows = pl.ds(pl.multiple_of(i * mb, mb), mb)
        q = q_ref[qrows, :]
        slot = i % 2
        i_next = jnp.minimum(i + 1, nb - 1)
        qs_ref[1 - slot] = scaled_queries(i_next)
        blk = lax.broadcasted_iota(jnp.int32, (nb, mb), 0)
        past = blk < i
        q_hi = q.astype(BF16)
        q_lo = (q - q_hi.astype(F32)).astype(BF16)
        gate = _dot_nt(kcat_ref[...], jnp.concatenate([q_hi, q_lo, q_hi], axis=1))

        for h in range(2):
            g = jnp.where(past, gate[h * nb:(h + 1) * nb], NEG_INF)
            avail = blk >= 0
            sel = jnp.where(blk == i, 1.0, 0.0)
            for _ in range(MOBA_TOPK):
                mx = jnp.max(jnp.where(avail, g, NEG_INF), axis=0, keepdims=True)
                is_max = avail & (g == mx)
                first = jnp.min(jnp.where(is_max, blk, nb), axis=0, keepdims=True)
                pick = blk == first
                sel = jnp.where(pick & past, 1.0, sel)
                avail = avail & ~pick
            sel_ref[h, :nb] = sel
            sel_ref[h, nb:] = jnp.zeros((8, mb), F32)

        m_ref[...] = jnp.full(m_ref.shape, NEG_INF, F32)
        acc_ref[...] = jnp.zeros(acc_ref.shape, F32)
        nsteps = (i + 2) // 2

        def step(p, carry):
            first = p == 0
            j0 = jnp.where(first, i, 2 * p - 1)
            j1 = jnp.minimum(2 * p, nb - 1)
            row1 = jnp.where(2 * p < i, j1, nb)
            variant0 = jnp.where(first, 1, 0)
            last = p == nsteps - 1
            n0 = jnp.where(last, i_next, jnp.minimum(2 * p + 1, nb - 1))
            n1 = jnp.where(last, 0, jnp.minimum(2 * p + 2, nb - 1))
            q_slot = jnp.where(last, 1 - slot, slot)
            s_next = _dot_nt(jnp.concatenate([kb_ref[n0], kb_ref[n1]], axis=0), qs_ref[q_slot])
            dist_a = ((i - j0) * mb).astype(F32)
            dist_b = ((i - j1) * mb).astype(F32)
            for h in range(2):
                cols = slice(h * mb, (h + 1) * mb)
                shift_a = -slope2[h] * dist_a
                shift_b = -slope2[h] * dist_b
                sa = s_ref[:mb, cols] + bias_ref[h, variant0]
                sb = s_ref[mb:, cols] + bias_ref[h, 0]
                chosen_a = sel_ref[h, pl.ds(j0, 1), :] > 0.5
                chosen_b = sel_ref[h, pl.ds(row1, 1), :] > 0.5
                m_a = jnp.where(chosen_a, jnp.max(sa, axis=0, keepdims=True) + shift_a, NEG_INF)
                m_b = jnp.where(chosen_b, jnp.max(sb, axis=0, keepdims=True) + shift_b, NEG_INF)
                m_old = m_ref[h]
                m_new = jnp.maximum(m_old, jnp.maximum(m_a, m_b))
                alpha = jnp.exp2(m_old - m_new)
                pa = jnp.exp2(sa - jnp.where(chosen_a, m_new - shift_a, POS_INF)).astype(BF16)
                pb = jnp.exp2(sb - jnp.where(chosen_b, m_new - shift_b, POS_INF)).astype(BF16)
                vt = jnp.concatenate([vt_ref[h, j0], vt_ref[h, j1]], axis=1)
                acc_ref[h] = alpha * acc_ref[h] + jnp.dot(vt, jnp.concatenate([pa, pb], axis=0),
                                                          preferred_element_type=F32)
                m_ref[h] = m_new
            s_ref[...] = s_next
            return carry

        lax.fori_loop(0, nsteps, step, 0)
        outs = []
        for h in range(2):
            acc = acc_ref[h]
            outs.append(acc[:HEAD_DIM] * (1.0 / acc[HEAD_DIM:HEAD_DIM + 1]))
        o_ref[qrows, :] = jnp.concatenate(outs, axis=0).T
        return carry0

    lax.fori_loop(0, nb, query_block, 0)


def attn_b(proj, *, q_col, k_col, v_col):
    B, S, _ = proj.shape
    mb = MOBA_BLOCK
    assert S % mb == 0
    nb = S // mb
    assert MOBA_TOPK < nb and nb % 8 == 0
    pairs = B_HEADS // 2
    kern = functools.partial(_attn_b_kernel, nb=nb)
    return pl.pallas_call(
        kern,
        grid=(B, pairs),
        in_specs=[_smem_spec(),
                  pl.BlockSpec((None, S, LANES), lambda b, p: (b, 0, q_col // LANES + p)),
                  pl.BlockSpec((None, S, LANES), lambda b, p: (b, 0, k_col // LANES + p)),
                  pl.BlockSpec((None, S, LANES), lambda b, p: (b, 0, v_col // LANES + p))],
        out_specs=pl.BlockSpec((None, S, LANES), lambda b, p: (b, 0, p)),
        out_shape=jax.ShapeDtypeStruct((B, S, B_HEADS * HEAD_DIM), F32),
        scratch_shapes=[pltpu.VMEM((nb, mb, LANES), BF16),
                        pltpu.VMEM((2, nb, VT_ROWS, mb), BF16),
                        pltpu.VMEM((nb, LANES), F32),
                        pltpu.VMEM((2 * nb, 3 * LANES), BF16),
                        pltpu.VMEM((2, 2, mb, mb), F32),
                        pltpu.VMEM((2, 2 * mb, LANES), BF16),
                        pltpu.VMEM((2, nb + 8, mb), F32),
                        pltpu.VMEM((2, VT_ROWS, mb), F32),
                        pltpu.VMEM((2, 1, mb), F32),
                        pltpu.VMEM((2 * mb, 2 * mb), F32)],
        compiler_params=_params("parallel", "parallel"),
        name="attn_moba",
    )(_alibi_slopes(B_HEADS), proj, proj, proj)


def _attn_c_kernel(slopes_ref, q_ref, k_ref, v_ref, o_ref,
                   q4_ref, k4_ref, v4_ref, kb_ref, vt_ref, bias_ref, orun_ref, lse_ref, *, seq, group):
    n = C_STEPS
    hp = pl.program_id(1)
    ntile = seq // n
    quarter = seq // 4
    slab = n // 4
    lo_half = lax.broadcasted_iota(jnp.int32, (n, LANES), 1) < HEAD_DIM
    ones = jnp.ones((BF16_ROWS, 2 * n), BF16)
    kk = lax.broadcasted_iota(jnp.int32, (2 * n, n), 0)
    qq = lax.broadcasted_iota(jnp.int32, (2 * n, n), 1)

    def reorder(c, carry):
        r4 = c // (ntile // 4)
        src = pl.ds(r4 + 4 * n * (c % (ntile // 4)), n, stride=4)
        dst = pl.ds(pl.multiple_of(c * n, n), n)
        q4_ref[dst, :] = q_ref[src, :]
        k4_ref[dst, :] = k_ref[src, :]
        v4_ref[dst, :] = v_ref[src, :]
        return carry

    lax.fori_loop(0, ntile, reorder, 0)

    for bi, (w, d) in enumerate(C_PATTERNS):
        nblk = seq // w
        first, last = bi == 0, bi == len(C_PATTERNS) - 1
        assert d in (1, 4, 16)

        def load_tile(ref, t, d=d):
            if d == 4:
                return ref[pl.ds(pl.multiple_of(t * n, n), n), :]
            if d == 16:
                r16, b16 = t // 2, t % 2
                return ref[pl.ds((r16 % 4) * quarter + b16 * (4 * n) + r16 // 4, n, stride=4), :]
            return jnp.concatenate([ref[pl.ds(pl.multiple_of(r4 * quarter + t * slab, slab), slab), :]
                                    for r4 in range(4)], axis=0)

        def store_tile(ref, t, val, d=d):
            if d == 4:
                ref[pl.ds(pl.multiple_of(t * n, n), n), :] = val
            elif d == 16:
                r16, b16 = t // 2, t % 2
                ref[pl.ds((r16 % 4) * quarter + b16 * (4 * n) + r16 // 4, n, stride=4), :] = val
            else:
                for r4 in range(4):
                    ref[pl.ds(pl.multiple_of(r4 * quarter + t * slab, slab), slab), :] = val[r4 * slab:(r4 + 1) * slab]

        if d == 1:
            member = lambda x: 4 * (x % slab) + x // slab
            steps = member(qq) - member(kk % n) + jnp.where(kk < n, n, 0)
        else:
            steps = qq + n - kk
        valid = (steps >= 0) & (steps <= n)
        stepf = steps.astype(F32)
        for h in range(2):
            full = jnp.where(valid, (-slopes_ref[2 * hp + h] * (LOG2E * d)) * stepf, NEG_INF)
            bias_ref[h, 1] = full
            bias_ref[h, 0] = jnp.where(kk >= n, full, NEG_INF)

        def prep(u, carry, load_tile=load_tile):
            for sub in range(group):
                t = u * group + sub
                kb_ref[t] = load_tile(k4_ref, t).astype(BF16)
                vt_ref[t] = load_tile(v4_ref, t).T.astype(BF16)
            return carry

        lax.fori_loop(0, ntile // group, prep, 0)

        def tiles(u, carry, load_tile=load_tile, store_tile=store_tile, nblk=nblk, first=first):
            ts = [u * group + sub for sub in range(group)]
            prevs, variants, scores = [], [], []
            for t in ts:
                has_prev = (t % nblk) > 0
                tp = jnp.where(has_prev, t - 1, t)
                prevs.append(tp)
                variants.append(jnp.where(has_prev, 1, 0))
                q = load_tile(q4_ref, t) * (HEAD_DIM ** -0.5 * LOG2E)
                q_both = jnp.concatenate([jnp.where(lo_half, q, 0.0), jnp.where(lo_half, 0.0, q)], axis=0)
                kcat = jnp.concatenate([kb_ref[tp], kb_ref[t]], axis=0)
                scores.append(_dot_nt(kcat, q_both.astype(BF16)))
            anchor = sum(_zero_after(sc[0:1, :n]) for sc in scores[1:])
            results = []
            for g, t in enumerate(ts):
                vtcat = jnp.concatenate([vt_ref[prevs[g]], vt_ref[t]], axis=1)
                o_rows, lse_rows = [], []
                for h in range(2):
                    s = scores[g][:, h * n:(h + 1) * n] + bias_ref[h, variants[g]]
                    m = jnp.max(s, axis=0, keepdims=True)
                    if g == 0:
                        m = m + anchor
                    p = jnp.exp2(s - m).astype(BF16)
                    vaug = jnp.concatenate([vtcat[h * HEAD_DIM:(h + 1) * HEAD_DIM], ones], axis=0)
                    acc = jnp.dot(vaug, p, preferred_element_type=F32)
                    l = acc[HEAD_DIM:HEAD_DIM + 1]
                    o_rows.append(acc[:HEAD_DIM] * (1.0 / l))
                    lse_rows.append(jnp.broadcast_to(m + jnp.log2(l), (HEAD_DIM, n)))
                results.append((jnp.concatenate(o_rows, axis=0), jnp.concatenate(lse_rows, axis=0)))
            for g, t in enumerate(ts):
                o_t = results[g][0].T
                lse_t = results[g][1].T
                if first:
                    store_tile(orun_ref, t, o_t)
                    store_tile(lse_ref, t, lse_t)
                else:
                    lse_old = load_tile(lse_ref, t)
                    top = jnp.maximum(lse_old, lse_t)
                    w_old = jnp.exp2(lse_old - top)
                    w_t = jnp.exp2(lse_t - top)
                    den = w_old + w_t
                    store_tile(orun_ref, t, (w_old * load_tile(orun_ref, t) + w_t * o_t) * (1.0 / den))
                    if not last:
                        store_tile(lse_ref, t, top + jnp.log2(den))
            return carry

        lax.fori_loop(0, ntile // group, tiles, 0)

    def restore(c, carry):
        r4 = c // (ntile // 4)
        o_ref[pl.ds(r4 + 4 * n * (c % (ntile // 4)), n, stride=4), :] = orun_ref[pl.ds(pl.multiple_of(c * n, n), n), :]
        return carry

    lax.fori_loop(0, ntile, restore, 0)


def attn_c(proj, *, q_col, k_col, v_col):
    B, S, _ = proj.shape
    n = C_STEPS
    assert all(S % w == 0 and w // d == n for w, d in C_PATTERNS)
    assert (S // n) % C_GROUP == 0 and S % (16 * n) == 0
    pairs = C_HEADS // 2
    kern = functools.partial(_attn_c_kernel, seq=S, group=C_GROUP)
    return pl.pallas_call(
        kern,
        grid=(B, pairs),
        in_specs=[_smem_spec(),
                  pl.BlockSpec((None, S, LANES), lambda b, p: (b, 0, q_col // LANES + p)),
                  pl.BlockSpec((None, S, LANES), lambda b, p: (b, 0, k_col // LANES + p)),
                  pl.BlockSpec((None, S, LANES), lambda b, p: (b, 0, v_col // LANES + p))],
        out_specs=pl.BlockSpec((None, S, LANES), lambda b, p: (b, 0, p)),
        out_shape=jax.ShapeDtypeStruct((B, S, C_HEADS * HEAD_DIM), F32),
        scratch_shapes=[pltpu.VMEM((S, LANES), F32),
                        pltpu.VMEM((S, LANES), F32),
                        pltpu.VMEM((S, LANES), F32),
                        pltpu.VMEM((S // n, n, LANES), BF16),
                        pltpu.VMEM((S // n, LANES, n), BF16),
                        pltpu.VMEM((2, 2, 2 * n, n), F32),
                        pltpu.VMEM((S, LANES), F32),
                        pltpu.VMEM((S, LANES), F32)],
        compiler_params=_params("parallel", "parallel"),
        name="attn_dilated",
    )(_alibi_slopes(C_HEADS), proj, proj, proj)


def _mix_out_kernel(oa_ref, ob_ref, oc_ref, ga_ref, gb_ref, gc_ref, h_ref, w_ref, o_ref):
    y = jnp.concatenate([_rms(oa_ref[...], ga_ref[...]).astype(BF16),
                         _rms(ob_ref[...], gb_ref[...]).astype(BF16),
                         _rms(oc_ref[...], gc_ref[...]).astype(BF16)], axis=-1)
    o_ref[...] = h_ref[...] + jnp.dot(y, w_ref[...], preferred_element_type=F32)


def mix_out(oa, ob, oc, ga, gb, gc, h, w, layer, *, tm):
    M, D = h.shape
    wa, wb, wc = oa.shape[1], ob.shape[1], oc.shape[1]
    K = wa + wb + wc
    assert w.shape[1:] == (K, D) and M % tm == 0
    row = lambda width: pl.BlockSpec((tm, width), lambda i: (i, 0))
    gain = lambda width: pl.BlockSpec((1, width), lambda i: (0, 0))
    return pl.pallas_call(
        _mix_out_kernel,
        grid=(M // tm,),
        in_specs=[row(wa), row(wb), row(wc), gain(wa), gain(wb), gain(wc), row(D),
                  pl.BlockSpec((None, K, D), lambda i: (layer, 0, 0))],
        out_specs=row(D),
        out_shape=jax.ShapeDtypeStruct((M, D), F32),
        compiler_params=_params("parallel"),
        name="mix_out",
    )(oa, ob, oc, ga, gb, gc, h, w)


def _xattn_kernel(h_ref, g_ref, wq_ref, kv_ref, wo_ref, o_ref):
    xw = X_HEADS * X_HEAD_DIM
    h = h_ref[...]
    xn = _rms(h, g_ref[...]).astype(BF16)
    q = jnp.dot(xn, wq_ref[...], preferred_element_type=F32) * (X_HEAD_DIM ** -0.5)
    heads = []
    for hd in range(X_HEADS):
        lanes = slice(hd * X_HEAD_DIM, (hd + 1) * X_HEAD_DIM)
        s = _dot_nt(q[:, lanes].astype(BF16), kv_ref[:, lanes])
        m = jnp.max(s, axis=-1, keepdims=True)
        p = jnp.exp(s - m)
        l = jnp.sum(p, axis=-1, keepdims=True)
        v = kv_ref[:, xw + hd * X_HEAD_DIM:xw + (hd + 1) * X_HEAD_DIM]
        heads.append(jnp.dot(p.astype(BF16), v, preferred_element_type=F32) / l)
    o = jnp.concatenate(heads, axis=-1).astype(BF16)
    o_ref[...] = h + jnp.dot(o, wo_ref[...], preferred_element_type=F32)


def xattn(h, g, wq, kv, wo, layer, *, tq):
    B, S, D = h.shape
    Mm = kv.shape[1]
    xw = X_HEADS * X_HEAD_DIM
    assert S % tq == 0
    return pl.pallas_call(
        _xattn_kernel,
        grid=(B, S // tq),
        in_specs=[pl.BlockSpec((None, tq, D), lambda b, i: (b, i, 0)),
                  pl.BlockSpec((1, D), lambda b, i: (0, 0)),
                  pl.BlockSpec((None, D, xw), lambda b, i: (layer, 0, 0)),
                  pl.BlockSpec((None, Mm, 2 * xw), lambda b, i: (b, 0, 0)),
                  pl.BlockSpec((None, xw, D), lambda b, i: (layer, 0, 0))],
        out_specs=pl.BlockSpec((None, tq, D), lambda b, i: (b, i, 0)),
        out_shape=jax.ShapeDtypeStruct((B, S, D), F32),
        compiler_params=_params("parallel", "parallel"),
        name="xattn",
    )(h, g, wq, kv, wo)


def _ffn_kernel(h_ref, g_ref, wg_ref, wu_ref, wd_ref, gf_ref, o_ref, xn_ref, *, final_norm):
    f = pl.program_id(1)

    @pl.when(f == 0)
    def _():
        h = h_ref[...]
        xn_ref[...] = _rms(h, g_ref[...]).astype(BF16)
        o_ref[...] = h

    xn = xn_ref[...]
    gate = jnp.dot(xn, wg_ref[...], preferred_element_type=F32)
    up = jnp.dot(xn, wu_ref[...], preferred_element_type=F32)
    act = (gate / (1.0 + jnp.exp(-gate))) * up
    o_ref[...] += jnp.dot(act.astype(BF16), wd_ref[...], preferred_element_type=F32)

    if final_norm:
        @pl.when(f == pl.num_programs(1) - 1)
        def _():
            o_ref[...] = _rms(o_ref[...], gf_ref[...])


def ffn(h, g, wg, wu, wd, gf, layer, *, tm, tf, final_norm):
    M, D = h.shape
    Fd = wg.shape[2]
    assert M % tm == 0 and Fd % tf == 0
    kern = functools.partial(_ffn_kernel, final_norm=final_norm)
    return pl.pallas_call(
        kern,
        grid=(M // tm, Fd // tf),
        in_specs=[pl.BlockSpec((tm, D), lambda i, f: (i, 0)),
                  pl.BlockSpec((1, D), lambda i, f: (0, 0)),
                  pl.BlockSpec((None, D, tf), lambda i, f: (layer, 0, f)),
                  pl.BlockSpec((None, D, tf), lambda i, f: (layer, 0, f)),
                  pl.BlockSpec((None, tf, D), lambda i, f: (layer, f, 0)),
                  pl.BlockSpec((1, D), lambda i, f: (0, 0))],
        out_specs=pl.BlockSpec((tm, D), lambda i, f: (i, 0)),
        out_shape=jax.ShapeDtypeStruct((M, D), F32),
        scratch_shapes=[pltpu.VMEM((tm, D), BF16)],
        compiler_params=_params("parallel", "arbitrary"),
        name="ffn",
    )(h, g, wg, wu, wd, gf)


def kernel(x, mem, norm_mix, w_in, sinks, gain_a, gain_b, gain_c, w_out, norm_xattn, norm_mem,
           wq_x, wk_x, wv_x, wo_x, norm_ffn, w_gate, w_up, w_down, norm_final):
    B, S, D = x.shape
    depth = w_in.shape[0]
    Mm = mem.shape[1]
    qa_w, kva_w = A_HEADS * HEAD_DIM, A_KV_HEADS * HEAD_DIM
    b_w, c_w = B_HEADS * HEAD_DIM, C_HEADS * HEAD_DIM
    widths = (qa_w, kva_w, kva_w, b_w, b_w, b_w, c_w, c_w, c_w)
    cols = [int(c) for c in np.concatenate([[0], np.cumsum(widths)[:-1]])]
    in_w = sum(widths)

    row = lambda v: v.reshape(1, -1)
    h = x.reshape(B * S, D)
    mem2 = mem.reshape(B * Mm, D)
    w_in_b, w_out_b = w_in.astype(BF16), w_out.astype(BF16)
    w_kv_b = jnp.concatenate([wk_x, wv_x], axis=2).astype(BF16)
    wq_b, wo_b = wq_x.astype(BF16), wo_x.astype(BF16)
    wg_b, wu_b, wd_b = w_gate.astype(BF16), w_up.astype(BF16), w_down.astype(BF16)
    for l in range(depth):
        proj = norm_matmul(h, row(norm_mix[l]), w_in_b, l, tm=1024, tn=1792, out_dtype=F32)
        proj = proj.reshape(B, S, in_w)
        oa = attn_a(proj, sinks[l], q_col=cols[0], k_col=cols[1], v_col=cols[2], tq=1024, group=4)
        ob = attn_b(proj, q_col=cols[3], k_col=cols[4], v_col=cols[5])
        oc = attn_c(proj, q_col=cols[6], k_col=cols[7], v_col=cols[8])
        h = mix_out(oa.reshape(B * S, qa_w), ob.reshape(B * S, b_w), oc.reshape(B * S, c_w),
                    row(gain_a[l]), row(gain_b[l]), row(gain_c[l]), h, w_out_b, l, tm=512)
        kv = norm_matmul(mem2, row(norm_mem[l]), w_kv_b, l, tm=512, tn=w_kv_b.shape[2], out_dtype=BF16)
        h = xattn(h.reshape(B, S, D), row(norm_xattn[l]), wq_b, kv.reshape(B, Mm, -1), wo_b, l,
                  tq=512).reshape(B * S, D)
        h = ffn(h, row(norm_ffn[l]), wg_b, wu_b, wd_b, row(norm_final), l, tm=1024, tf=512,
                final_norm=(l == depth - 1))
    return h.reshape(B, S, D)
```

```python
import functools

import numpy as np
import jax
import jax.numpy as jnp
from jax import lax
from jax.experimental import pallas as pl
from jax.experimental.pallas import tpu as pltpu

F32 = jnp.float32
BF16 = jnp.bfloat16
NEG_INF = float("-inf")
POS_INF = float("inf")
LOG2E = 1.4426950408889634

EPS = 1e-5
HEAD_DIM = 64
LANES = 128
BF16_ROWS = 16
A_HEADS, A_KV_HEADS, A_WINDOW, A_BLOCK = 8, 2, 128, 128
B_HEADS, MOBA_BLOCK, MOBA_TOPK = 8, 256, 3
C_HEADS = 16
C_PATTERNS = ((128, 1), (512, 4), (2048, 16))
C_STEPS = 128
C_GROUP = 16
X_HEADS, X_HEAD_DIM = 4, 128
VMEM_LIMIT = 56 * 1024 * 1024
VT_ROWS = HEAD_DIM + BF16_ROWS


def _alibi_slopes(n):
    return jnp.asarray(2.0 ** (-8.0 * np.arange(1, n + 1) / n), F32)


def _params(*sem):
    return pltpu.CompilerParams(dimension_semantics=sem, vmem_limit_bytes=VMEM_LIMIT)


def _rms(x, g):
    return x * lax.rsqrt(jnp.mean(x * x, axis=-1, keepdims=True) + EPS) * g


def _dot_nt(a, b, precision=None):
    return lax.dot_general(a, b, (((1,), (1,)), ((), ())), precision=precision,
                           preferred_element_type=F32)


def _smem_spec():
    return pl.BlockSpec(memory_space=pltpu.SMEM)


def _zero_after(x):
    bits = pltpu.bitcast(x, jnp.uint32)
    return pltpu.bitcast(lax.shift_right_logical(bits, jnp.uint32(32)), F32)


def _norm_matmul_kernel(x_ref, g_ref, w_ref, o_ref, xn_ref):
    @pl.when(pl.program_id(1) == 0)
    def _():
        xn_ref[...] = _rms(x_ref[...], g_ref[...]).astype(BF16)

    o_ref[...] = jnp.dot(xn_ref[...], w_ref[...], preferred_element_type=F32).astype(o_ref.dtype)


def norm_matmul(x, g, w, layer, *, tm, tn, out_dtype):
    M, D = x.shape
    N = w.shape[2]
    assert M % tm == 0 and N % tn == 0
    return pl.pallas_call(
        _norm_matmul_kernel,
        grid=(M // tm, N // tn),
        in_specs=[pl.BlockSpec((tm, D), lambda i, j: (i, 0)),
                  pl.BlockSpec((1, D), lambda i, j: (0, 0)),
                  pl.BlockSpec((None, D, tn), lambda i, j: (layer, 0, j))],
        out_specs=pl.BlockSpec((tm, tn), lambda i, j: (i, j)),
        out_shape=jax.ShapeDtypeStruct((M, N), out_dtype),
        scratch_shapes=[pltpu.VMEM((tm, D), BF16)],
        compiler_params=_params("parallel", "arbitrary"),
        name="norm_matmul",
    )(x, g, w)


def _attn_a_kernel(slopes_ref, sinks_ref, q_ref, k_ref, v_ref, o_ref, kb_ref, vt_ref, bias_ref,
                   *, seq, n_sub, group):
    n = A_BLOCK
    qi = pl.program_id(1)
    heads_per_kv = A_HEADS // A_KV_HEADS
    lo_half = lax.broadcasted_iota(jnp.int32, (n, LANES), 1) < HEAD_DIM
    ones = jnp.ones((BF16_ROWS, 2 * n), BF16)

    @pl.when(qi == 0)
    def _():
        kk = lax.broadcasted_iota(jnp.int32, (2 * n, n), 0)
        qq = lax.broadcasted_iota(jnp.int32, (2 * n, n), 1)
        dist = qq + n - kk
        valid = (dist >= 0) & (dist < A_WINDOW)
        distf = dist.astype(F32)
        for j in range(A_HEADS):
            full = jnp.where(valid, (-slopes_ref[j] * LOG2E) * distf, NEG_INF)
            bias_ref[j, 1] = full
            bias_ref[j, 0] = jnp.where(kk >= n, full, NEG_INF)

        def prep(u, carry):
            for sub in range(4):
                rows = pl.ds(pl.multiple_of((4 * u + sub) * n, n), n)
                kb_ref[4 * u + sub] = k_ref[rows, :].astype(BF16)
                vt_ref[4 * u + sub] = v_ref[rows, :].T.astype(BF16)
            return carry

        lax.fori_loop(0, seq // (4 * n), prep, 0)

    def blocks(u, carry):
        subs = [u * group + g for g in range(group)]
        ts, prevs, variants, scores = [], [], [], []
        for sb in subs:
            t = qi * n_sub + sb
            ts.append(t)
            prevs.append(jnp.maximum(t - 1, 0))
            variants.append(jnp.where(t > 0, 1, 0))
            rows = pl.ds(pl.multiple_of(sb * n, n), n)
            stack = []
            for j in range(A_HEADS):
                qj = q_ref[rows, (j // 2) * LANES:(j // 2 + 1) * LANES] * (HEAD_DIM ** -0.5 * LOG2E)
                kv = j // heads_per_kv
                if j % 2 != kv:
                    qj = pltpu.roll(qj, HEAD_DIM, axis=1)
                stack.append(jnp.where(lo_half if kv == 0 else ~lo_half, qj, 0.0))
            q_all = jnp.concatenate(stack, axis=0).astype(BF16)
            kcat = jnp.concatenate([kb_ref[prevs[-1]], kb_ref[t]], axis=0)
            scores.append(_dot_nt(kcat, q_all))
        anchor = sum(_zero_after(sc[0:1, :n]) for sc in scores[1:]) if group > 1 else None
        for g, sb in enumerate(subs):
            rows = pl.ds(pl.multiple_of(sb * n, n), n)
            vtcat = jnp.concatenate([vt_ref[prevs[g]], vt_ref[ts[g]]], axis=1)
            outs = []
            for kv in range(A_KV_HEADS):
                probs, maxes = [], []
                for j in range(kv * heads_per_kv, (kv + 1) * heads_per_kv):
                    s = scores[g][:, j * n:(j + 1) * n] + bias_ref[j, variants[g]]
                    m = jnp.maximum(jnp.max(s, axis=0, keepdims=True), sinks_ref[j] * LOG2E)
                    if g == 0 and j == 0 and anchor is not None:
                        m = m + anchor
                    probs.append(jnp.exp2(s - m).astype(BF16))
                    maxes.append(m)
                vaug = jnp.concatenate([vtcat[kv * HEAD_DIM:(kv + 1) * HEAD_DIM], ones], axis=0)
                acc = jnp.dot(vaug, jnp.concatenate(probs, axis=1), preferred_element_type=F32)
                for c, j in enumerate(range(kv * heads_per_kv, (kv + 1) * heads_per_kv)):
                    cols = slice(c * n, (c + 1) * n)
                    denom = acc[HEAD_DIM:HEAD_DIM + 1, cols] + jnp.exp2(sinks_ref[j] * LOG2E - maxes[c])
                    outs.append(acc[:HEAD_DIM, cols] * (1.0 / denom))
            for pair in range(A_HEADS // 2):
                o_ref[rows, pair * LANES:(pair + 1) * LANES] = jnp.concatenate(outs[2 * pair:2 * pair + 2], axis=0).T
        return carry

    lax.fori_loop(0, n_sub // group, blocks, 0)


def attn_a(proj, sinks, *, q_col, k_col, v_col, tq, group):
    B, S, _ = proj.shape
    n = A_BLOCK
    qw = A_HEADS * HEAD_DIM
    assert S % tq == 0 and tq % (n * group) == 0 and S % (4 * n) == 0
    assert q_col % qw == 0 and k_col % LANES == 0 and v_col % LANES == 0
    assert A_WINDOW == n and (A_HEADS // A_KV_HEADS) % 2 == 0
    kern = functools.partial(_attn_a_kernel, seq=S, n_sub=tq // n, group=group)
    return pl.pallas_call(
        kern,
        grid=(B, S // tq),
        in_specs=[_smem_spec(), _smem_spec(),
                  pl.BlockSpec((None, tq, qw), lambda b, i: (b, i, q_col // qw)),
                  pl.BlockSpec((None, S, LANES), lambda b, i: (b, 0, k_col // LANES)),
                  pl.BlockSpec((None, S, LANES), lambda b, i: (b, 0, v_col // LANES))],
        out_specs=pl.BlockSpec((None, tq, qw), lambda b, i: (b, i, 0)),
        out_shape=jax.ShapeDtypeStruct((B, S, qw), F32),
        scratch_shapes=[pltpu.VMEM((S // n, n, LANES), BF16),
                        pltpu.VMEM((S // n, LANES, n), BF16),
                        pltpu.VMEM((A_HEADS, 2, 2 * n, n), F32)],
        compiler_params=_params("parallel", "arbitrary"),
        name="attn_swa",
    )(_alibi_slopes(A_HEADS), sinks, proj, proj, proj)


def _attn_b_kernel(slopes_ref, q_ref, k_ref, v_ref, o_ref,
                   kb_ref, vt_ref, kmean_ref, kcat_ref, bias_ref, qs_ref, sel_ref, acc_ref, m_ref, s_ref,
                   cmax_ref, *, nb):
    mb = MOBA_BLOCK
    hp = pl.program_id(1)
    slope2 = [slopes_ref[2 * hp + h] * LOG2E for h in range(2)]

    def prepare():
        kpos = lax.broadcasted_iota(jnp.int32, (mb, mb), 0)
        qpos = lax.broadcasted_iota(jnp.int32, (mb, mb), 1)
        relf = (qpos - kpos).astype(F32)
        for h in range(2):
            alibi = (-slope2[h]) * relf
            bias_ref[h, 0] = alibi
            bias_ref[h, 1] = jnp.where(qpos >= kpos, alibi, NEG_INF)
        ones = jnp.ones((BF16_ROWS, mb), BF16)

        def prep(n, carry):
            rows = pl.ds(pl.multiple_of(n * mb, mb), mb)
            kblk = k_ref[rows, :]
            kb_ref[n] = kblk.astype(BF16)
            kmean_ref[pl.ds(n, 1), :] = jnp.mean(kblk, axis=0, keepdims=True)
            vt = v_ref[rows, :].T
            for h in range(2):
                vt_ref[h, n, :HEAD_DIM, :] = vt[h * HEAD_DIM:(h + 1) * HEAD_DIM].astype(BF16)
                vt_ref[h, n, HEAD_DIM:, :] = ones
            return carry

        lax.fori_loop(0, nb, prep, 0)
        km = kmean_ref[...]
        lo_lanes = lax.broadcasted_iota(jnp.int32, (nb, LANES), 1) < HEAD_DIM
        for h in range(2):
            kmh = jnp.where(lo_lanes if h == 0 else ~lo_lanes, km, 0.0)
            hi = kmh.astype(BF16)
            lo = (kmh - hi.astype(F32)).astype(BF16)
            kcat_ref[h * nb:(h + 1) * nb, :] = jnp.concatenate([hi, hi, lo], axis=1)

    prepare()
    lo_half = lax.broadcasted_iota(jnp.int32, (mb, LANES), 1) < HEAD_DIM

    def scaled_queries(i):
        qc = q_ref[pl.ds(pl.multiple_of(i * mb, mb), mb), :] * (HEAD_DIM ** -0.5 * LOG2E)
        return jnp.concatenate([jnp.where(lo_half, qc, 0.0), jnp.where(lo_half, 0.0, qc)], axis=0).astype(BF16)

    def stage_scores(raw, own_first):
        variant_a = jnp.where(own_first, 1, 0)
        for h in range(2):
            cols = slice(h * mb, (h + 1) * mb)
            za = raw[:mb, cols] + bias_ref[h, variant_a]
            zb = raw[mb:, cols] + bias_ref[h, 0]
            s_ref[:mb, cols] = za
            s_ref[mb:, cols] = zb
            cmax_ref[h, 0] = jnp.max(za, axis=0, keepdims=True)
            cmax_ref[h, 1] = jnp.max(zb, axis=0, keepdims=True)

    q_first = scaled_queries(0)
    qs_ref[0] = q_first
    stage_scores(_dot_nt(jnp.concatenate([kb_ref[0], kb_ref[0]], axis=0), q_first), True)

    def query_block(i, carry0):
        qrows = pl.ds(pl.multiple_of(i * mb, mb), mb)
        q = q_ref[qrows, :]
        slot = i % 2
        i_next = jnp.minimum(i + 1, nb - 1)
        qs_ref[1 - slot] = scaled_queries(i_next)
        blk = lax.broadcasted_iota(jnp.int32, (nb, mb), 0)
        past = blk < i
        q_hi = q.astype(BF16)
        q_lo = (q - q_hi.astype(F32)).astype(BF16)
        gate = _dot_nt(kcat_ref[...], jnp.concatenate([q_hi, q_lo, q_hi], axis=1))

        for h in range(2):
            g = jnp.where(past, gate[h * nb:(h + 1) * nb], NEG_INF)
            avail = blk >= 0
            sel = jnp.where(blk == i, 1.0, 0.0)
            for _ in range(MOBA_TOPK):
                mx = jnp.max(jnp.where(avail, g, NEG_INF), axis=0, keepdims=True)
                is_max = avail & (g == mx)
                first = jnp.min(jnp.where(is_max, blk, nb), axis=0, keepdims=True)
                pick = blk == first
                sel = jnp.where(pick & past, 1.0, sel)
                avail = avail & ~pick
            sel_ref[h, :nb] = sel
            sel_ref[h, nb:] = jnp.zeros((8, mb), F32)

        m_ref[...] = jnp.full(m_ref.shape, NEG_INF, F32)
        acc_ref[...] = jnp.zeros(acc_ref.shape, F32)
        nsteps = (i + 2) // 2

        def step(p, carry):
            first = p == 0
            j0 = jnp.where(first, i, 2 * p - 1)
            j1 = jnp.minimum(2 * p, nb - 1)
            row1 = jnp.where(2 * p < i, j1, nb)
            last = p == nsteps - 1
            n0 = jnp.where(last, i_next, jnp.minimum(2 * p + 1, nb - 1))
            n1 = jnp.where(last, 0, jnp.minimum(2 * p + 2, nb - 1))
            q_slot = jnp.where(last, 1 - slot, slot)
            s_next = _dot_nt(jnp.concatenate([kb_ref[n0], kb_ref[n1]], axis=0), qs_ref[q_slot])
            dist_a = ((i - j0) * mb).astype(F32)
            dist_b = ((i - j1) * mb).astype(F32)
            for h in range(2):
                cols = slice(h * mb, (h + 1) * mb)
                shift_a = -slope2[h] * dist_a
                shift_b = -slope2[h] * dist_b
                chosen_a = sel_ref[h, pl.ds(j0, 1), :] > 0.5
                chosen_b = sel_ref[h, pl.ds(row1, 1), :] > 0.5
                m_a = jnp.where(chosen_a, cmax_ref[h, 0] + shift_a, NEG_INF)
                m_b = jnp.where(chosen_b, cmax_ref[h, 1] + shift_b, NEG_INF)
                m_old = m_ref[h]
                m_new = jnp.maximum(m_old, jnp.maximum(m_a, m_b))
                alpha = jnp.exp2(m_old - m_new)
                pa = jnp.exp2(s_ref[:mb, cols] - jnp.where(chosen_a, m_new - shift_a, POS_INF)).astype(BF16)
                pb = jnp.exp2(s_ref[mb:, cols] - jnp.where(chosen_b, m_new - shift_b, POS_INF)).astype(BF16)
                vt = jnp.concatenate([vt_ref[h, j0], vt_ref[h, j1]], axis=1)
                acc_ref[h] = alpha * acc_ref[h] + jnp.dot(vt, jnp.concatenate([pa, pb], axis=0),
                                                          preferred_element_type=F32)
                m_ref[h] = m_new
            stage_scores(s_next, last)
            return carry

        lax.fori_loop(0, nsteps, step, 0)
        outs = []
        for h in range(2):
            acc = acc_ref[h]
            outs.append(acc[:HEAD_DIM] * (1.0 / acc[HEAD_DIM:HEAD_DIM + 1]))
        o_ref[qrows, :] = jnp.concatenate(outs, axis=0).T
        return carry0

    lax.fori_loop(0, nb, query_block, 0)


def attn_b(proj, *, q_col, k_col, v_col):
    B, S, _ = proj.shape
    mb = MOBA_BLOCK
    assert S % mb == 0
    nb = S // mb
    assert MOBA_TOPK < nb and nb % 8 == 0
    pairs = B_HEADS // 2
    kern = functools.partial(_attn_b_kernel, nb=nb)
    return pl.pallas_call(
        kern,
        grid=(B, pairs),
        in_specs=[_smem_spec(),
                  pl.BlockSpec((None, S, LANES), lambda b, p: (b, 0, q_col // LANES + p)),
                  pl.BlockSpec((None, S, LANES), lambda b, p: (b, 0, k_col // LANES + p)),
                  pl.BlockSpec((None, S, LANES), lambda b, p: (b, 0, v_col // LANES + p))],
        out_specs=pl.BlockSpec((None, S, LANES), lambda b, p: (b, 0, p)),
        out_shape=jax.ShapeDtypeStruct((B, S, B_HEADS * HEAD_DIM), F32),
        scratch_shapes=[pltpu.VMEM((nb, mb, LANES), BF16),
                        pltpu.VMEM((2, nb, VT_ROWS, mb), BF16),
                        pltpu.VMEM((nb, LANES), F32),
                        pltpu.VMEM((2 * nb, 3 * LANES), BF16),
                        pltpu.VMEM((2, 2, mb, mb), F32),
                        pltpu.VMEM((2, 2 * mb, LANES), BF16),
                        pltpu.VMEM((2, nb + 8, mb), F32),
                        pltpu.VMEM((2, VT_ROWS, mb), F32),
                        pltpu.VMEM((2, 1, mb), F32),
                        pltpu.VMEM((2 * mb, 2 * mb), F32),
                        pltpu.VMEM((2, 2, 1, mb), F32)],
        compiler_params=_params("parallel", "parallel"),
        name="attn_moba",
    )(_alibi_slopes(B_HEADS), proj, proj, proj)


def _attn_c_kernel(slopes_ref, q_ref, k_ref, v_ref, o_ref,
                   q4_ref, k4_ref, v4_ref, kb_ref, vt_ref, bias_ref, orun_ref, lse_ref, *, seq, group):
    n = C_STEPS
    hp = pl.program_id(1)
    ntile = seq // n
    quarter = seq // 4
    slab = n // 4
    lo_half = lax.broadcasted_iota(jnp.int32, (n, LANES), 1) < HEAD_DIM
    ones = jnp.ones((BF16_ROWS, 2 * n), BF16)
    kk = lax.broadcasted_iota(jnp.int32, (2 * n, n), 0)
    qq = lax.broadcasted_iota(jnp.int32, (2 * n, n), 1)

    def reorder(c, carry):
        r4 = c // (ntile // 4)
        src = pl.ds(r4 + 4 * n * (c % (ntile // 4)), n, stride=4)
        dst = pl.ds(pl.multiple_of(c * n, n), n)
        q4_ref[dst, :] = q_ref[src, :]
        k4_ref[dst, :] = k_ref[src, :]
        v4_ref[dst, :] = v_ref[src, :]
        return carry

    lax.fori_loop(0, ntile, reorder, 0)

    for bi, (w, d) in enumerate(C_PATTERNS):
        nblk = seq // w
        first, last = bi == 0, bi == len(C_PATTERNS) - 1
        assert d in (1, 4, 16)

        def load_tile(ref, t, d=d):
            if d == 4:
                return ref[pl.ds(pl.multiple_of(t * n, n), n), :]
            if d == 16:
                r16, b16 = t // 2, t % 2
                return ref[pl.ds((r16 % 4) * quarter + b16 * (4 * n) + r16 // 4, n, stride=4), :]
            return jnp.concatenate([ref[pl.ds(pl.multiple_of(r4 * quarter + t * slab, slab), slab), :]
                                    for r4 in range(4)], axis=0)

        def store_tile(ref, t, val, d=d):
            if d == 4:
                ref[pl.ds(pl.multiple_of(t * n, n), n), :] = val
            elif d == 16:
                r16, b16 = t // 2, t % 2
                ref[pl.ds((r16 % 4) * quarter + b16 * (4 * n) + r16 // 4, n, stride=4), :] = val
            else:
                for r4 in range(4):
                    ref[pl.ds(pl.multiple_of(r4 * quarter + t * slab, slab), slab), :] = val[r4 * slab:(r4 + 1) * slab]

        if d == 1:
            member = lambda x: 4 * (x % slab) + x // slab
            steps = member(qq) - member(kk % n) + jnp.where(kk < n, n, 0)
        else:
            steps = qq + n - kk
        valid = (steps >= 0) & (steps <= n)
        stepf = steps.astype(F32)
        for h in range(2):
            full = jnp.where(valid, (-slopes_ref[2 * hp + h] * (LOG2E * d)) * stepf, NEG_INF)
            bias_ref[h, 1] = full
            bias_ref[h, 0] = jnp.where(kk >= n, full, NEG_INF)

        def prep(u, carry, load_tile=load_tile):
            for sub in range(group):
                t = u * group + sub
                kb_ref[t] = load_tile(k4_ref, t).astype(BF16)
                vt_ref[t] = load_tile(v4_ref, t).T.astype(BF16)
            return carry

        lax.fori_loop(0, ntile // group, prep, 0)

        def tiles(u, carry, load_tile=load_tile, store_tile=store_tile, nblk=nblk, first=first):
            ts = [u * group + sub for sub in range(group)]
            prevs, variants, scores = [], [], []
            for t in ts:
                has_prev = (t % nblk) > 0
                tp = jnp.where(has_prev, t - 1, t)
                prevs.append(tp)
                variants.append(jnp.where(has_prev, 1, 0))
                q = load_tile(q4_ref, t) * (HEAD_DIM ** -0.5 * LOG2E)
                q_both = jnp.concatenate([jnp.where(lo_half, q, 0.0), jnp.where(lo_half, 0.0, q)], axis=0)
                kcat = jnp.concatenate([kb_ref[tp], kb_ref[t]], axis=0)
                scores.append(_dot_nt(kcat, q_both.astype(BF16)))
            anchor = sum(_zero_after(sc[0:1, :n]) for sc in scores[1:])
            results = []
            for g, t in enumerate(ts):
                vtcat = jnp.concatenate([vt_ref[prevs[g]], vt_ref[t]], axis=1)
                o_rows, lse_rows = [], []
                for h in range(2):
                    s = scores[g][:, h * n:(h + 1) * n] + bias_ref[h, variants[g]]
                    m = jnp.max(s, axis=0, keepdims=True)
                    if g == 0:
                        m = m + anchor
                    p = jnp.exp2(s - m).astype(BF16)
                    vaug = jnp.concatenate([vtcat[h * HEAD_DIM:(h + 1) * HEAD_DIM], ones], axis=0)
                    acc = jnp.dot(vaug, p, preferred_element_type=F32)
                    l = acc[HEAD_DIM:HEAD_DIM + 1]
                    o_rows.append(acc[:HEAD_DIM] * (1.0 / l))
                    lse_rows.append(jnp.broadcast_to(m + jnp.log2(l), (HEAD_DIM, n)))
                results.append((jnp.concatenate(o_rows, axis=0), jnp.concatenate(lse_rows, axis=0)))
            for g, t in enumerate(ts):
                o_t = results[g][0].T
                lse_t = results[g][1].T
                if first:
                    store_tile(orun_ref, t, o_t)
                    store_tile(lse_ref, t, lse_t)
                else:
                    lse_old = load_tile(lse_ref, t)
                    top = jnp.maximum(lse_old, lse_t)
                    w_old = jnp.exp2(lse_old - top)
                    w_t = jnp.exp2(lse_t - top)
                    den = w_old + w_t
                    store_tile(orun_ref, t, (w_old * load_tile(orun_ref, t) + w_t * o_t) * (1.0 / den))
                    if not last:
                        store_tile(lse_ref, t, top + jnp.log2(den))
            return carry

        lax.fori_loop(0, ntile // group, tiles, 0)

    def restore(c, carry):
        r4 = c // (ntile // 4)
        o_ref[pl.ds(r4 + 4 * n * (c % (ntile // 4)), n, stride=4), :] = orun_ref[pl.ds(pl.multiple_of(c * n, n), n), :]
        return carry

    lax.fori_loop(0, ntile, restore, 0)


def attn_c(proj, *, q_col, k_col, v_col):
    B, S, _ = proj.shape
    n = C_STEPS
    assert all(S % w == 0 and w // d == n for w, d in C_PATTERNS)
    assert (S // n) % C_GROUP == 0 and S % (16 * n) == 0
    pairs = C_HEADS // 2
    kern = functools.partial(_attn_c_kernel, seq=S, group=C_GROUP)
    return pl.pallas_call(
        kern,
        grid=(B, pairs),
        in_specs=[_smem_spec(),
                  pl.BlockSpec((None, S, LANES), lambda b, p: (b, 0, q_col // LANES + p)),
                  pl.BlockSpec((None, S, LANES), lambda b, p: (b, 0, k_col // LANES + p)),
                  pl.BlockSpec((None, S, LANES), lambda b, p: (b, 0, v_col // LANES + p))],
        out_specs=pl.BlockSpec((None, S, LANES), lambda b, p: (b, 0, p)),
        out_shape=jax.ShapeDtypeStruct((B, S, C_HEADS * HEAD_DIM), F32),
        scratch_shapes=[pltpu.VMEM((S, LANES), F32),
                        pltpu.VMEM((S, LANES), F32),
                        pltpu.VMEM((S, LANES), F32),
                        pltpu.VMEM((S // n, n, LANES), BF16),
                        pltpu.VMEM((S // n, LANES, n), BF16),
                        pltpu.VMEM((2, 2, 2 * n, n), F32),
                        pltpu.VMEM((S, LANES), F32),
                        pltpu.VMEM((S, LANES), F32)],
        compiler_params=_params("parallel", "parallel"),
        name="attn_dilated",
    )(_alibi_slopes(C_HEADS), proj, proj, proj)


def _mix_out_kernel(oa_ref, ob_ref, oc_ref, ga_ref, gb_ref, gc_ref, h_ref, w_ref, o_ref):
    y = jnp.concatenate([_rms(oa_ref[...], ga_ref[...]).astype(BF16),
                         _rms(ob_ref[...], gb_ref[...]).astype(BF16),
                         _rms(oc_ref[...], gc_ref[...]).astype(BF16)], axis=-1)
    o_ref[...] = h_ref[...] + jnp.dot(y, w_ref[...], preferred_element_type=F32)


def mix_out(oa, ob, oc, ga, gb, gc, h, w, layer, *, tm):
    M, D = h.shape
    wa, wb, wc = oa.shape[1], ob.shape[1], oc.shape[1]
    K = wa + wb + wc
    assert w.shape[1:] == (K, D) and M % tm == 0
    row = lambda width: pl.BlockSpec((tm, width), lambda i: (i, 0))
    gain = lambda width: pl.BlockSpec((1, width), lambda i: (0, 0))
    return pl.pallas_call(
        _mix_out_kernel,
        grid=(M // tm,),
        in_specs=[row(wa), row(wb), row(wc), gain(wa), gain(wb), gain(wc), row(D),
                  pl.BlockSpec((None, K, D), lambda i: (layer, 0, 0))],
        out_specs=row(D),
        out_shape=jax.ShapeDtypeStruct((M, D), F32),
        compiler_params=_params("parallel"),
        name="mix_out",
    )(oa, ob, oc, ga, gb, gc, h, w)


def _xattn_kernel(h_ref, g_ref, wq_ref, kv_ref, wo_ref, o_ref):
    xw = X_HEADS * X_HEAD_DIM
    h = h_ref[...]
    xn = _rms(h, g_ref[...]).astype(BF16)
    q = jnp.dot(xn, wq_ref[...], preferred_element_type=F32) * (X_HEAD_DIM ** -0.5)
    heads = []
    for hd in range(X_HEADS):
        lanes = slice(hd * X_HEAD_DIM, (hd + 1) * X_HEAD_DIM)
        s = _dot_nt(q[:, lanes].astype(BF16), kv_ref[:, lanes])
        m = jnp.max(s, axis=-1, keepdims=True)
        p = jnp.exp(s - m)
        l = jnp.sum(p, axis=-1, keepdims=True)
        v = kv_ref[:, xw + hd * X_HEAD_DIM:xw + (hd + 1) * X_HEAD_DIM]
        heads.append(jnp.dot(p.astype(BF16), v, preferred_element_type=F32) / l)
    o = jnp.concatenate(heads, axis=-1).astype(BF16)
    o_ref[...] = h + jnp.dot(o, wo_ref[...], preferred_element_type=F32)


def xattn(h, g, wq, kv, wo, layer, *, tq):
    B, S, D = h.shape
    Mm = kv.shape[1]
    xw = X_HEADS * X_HEAD_DIM
    assert S % tq == 0
    return pl.pallas_call(
        _xattn_kernel,
        grid=(B, S // tq),
        in_specs=[pl.BlockSpec((None, tq, D), lambda b, i: (b, i, 0)),
                  pl.BlockSpec((1, D), lambda b, i: (0, 0)),
                  pl.BlockSpec((None, D, xw), lambda b, i: (layer, 0, 0)),
                  pl.BlockSpec((None, Mm, 2 * xw), lambda b, i: (b, 0, 0)),
                  pl.BlockSpec((None, xw, D), lambda b, i: (layer, 0, 0))],
        out_specs=pl.BlockSpec((None, tq, D), lambda b, i: (b, i, 0)),
        out_shape=jax.ShapeDtypeStruct((B, S, D), F32),
        compiler_params=_params("parallel", "parallel"),
        name="xattn",
    )(h, g, wq, kv, wo)


def _ffn_kernel(h_ref, g_ref, wg_ref, wu_ref, wd_ref, gf_ref, o_ref, xn_ref, *, final_norm):
    f = pl.program_id(1)

    @pl.when(f == 0)
    def _():
        h = h_ref[...]
        xn_ref[...] = _rms(h, g_ref[...]).astype(BF16)
        o_ref[...] = h

    xn = xn_ref[...]
    gate = jnp.dot(xn, wg_ref[...], preferred_element_type=F32)
    up = jnp.dot(xn, wu_ref[...], preferred_element_type=F32)
    act = (gate / (1.0 + jnp.exp(-gate))) * up
    o_ref[...] += jnp.dot(act.astype(BF16), wd_ref[...], preferred_element_type=F32)

    if final_norm:
        @pl.when(f == pl.num_programs(1) - 1)
        def _():
            o_ref[...] = _rms(o_ref[...], gf_ref[...])


def ffn(h, g, wg, wu, wd, gf, layer, *, tm, tf, final_norm):
    M, D = h.shape
    Fd = wg.shape[2]
    assert M % tm == 0 and Fd % tf == 0
    kern = functools.partial(_ffn_kernel, final_norm=final_norm)
    return pl.pallas_call(
        kern,
        grid=(M // tm, Fd // tf),
        in_specs=[pl.BlockSpec((tm, D), lambda i, f: (i, 0)),
                  pl.BlockSpec((1, D), lambda i, f: (0, 0)),
                  pl.BlockSpec((None, D, tf), lambda i, f: (layer, 0, f)),
                  pl.BlockSpec((None, D, tf), lambda i, f: (layer, 0, f)),
                  pl.BlockSpec((None, tf, D), lambda i, f: (layer, f, 0)),
                  pl.BlockSpec((1, D), lambda i, f: (0, 0))],
        out_specs=pl.BlockSpec((tm, D), lambda i, f: (i, 0)),
        out_shape=jax.ShapeDtypeStruct((M, D), F32),
        scratch_shapes=[pltpu.VMEM((tm, D), BF16)],
        compiler_params=_params("parallel", "arbitrary"),
        name="ffn",
    )(h, g, wg, wu, wd, gf)


def kernel(x, mem, norm_mix, w_in, sinks, gain_a, gain_b, gain_c, w_out, norm_xattn, norm_mem,
           wq_x, wk_x, wv_x, wo_x, norm_ffn, w_gate, w_up, w_down, norm_final):
    B, S, D = x.shape
    depth = w_in.shape[0]
    Mm = mem.shape[1]
    qa_w, kva_w = A_HEADS * HEAD_DIM, A_KV_HEADS * HEAD_DIM
    b_w, c_w = B_HEADS * HEAD_DIM, C_HEADS * HEAD_DIM
    widths = (qa_w, kva_w, kva_w, b_w, b_w, b_w, c_w, c_w, c_w)
    cols = [int(c) for c in np.concatenate([[0], np.cumsum(widths)[:-1]])]
    in_w = sum(widths)

    row = lambda v: v.reshape(1, -1)
    h = x.reshape(B * S, D)
    mem2 = mem.reshape(B * Mm, D)
    w_in_b, w_out_b = w_in.astype(BF16), w_out.astype(BF16)
    w_kv_b = jnp.concatenate([wk_x, wv_x], axis=2).astype(BF16)
    wq_b, wo_b = wq_x.astype(BF16), wo_x.astype(BF16)
    wg_b, wu_b, wd_b = w_gate.astype(BF16), w_up.astype(BF16), w_down.astype(BF16)
    for l in range(depth):
        proj = norm_matmul(h, row(norm_mix[l]), w_in_b, l, tm=1024, tn=1792, out_dtype=F32)
        proj = proj.reshape(B, S, in_w)
        oa = attn_a(proj, sinks[l], q_col=cols[0], k_col=cols[1], v_col=cols[2], tq=1024, group=4)
        ob = attn_b(proj, q_col=cols[3], k_col=cols[4], v_col=cols[5])
        oc = attn_c(proj, q_col=cols[6], k_col=cols[7], v_col=cols[8])
        h = mix_out(oa.reshape(B * S, qa_w), ob.reshape(B * S, b_w), oc.reshape(B * S, c_w),
                    row(gain_a[l]), row(gain_b[l]), row(gain_c[l]), h, w_out_b, l, tm=512)
        kv = norm_matmul(mem2, row(norm_mem[l]), w_kv_b, l, tm=512, tn=w_kv_b.shape[2], out_dtype=BF16)
        h = xattn(h.reshape(B, S, D), row(norm_xattn[l]), wq_b, kv.reshape(B, Mm, -1), wo_b, l,
                  tq=512).reshape(B * S, D)
        h = ffn(h, row(norm_ffn[l]), wg_b, wu_b, wd_b, row(norm_final), l, tm=1024, tf=512,
                final_norm=(l == depth - 1))
    return h.reshape(B, S, D)
```

```python
import functools

import numpy as np
import jax
import jax.numpy as jnp
from jax import lax
from jax.experimental import pallas as pl
from jax.experimental.pallas import tpu as pltpu

F32 = jnp.float32
BF16 = jnp.bfloat16
NEG_INF = float("-inf")
POS_INF = float("inf")
LOG2E = 1.4426950408889634

EPS = 1e-5
HEAD_DIM = 64
LANES = 128
BF16_ROWS = 16
A_HEADS, A_KV_HEADS, A_WINDOW, A_BLOCK = 8, 2, 128, 128
B_HEADS, MOBA_BLOCK, MOBA_TOPK = 8, 256, 3
C_HEADS = 16
C_PATTERNS = ((128, 1), (512, 4), (2048, 16))
C_STEPS = 128
C_GROUP = 16
GATE_UNROLL = 4
X_HEADS, X_HEAD_DIM = 4, 128
VMEM_LIMIT = 56 * 1024 * 1024
VT_ROWS = HEAD_DIM + BF16_ROWS


def _alibi_slopes(n):
    return jnp.asarray(2.0 ** (-8.0 * np.arange(1, n + 1) / n), F32)


def _params(*sem):
    return pltpu.CompilerParams(dimension_semantics=sem, vmem_limit_bytes=VMEM_LIMIT)


def _rms(x, g):
    return x * lax.rsqrt(jnp.mean(x * x, axis=-1, keepdims=True) + EPS) * g


def _dot_nt(a, b, precision=None):
    return lax.dot_general(a, b, (((1,), (1,)), ((), ())), precision=precision,
                           preferred_element_type=F32)


def _smem_spec():
    return pl.BlockSpec(memory_space=pltpu.SMEM)


def _zero_after(x):
    bits = pltpu.bitcast(x, jnp.uint32)
    return pltpu.bitcast(lax.shift_right_logical(bits, jnp.uint32(32)), F32)


def _norm_matmul_kernel(x_ref, g_ref, w_ref, o_ref, xn_ref):
    @pl.when(pl.program_id(1) == 0)
    def _():
        xn_ref[...] = _rms(x_ref[...], g_ref[...]).astype(BF16)

    o_ref[...] = jnp.dot(xn_ref[...], w_ref[...], preferred_element_type=F32).astype(o_ref.dtype)


def norm_matmul(x, g, w, layer, *, tm, tn, out_dtype):
    M, D = x.shape
    N = w.shape[2]
    assert M % tm == 0 and N % tn == 0
    return pl.pallas_call(
        _norm_matmul_kernel,
        grid=(M // tm, N // tn),
        in_specs=[pl.BlockSpec((tm, D), lambda i, j: (i, 0)),
                  pl.BlockSpec((1, D), lambda i, j: (0, 0)),
                  pl.BlockSpec((None, D, tn), lambda i, j: (layer, 0, j))],
        out_specs=pl.BlockSpec((tm, tn), lambda i, j: (i, j)),
        out_shape=jax.ShapeDtypeStruct((M, N), out_dtype),
        scratch_shapes=[pltpu.VMEM((tm, D), BF16)],
        compiler_params=_params("parallel", "arbitrary"),
        name="norm_matmul",
    )(x, g, w)


def _attn_a_kernel(slopes_ref, sinks_ref, q_ref, k_ref, v_ref, o_ref, kb_ref, vt_ref, bias_ref,
                   *, seq, n_sub, group):
    n = A_BLOCK
    qi = pl.program_id(1)
    heads_per_kv = A_HEADS // A_KV_HEADS
    lo_half = lax.broadcasted_iota(jnp.int32, (n, LANES), 1) < HEAD_DIM
    ones = jnp.ones((BF16_ROWS, 2 * n), BF16)

    @pl.when(qi == 0)
    def _():
        kk = lax.broadcasted_iota(jnp.int32, (2 * n, n), 0)
        qq = lax.broadcasted_iota(jnp.int32, (2 * n, n), 1)
        dist = qq + n - kk
        valid = (dist >= 0) & (dist < A_WINDOW)
        distf = dist.astype(F32)
        for j in range(A_HEADS):
            full = jnp.where(valid, (-slopes_ref[j] * LOG2E) * distf, NEG_INF)
            bias_ref[j, 1] = full
            bias_ref[j, 0] = jnp.where(kk >= n, full, NEG_INF)

        def prep(u, carry):
            for sub in range(4):
                rows = pl.ds(pl.multiple_of((4 * u + sub) * n, n), n)
                kb_ref[4 * u + sub] = k_ref[rows, :].astype(BF16)
                vt_ref[4 * u + sub] = v_ref[rows, :].T.astype(BF16)
            return carry

        lax.fori_loop(0, seq // (4 * n), prep, 0)

    def blocks(u, carry):
        subs = [u * group + g for g in range(group)]
        ts, prevs, variants, scores = [], [], [], []
        for sb in subs:
            t = qi * n_sub + sb
            ts.append(t)
            prevs.append(jnp.maximum(t - 1, 0))
            variants.append(jnp.where(t > 0, 1, 0))
            rows = pl.ds(pl.multiple_of(sb * n, n), n)
            stack = []
            for j in range(A_HEADS):
                qj = q_ref[rows, (j // 2) * LANES:(j // 2 + 1) * LANES] * (HEAD_DIM ** -0.5 * LOG2E)
                kv = j // heads_per_kv
                if j % 2 != kv:
                    qj = pltpu.roll(qj, HEAD_DIM, axis=1)
                stack.append(jnp.where(lo_half if kv == 0 else ~lo_half, qj, 0.0))
            q_all = jnp.concatenate(stack, axis=0).astype(BF16)
            kcat = jnp.concatenate([kb_ref[prevs[-1]], kb_ref[t]], axis=0)
            scores.append(_dot_nt(kcat, q_all))
        anchor = sum(_zero_after(sc[0:1, :n]) for sc in scores[1:]) if group > 1 else None
        for g, sb in enumerate(subs):
            rows = pl.ds(pl.multiple_of(sb * n, n), n)
            vtcat = jnp.concatenate([vt_ref[prevs[g]], vt_ref[ts[g]]], axis=1)
            outs = []
            for kv in range(A_KV_HEADS):
                probs, maxes = [], []
                for j in range(kv * heads_per_kv, (kv + 1) * heads_per_kv):
                    s = scores[g][:, j * n:(j + 1) * n] + bias_ref[j, variants[g]]
                    m = jnp.maximum(jnp.max(s, axis=0, keepdims=True), sinks_ref[j] * LOG2E)
                    if g == 0 and j == 0 and anchor is not None:
                        m = m + anchor
                    probs.append(jnp.exp2(s - m).astype(BF16))
                    maxes.append(m)
                vaug = jnp.concatenate([vtcat[kv * HEAD_DIM:(kv + 1) * HEAD_DIM], ones], axis=0)
                acc = jnp.dot(vaug, jnp.concatenate(probs, axis=1), preferred_element_type=F32)
                for c, j in enumerate(range(kv * heads_per_kv, (kv + 1) * heads_per_kv)):
                    cols = slice(c * n, (c + 1) * n)
                    denom = acc[HEAD_DIM:HEAD_DIM + 1, cols] + jnp.exp2(sinks_ref[j] * LOG2E - maxes[c])
                    outs.append(acc[:HEAD_DIM, cols] * (1.0 / denom))
            for pair in range(A_HEADS // 2):
                o_ref[rows, pair * LANES:(pair + 1) * LANES] = jnp.concatenate(outs[2 * pair:2 * pair + 2], axis=0).T
        return carry

    lax.fori_loop(0, n_sub // group, blocks, 0)


def attn_a(proj, sinks, *, q_col, k_col, v_col, tq, group):
    B, S, _ = proj.shape
    n = A_BLOCK
    qw = A_HEADS * HEAD_DIM
    assert S % tq == 0 and tq % (n * group) == 0 and S % (4 * n) == 0
    assert q_col % qw == 0 and k_col % LANES == 0 and v_col % LANES == 0
    assert A_WINDOW == n and (A_HEADS // A_KV_HEADS) % 2 == 0
    kern = functools.partial(_attn_a_kernel, seq=S, n_sub=tq // n, group=group)
    return pl.pallas_call(
        kern,
        grid=(B, S // tq),
        in_specs=[_smem_spec(), _smem_spec(),
                  pl.BlockSpec((None, tq, qw), lambda b, i: (b, i, q_col // qw)),
                  pl.BlockSpec((None, S, LANES), lambda b, i: (b, 0, k_col // LANES)),
                  pl.BlockSpec((None, S, LANES), lambda b, i: (b, 0, v_col // LANES))],
        out_specs=pl.BlockSpec((None, tq, qw), lambda b, i: (b, i, 0)),
        out_shape=jax.ShapeDtypeStruct((B, S, qw), F32),
        scratch_shapes=[pltpu.VMEM((S // n, n, LANES), BF16),
                        pltpu.VMEM((S // n, LANES, n), BF16),
                        pltpu.VMEM((A_HEADS, 2, 2 * n, n), F32)],
        compiler_params=_params("parallel", "arbitrary"),
        name="attn_swa",
    )(_alibi_slopes(A_HEADS), sinks, proj, proj, proj)


def _attn_b_kernel(slopes_ref, q_ref, k_ref, v_ref, o_ref,
                   kb_ref, vt_ref, kmean_ref, kcat_ref, bias_ref, qs_ref, sel_ref, acc_ref, m_ref, s_ref,
                   cmax_ref, ot_ref, *, nb):
    mb = MOBA_BLOCK
    hp = pl.program_id(1)
    slope2 = [slopes_ref[2 * hp + h] * LOG2E for h in range(2)]

    def prepare():
        kpos = lax.broadcasted_iota(jnp.int32, (mb, mb), 0)
        qpos = lax.broadcasted_iota(jnp.int32, (mb, mb), 1)
        relf = (qpos - kpos).astype(F32)
        for h in range(2):
            alibi = (-slope2[h]) * relf
            bias_ref[h, 0] = alibi
            bias_ref[h, 1] = jnp.where(qpos >= kpos, alibi, NEG_INF)
        ones = jnp.ones((BF16_ROWS, mb), BF16)

        def prep(n, carry):
            rows = pl.ds(pl.multiple_of(n * mb, mb), mb)
            kblk = k_ref[rows, :]
            kb_ref[n] = kblk.astype(BF16)
            kmean_ref[pl.ds(n, 1), :] = jnp.mean(kblk, axis=0, keepdims=True)
            vt = v_ref[rows, :].T
            for h in range(2):
                vt_ref[h, n, :HEAD_DIM, :] = vt[h * HEAD_DIM:(h + 1) * HEAD_DIM].astype(BF16)
                vt_ref[h, n, HEAD_DIM:, :] = ones
            return carry

        lax.fori_loop(0, nb, prep, 0)
        km = kmean_ref[...]
        lo_lanes = lax.broadcasted_iota(jnp.int32, (nb, LANES), 1) < HEAD_DIM
        for h in range(2):
            kmh = jnp.where(lo_lanes if h == 0 else ~lo_lanes, km, 0.0)
            hi = kmh.astype(BF16)
            lo = (kmh - hi.astype(F32)).astype(BF16)
            kcat_ref[h * nb:(h + 1) * nb, :] = jnp.concatenate([hi, hi, lo], axis=1)

        blk = lax.broadcasted_iota(jnp.int32, (nb, mb), 0)

        def select(u, carry):
            for sub in range(GATE_UNROLL):
                i = u * GATE_UNROLL + sub
                q = q_ref[pl.ds(pl.multiple_of(i * mb, mb), mb), :]
                past = blk < i
                q_hi = q.astype(BF16)
                q_lo = (q - q_hi.astype(F32)).astype(BF16)
                gate = _dot_nt(kcat_ref[...], jnp.concatenate([q_hi, q_lo, q_hi], axis=1))
                for h in range(2):
                    g = jnp.where(past, gate[h * nb:(h + 1) * nb], NEG_INF)
                    avail = blk >= 0
                    sel = jnp.where(blk == i, 1.0, 0.0)
                    for _ in range(MOBA_TOPK):
                        mx = jnp.max(jnp.where(avail, g, NEG_INF), axis=0, keepdims=True)
                        is_max = avail & (g == mx)
                        first = jnp.min(jnp.where(is_max, blk, nb), axis=0, keepdims=True)
                        pick = blk == first
                        sel = jnp.where(pick & past, 1.0, sel)
                        avail = avail & ~pick
                    sel_ref[i, h, :nb] = sel
                    sel_ref[i, h, nb:] = jnp.zeros((8, mb), F32)
            return carry

        lax.fori_loop(0, nb // GATE_UNROLL, select, 0)

    prepare()
    lo_half = lax.broadcasted_iota(jnp.int32, (mb, LANES), 1) < HEAD_DIM

    def scaled_queries(i):
        qc = q_ref[pl.ds(pl.multiple_of(i * mb, mb), mb), :] * (HEAD_DIM ** -0.5 * LOG2E)
        return jnp.concatenate([jnp.where(lo_half, qc, 0.0), jnp.where(lo_half, 0.0, qc)], axis=0).astype(BF16)

    def stage_scores(raw, own_first):
        variant_a = jnp.where(own_first, 1, 0)
        for h in range(2):
            cols = slice(h * mb, (h + 1) * mb)
            za = raw[:mb, cols] + bias_ref[h, variant_a]
            zb = raw[mb:, cols] + bias_ref[h, 0]
            s_ref[:mb, cols] = za
            s_ref[mb:, cols] = zb
            cmax_ref[h, 0] = jnp.max(za, axis=0, keepdims=True)
            cmax_ref[h, 1] = jnp.max(zb, axis=0, keepdims=True)

    q_first = scaled_queries(0)
    qs_ref[0] = q_first
    stage_scores(_dot_nt(jnp.concatenate([kb_ref[0], kb_ref[0]], axis=0), q_first), True)

    def query_block(i, carry0):
        slot = i % 2
        i_next = jnp.minimum(i + 1, nb - 1)
        qs_ref[1 - slot] = scaled_queries(i_next)
        m_ref[...] = jnp.full(m_ref.shape, NEG_INF, F32)
        acc_ref[...] = jnp.zeros(acc_ref.shape, F32)
        nsteps = (i + 2) // 2

        def step(p, carry):
            first = p == 0
            j0 = jnp.where(first, i, 2 * p - 1)
            j1 = jnp.minimum(2 * p, nb - 1)
            row1 = jnp.where(2 * p < i, j1, nb)
            last = p == nsteps - 1
            n0 = jnp.where(last, i_next, jnp.minimum(2 * p + 1, nb - 1))
            n1 = jnp.where(last, 0, jnp.minimum(2 * p + 2, nb - 1))
            q_slot = jnp.where(last, 1 - slot, slot)
            s_next = _dot_nt(jnp.concatenate([kb_ref[n0], kb_ref[n1]], axis=0), qs_ref[q_slot])
            dist_a = ((i - j0) * mb).astype(F32)
            dist_b = ((i - j1) * mb).astype(F32)
            for h in range(2):
                cols = slice(h * mb, (h + 1) * mb)
                shift_a = -slope2[h] * dist_a
                shift_b = -slope2[h] * dist_b
                chosen_a = sel_ref[i, h, pl.ds(j0, 1), :] > 0.5
                chosen_b = sel_ref[i, h, pl.ds(row1, 1), :] > 0.5
                m_a = jnp.where(chosen_a, cmax_ref[h, 0] + shift_a, NEG_INF)
                m_b = jnp.where(chosen_b, cmax_ref[h, 1] + shift_b, NEG_INF)
                m_old = m_ref[h]
                m_new = jnp.maximum(m_old, jnp.maximum(m_a, m_b))
                alpha = jnp.exp2(m_old - m_new)
                pa = jnp.exp2(s_ref[:mb, cols] - jnp.where(chosen_a, m_new - shift_a, POS_INF)).astype(BF16)
                pb = jnp.exp2(s_ref[mb:, cols] - jnp.where(chosen_b, m_new - shift_b, POS_INF)).astype(BF16)
                vt = jnp.concatenate([vt_ref[h, j0], vt_ref[h, j1]], axis=1)
                acc_ref[h] = alpha * acc_ref[h] + jnp.dot(vt, jnp.concatenate([pa, pb], axis=0),
                                                          preferred_element_type=F32)
                m_ref[h] = m_new
            stage_scores(s_next, last)
            return carry

        lax.fori_loop(0, nsteps, step, 0)
        outs = []
        for h in range(2):
            acc = acc_ref[h]
            outs.append(acc[:HEAD_DIM] * (1.0 / acc[HEAD_DIM:HEAD_DIM + 1]))
        ot_ref[i] = jnp.concatenate(outs, axis=0)
        return carry0

    lax.fori_loop(0, nb, query_block, 0)

    def emit(u, carry):
        for sub in range(GATE_UNROLL):
            i = u * GATE_UNROLL + sub
            o_ref[pl.ds(pl.multiple_of(i * mb, mb), mb), :] = ot_ref[i].T
        return carry

    lax.fori_loop(0, nb // GATE_UNROLL, emit, 0)


def attn_b(proj, *, q_col, k_col, v_col):
    B, S, _ = proj.shape
    mb = MOBA_BLOCK
    assert S % mb == 0
    nb = S // mb
    assert MOBA_TOPK < nb and nb % 8 == 0 and nb % GATE_UNROLL == 0
    pairs = B_HEADS // 2
    kern = functools.partial(_attn_b_kernel, nb=nb)
    return pl.pallas_call(
        kern,
        grid=(B, pairs),
        in_specs=[_smem_spec(),
                  pl.BlockSpec((None, S, LANES), lambda b, p: (b, 0, q_col // LANES + p)),
                  pl.BlockSpec((None, S, LANES), lambda b, p: (b, 0, k_col // LANES + p)),
                  pl.BlockSpec((None, S, LANES), lambda b, p: (b, 0, v_col // LANES + p))],
        out_specs=pl.BlockSpec((None, S, LANES), lambda b, p: (b, 0, p)),
        out_shape=jax.ShapeDtypeStruct((B, S, B_HEADS * HEAD_DIM), F32),
        scratch_shapes=[pltpu.VMEM((nb, mb, LANES), BF16),
                        pltpu.VMEM((2, nb, VT_ROWS, mb), BF16),
                        pltpu.VMEM((nb, LANES), F32),
                        pltpu.VMEM((2 * nb, 3 * LANES), BF16),
                        pltpu.VMEM((2, 2, mb, mb), F32),
                        pltpu.VMEM((2, 2 * mb, LANES), BF16),
                        pltpu.VMEM((nb, 2, nb + 8, mb), F32),
                        pltpu.VMEM((2, VT_ROWS, mb), F32),
                        pltpu.VMEM((2, 1, mb), F32),
                        pltpu.VMEM((2 * mb, 2 * mb), F32),
                        pltpu.VMEM((2, 2, 1, mb), F32),
                        pltpu.VMEM((nb, LANES, mb), F32)],
        compiler_params=_params("parallel", "parallel"),
        name="attn_moba",
    )(_alibi_slopes(B_HEADS), proj, proj, proj)


def _attn_c_kernel(slopes_ref, q_ref, k_ref, v_ref, o_ref,
                   q4_ref, k4_ref, v4_ref, kb_ref, vt_ref, bias_ref, orun_ref, lse_ref, *, seq, group):
    n = C_STEPS
    hp = pl.program_id(1)
    ntile = seq // n
    quarter = seq // 4
    slab = n // 4
    lo_half = lax.broadcasted_iota(jnp.int32, (n, LANES), 1) < HEAD_DIM
    ones = jnp.ones((BF16_ROWS, 2 * n), BF16)
    kk = lax.broadcasted_iota(jnp.int32, (2 * n, n), 0)
    qq = lax.broadcasted_iota(jnp.int32, (2 * n, n), 1)

    def reorder(c, carry):
        r4 = c // (ntile // 4)
        src = pl.ds(r4 + 4 * n * (c % (ntile // 4)), n, stride=4)
        dst = pl.ds(pl.multiple_of(c * n, n), n)
        q4_ref[dst, :] = q_ref[src, :]
        k4_ref[dst, :] = k_ref[src, :]
        v4_ref[dst, :] = v_ref[src, :]
        return carry

    lax.fori_loop(0, ntile, reorder, 0)

    for bi, (w, d) in enumerate(C_PATTERNS):
        nblk = seq // w
        first, last = bi == 0, bi == len(C_PATTERNS) - 1
        assert d in (1, 4, 16)

        def load_tile(ref, t, d=d):
            if d == 4:
                return ref[pl.ds(pl.multiple_of(t * n, n), n), :]
            if d == 16:
                r16, b16 = t // 2, t % 2
                return ref[pl.ds((r16 % 4) * quarter + b16 * (4 * n) + r16 // 4, n, stride=4), :]
            return jnp.concatenate([ref[pl.ds(pl.multiple_of(r4 * quarter + t * slab, slab), slab), :]
                                    for r4 in range(4)], axis=0)

        def store_tile(ref, t, val, d=d):
            if d == 4:
                ref[pl.ds(pl.multiple_of(t * n, n), n), :] = val
            elif d == 16:
                r16, b16 = t // 2, t % 2
                ref[pl.ds((r16 % 4) * quarter + b16 * (4 * n) + r16 // 4, n, stride=4), :] = val
            else:
                for r4 in range(4):
                    ref[pl.ds(pl.multiple_of(r4 * quarter + t * slab, slab), slab), :] = val[r4 * slab:(r4 + 1) * slab]

        if d == 1:
            member = lambda x: 4 * (x % slab) + x // slab
            steps = member(qq) - member(kk % n) + jnp.where(kk < n, n, 0)
        else:
            steps = qq + n - kk
        valid = (steps >= 0) & (steps <= n)
        stepf = steps.astype(F32)
        for h in range(2):
            full = jnp.where(valid, (-slopes_ref[2 * hp + h] * (LOG2E * d)) * stepf, NEG_INF)
            bias_ref[h, 1] = full
            bias_ref[h, 0] = jnp.where(kk >= n, full, NEG_INF)

        def prep(u, carry, load_tile=load_tile):
            for sub in range(group):
                t = u * group + sub
                kb_ref[t] = load_tile(k4_ref, t).astype(BF16)
                vt_ref[t] = load_tile(v4_ref, t).T.astype(BF16)
            return carry

        lax.fori_loop(0, ntile // group, prep, 0)

        def tiles(u, carry, load_tile=load_tile, store_tile=store_tile, nblk=nblk, first=first):
            ts = [u * group + sub for sub in range(group)]
            prevs, variants, scores = [], [], []
            for t in ts:
                has_prev = (t % nblk) > 0
                tp = jnp.where(has_prev, t - 1, t)
                prevs.append(tp)
                variants.append(jnp.where(has_prev, 1, 0))
                q = load_tile(q4_ref, t) * (HEAD_DIM ** -0.5 * LOG2E)
                q_both = jnp.concatenate([jnp.where(lo_half, q, 0.0), jnp.where(lo_half, 0.0, q)], axis=0)
                kcat = jnp.concatenate([kb_ref[tp], kb_ref[t]], axis=0)
                scores.append(_dot_nt(kcat, q_both.astype(BF16)))
            anchor = sum(_zero_after(sc[0:1, :n]) for sc in scores[1:])
            results = []
            for g, t in enumerate(ts):
                vtcat = jnp.concatenate([vt_ref[prevs[g]], vt_ref[t]], axis=1)
                o_rows, lse_rows = [], []
                for h in range(2):
                    s = scores[g][:, h * n:(h + 1) * n] + bias_ref[h, variants[g]]
                    m = jnp.max(s, axis=0, keepdims=True)
                    if g == 0:
                        m = m + anchor
                    p = jnp.exp2(s - m).astype(BF16)
                    vaug = jnp.concatenate([vtcat[h * HEAD_DIM:(h + 1) * HEAD_DIM], ones], axis=0)
                    acc = jnp.dot(vaug, p, preferred_element_type=F32)
                    l = acc[HEAD_DIM:HEAD_DIM + 1]
                    o_rows.append(acc[:HEAD_DIM] * (1.0 / l))
                    lse_rows.append(jnp.broadcast_to(m + jnp.log2(l), (HEAD_DIM, n)))
                results.append((jnp.concatenate(o_rows, axis=0), jnp.concatenate(lse_rows, axis=0)))
            for g, t in enumerate(ts):
                o_t = results[g][0].T
                lse_t = results[g][1].T
                if first:
                    store_tile(orun_ref, t, o_t)
                    store_tile(lse_ref, t, lse_t)
                else:
                    lse_old = load_tile(lse_ref, t)
                    top = jnp.maximum(lse_old, lse_t)
                    w_old = jnp.exp2(lse_old - top)
                    w_t = jnp.exp2(lse_t - top)
                    den = w_old + w_t
                    store_tile(orun_ref, t, (w_old * load_tile(orun_ref, t) + w_t * o_t) * (1.0 / den))
                    if not last:
                        store_tile(lse_ref, t, top + jnp.log2(den))
            return carry

        lax.fori_loop(0, ntile // group, tiles, 0)

    def restore(c, carry):
        r4 = c // (ntile // 4)
        o_ref[pl.ds(r4 + 4 * n * (c % (ntile // 4)), n, stride=4), :] = orun_ref[pl.ds(pl.multiple_of(c * n, n), n), :]
        return carry

    lax.fori_loop(0, ntile, restore, 0)


def attn_c(proj, *, q_col, k_col, v_col):
    B, S, _ = proj.shape
    n = C_STEPS
    assert all(S % w == 0 and w // d == n for w, d in C_PATTERNS)
    assert (S // n) % C_GROUP == 0 and S % (16 * n) == 0
    pairs = C_HEADS // 2
    kern = functools.partial(_attn_c_kernel, seq=S, group=C_GROUP)
    return pl.pallas_call(
        kern,
        grid=(B, pairs),
        in_specs=[_smem_spec(),
                  pl.BlockSpec((None, S, LANES), lambda b, p: (b, 0, q_col // LANES + p)),
                  pl.BlockSpec((None, S, LANES), lambda b, p: (b, 0, k_col // LANES + p)),
                  pl.BlockSpec((None, S, LANES), lambda b, p: (b, 0, v_col // LANES + p))],
        out_specs=pl.BlockSpec((None, S, LANES), lambda b, p: (b, 0, p)),
        out_shape=jax.ShapeDtypeStruct((B, S, C_HEADS * HEAD_DIM), F32),
        scratch_shapes=[pltpu.VMEM((S, LANES), F32),
                        pltpu.VMEM((S, LANES), F32),
                        pltpu.VMEM((S, LANES), F32),
                        pltpu.VMEM((S // n, n, LANES), BF16),
                        pltpu.VMEM((S // n, LANES, n), BF16),
                        pltpu.VMEM((2, 2, 2 * n, n), F32),
                        pltpu.VMEM((S, LANES), F32),
                        pltpu.VMEM((S, LANES), F32)],
        compiler_params=_params("parallel", "parallel"),
        name="attn_dilated",
    )(_alibi_slopes(C_HEADS), proj, proj, proj)


def _mix_out_kernel(oa_ref, ob_ref, oc_ref, ga_ref, gb_ref, gc_ref, h_ref, w_ref, o_ref):
    y = jnp.concatenate([_rms(oa_ref[...], ga_ref[...]).astype(BF16),
                         _rms(ob_ref[...], gb_ref[...]).astype(BF16),
                         _rms(oc_ref[...], gc_ref[...]).astype(BF16)], axis=-1)
    o_ref[...] = h_ref[...] + jnp.dot(y, w_ref[...], preferred_element_type=F32)


def mix_out(oa, ob, oc, ga, gb, gc, h, w, layer, *, tm):
    M, D = h.shape
    wa, wb, wc = oa.shape[1], ob.shape[1], oc.shape[1]
    K = wa + wb + wc
    assert w.shape[1:] == (K, D) and M % tm == 0
    row = lambda width: pl.BlockSpec((tm, width), lambda i: (i, 0))
    gain = lambda width: pl.BlockSpec((1, width), lambda i: (0, 0))
    return pl.pallas_call(
        _mix_out_kernel,
        grid=(M // tm,),
        in_specs=[row(wa), row(wb), row(wc), gain(wa), gain(wb), gain(wc), row(D),
                  pl.BlockSpec((None, K, D), lambda i: (layer, 0, 0))],
        out_specs=row(D),
        out_shape=jax.ShapeDtypeStruct((M, D), F32),
        compiler_params=_params("parallel"),
        name="mix_out",
    )(oa, ob, oc, ga, gb, gc, h, w)


def _xattn_kernel(h_ref, g_ref, wq_ref, kv_ref, wo_ref, o_ref):
    xw = X_HEADS * X_HEAD_DIM
    h = h_ref[...]
    xn = _rms(h, g_ref[...]).astype(BF16)
    q = jnp.dot(xn, wq_ref[...], preferred_element_type=F32) * (X_HEAD_DIM ** -0.5)
    heads = []
    for hd in range(X_HEADS):
        lanes = slice(hd * X_HEAD_DIM, (hd + 1) * X_HEAD_DIM)
        s = _dot_nt(q[:, lanes].astype(BF16), kv_ref[:, lanes])
        m = jnp.max(s, axis=-1, keepdims=True)
        p = jnp.exp(s - m)
        l = jnp.sum(p, axis=-1, keepdims=True)
        v = kv_ref[:, xw + hd * X_HEAD_DIM:xw + (hd + 1) * X_HEAD_DIM]
        heads.append(jnp.dot(p.astype(BF16), v, preferred_element_type=F32) / l)
    o = jnp.concatenate(heads, axis=-1).astype(BF16)
    o_ref[...] = h + jnp.dot(o, wo_ref[...], preferred_element_type=F32)


def xattn(h, g, wq, kv, wo, layer, *, tq):
    B, S, D = h.shape
    Mm = kv.shape[1]
    xw = X_HEADS * X_HEAD_DIM
    assert S % tq == 0
    return pl.pallas_call(
        _xattn_kernel,
        grid=(B, S // tq),
        in_specs=[pl.BlockSpec((None, tq, D), lambda b, i: (b, i, 0)),
                  pl.BlockSpec((1, D), lambda b, i: (0, 0)),
                  pl.BlockSpec((None, D, xw), lambda b, i: (layer, 0, 0)),
                  pl.BlockSpec((None, Mm, 2 * xw), lambda b, i: (b, 0, 0)),
                  pl.BlockSpec((None, xw, D), lambda b, i: (layer, 0, 0))],
        out_specs=pl.BlockSpec((None, tq, D), lambda b, i: (b, i, 0)),
        out_shape=jax.ShapeDtypeStruct((B, S, D), F32),
        compiler_params=_params("parallel", "parallel"),
        name="xattn",
    )(h, g, wq, kv, wo)


def _ffn_kernel(h_ref, g_ref, wg_ref, wu_ref, wd_ref, gf_ref, o_ref, xn_ref, *, final_norm):
    f = pl.program_id(1)

    @pl.when(f == 0)
    def _():
        h = h_ref[...]
        xn_ref[...] = _rms(h, g_ref[...]).astype(BF16)
        o_ref[...] = h

    xn = xn_ref[...]
    gate = jnp.dot(xn, wg_ref[...], preferred_element_type=F32)
    up = jnp.dot(xn, wu_ref[...], preferred_element_type=F32)
    act = (gate / (1.0 + jnp.exp(-gate))) * up
    o_ref[...] += jnp.dot(act.astype(BF16), wd_ref[...], preferred_element_type=F32)

    if final_norm:
        @pl.when(f == pl.num_programs(1) - 1)
        def _():
            o_ref[...] = _rms(o_ref[...], gf_ref[...])


def ffn(h, g, wg, wu, wd, gf, layer, *, tm, tf, final_norm):
    M, D = h.shape
    Fd = wg.shape[2]
    assert M % tm == 0 and Fd % tf == 0
    kern = functools.partial(_ffn_kernel, final_norm=final_norm)
    return pl.pallas_call(
        kern,
        grid=(M // tm, Fd // tf),
        in_specs=[pl.BlockSpec((tm, D), lambda i, f: (i, 0)),
                  pl.BlockSpec((1, D), lambda i, f: (0, 0)),
                  pl.BlockSpec((None, D, tf), lambda i, f: (layer, 0, f)),
                  pl.BlockSpec((None, D, tf), lambda i, f: (layer, 0, f)),
                  pl.BlockSpec((None, tf, D), lambda i, f: (layer, f, 0)),
                  pl.BlockSpec((1, D), lambda i, f: (0, 0))],
        out_specs=pl.BlockSpec((tm, D), lambda i, f: (i, 0)),
        out_shape=jax.ShapeDtypeStruct((M, D), F32),
        scratch_shapes=[pltpu.VMEM((tm, D), BF16)],
        compiler_params=_params("parallel", "arbitrary"),
        name="ffn",
    )(h, g, wg, wu, wd, gf)


def kernel(x, mem, norm_mix, w_in, sinks, gain_a, gain_b, gain_c, w_out, norm_xattn, norm_mem,
           wq_x, wk_x, wv_x, wo_x, norm_ffn, w_gate, w_up, w_down, norm_final):
    B, S, D = x.shape
    depth = w_in.shape[0]
    Mm = mem.shape[1]
    qa_w, kva_w = A_HEADS * HEAD_DIM, A_KV_HEADS * HEAD_DIM
    b_w, c_w = B_HEADS * HEAD_DIM, C_HEADS * HEAD_DIM
    widths = (qa_w, kva_w, kva_w, b_w, b_w, b_w, c_w, c_w, c_w)
    cols = [int(c) for c in np.concatenate([[0], np.cumsum(widths)[:-1]])]
    in_w = sum(widths)

    row = lambda v: v.reshape(1, -1)
    h = x.reshape(B * S, D)
    mem2 = mem.reshape(B * Mm, D)
    w_in_b, w_out_b = w_in.astype(BF16), w_out.astype(BF16)
    w_kv_b = jnp.concatenate([wk_x, wv_x], axis=2).astype(BF16)
    wq_b, wo_b = wq_x.astype(BF16), wo_x.astype(BF16)
    wg_b, wu_b, wd_b = w_gate.astype(BF16), w_up.astype(BF16), w_down.astype(BF16)
    for l in range(depth):
        proj = norm_matmul(h, row(norm_mix[l]), w_in_b, l, tm=1024, tn=1792, out_dtype=F32)
        proj = proj.reshape(B, S, in_w)
        oa = attn_a(proj, sinks[l], q_col=cols[0], k_col=cols[1], v_col=cols[2], tq=1024, group=4)
        ob = attn_b(proj, q_col=cols[3], k_col=cols[4], v_col=cols[5])
        oc = attn_c(proj, q_col=cols[6], k_col=cols[7], v_col=cols[8])
        h = mix_out(oa.reshape(B * S, qa_w), ob.reshape(B * S, b_w), oc.reshape(B * S, c_w),
                    row(gain_a[l]), row(gain_b[l]), row(gain_c[l]), h, w_out_b, l, tm=512)
        kv = norm_matmul(mem2, row(norm_mem[l]), w_kv_b, l, tm=512, tn=w_kv_b.shape[2], out_dtype=BF16)
        h = xattn(h.reshape(B, S, D), row(norm_xattn[l]), wq_b, kv.reshape(B, Mm, -1), wo_b, l,
                  tq=512).reshape(B * S, D)
        h = ffn(h, row(norm_ffn[l]), wg_b, wu_b, wd_b, row(norm_final), l, tm=1024, tf=512,
                final_norm=(l == depth - 1))
    return h.reshape(B, S, D)
```

```python
import functools

import numpy as np
import jax
import jax.numpy as jnp
from jax import lax
from jax.experimental import pallas as pl
from jax.experimental.pallas import tpu as pltpu

F32 = jnp.float32
BF16 = jnp.bfloat16
NEG_INF = float("-inf")
POS_INF = float("inf")
LOG2E = 1.4426950408889634

EPS = 1e-5
HEAD_DIM = 64
LANES = 128
BF16_ROWS = 16
A_HEADS, A_KV_HEADS, A_WINDOW, A_BLOCK = 8, 2, 128, 128
B_HEADS, MOBA_BLOCK, MOBA_TOPK = 8, 256, 3
C_HEADS = 16
C_PATTERNS = ((128, 1), (512, 4), (2048, 16))
C_STEPS = 128
C_GROUP = 16
GATE_UNROLL = 4
SUBLANES = 8
X_HEADS, X_HEAD_DIM = 4, 128
VMEM_LIMIT = 56 * 1024 * 1024
VT_ROWS = HEAD_DIM + BF16_ROWS


def _alibi_slopes(n):
    return jnp.asarray(2.0 ** (-8.0 * np.arange(1, n + 1) / n), F32)


def _params(*sem):
    return pltpu.CompilerParams(dimension_semantics=sem, vmem_limit_bytes=VMEM_LIMIT)


def _rms(x, g):
    return x * lax.rsqrt(jnp.mean(x * x, axis=-1, keepdims=True) + EPS) * g


def _dot_nt(a, b, precision=None):
    return lax.dot_general(a, b, (((1,), (1,)), ((), ())), precision=precision,
                           preferred_element_type=F32)


def _smem_spec():
    return pl.BlockSpec(memory_space=pltpu.SMEM)


def _zero_after(x):
    bits = pltpu.bitcast(x, jnp.uint32)
    return pltpu.bitcast(lax.shift_right_logical(bits, jnp.uint32(32)), F32)


def _norm_matmul_kernel(x_ref, g_ref, w_ref, o_ref, xn_ref):
    @pl.when(pl.program_id(1) == 0)
    def _():
        xn_ref[...] = _rms(x_ref[...], g_ref[...]).astype(BF16)

    o_ref[...] = jnp.dot(xn_ref[...], w_ref[...], preferred_element_type=F32).astype(o_ref.dtype)


def norm_matmul(x, g, w, layer, *, tm, tn, out_dtype):
    M, D = x.shape
    N = w.shape[2]
    assert M % tm == 0 and N % tn == 0
    return pl.pallas_call(
        _norm_matmul_kernel,
        grid=(M // tm, N // tn),
        in_specs=[pl.BlockSpec((tm, D), lambda i, j: (i, 0)),
                  pl.BlockSpec((1, D), lambda i, j: (0, 0)),
                  pl.BlockSpec((None, D, tn), lambda i, j: (layer, 0, j))],
        out_specs=pl.BlockSpec((tm, tn), lambda i, j: (i, j)),
        out_shape=jax.ShapeDtypeStruct((M, N), out_dtype),
        scratch_shapes=[pltpu.VMEM((tm, D), BF16)],
        compiler_params=_params("parallel", "arbitrary"),
        name="norm_matmul",
    )(x, g, w)


def _attn_a_kernel(slopes_ref, sinks_ref, q_ref, k_ref, v_ref, o_ref, kb_ref, vt_ref, bias_ref,
                   *, seq, n_sub, group):
    n = A_BLOCK
    qi = pl.program_id(1)
    heads_per_kv = A_HEADS // A_KV_HEADS
    lo_half = lax.broadcasted_iota(jnp.int32, (n, LANES), 1) < HEAD_DIM
    ones = jnp.ones((BF16_ROWS, 2 * n), BF16)

    @pl.when(qi == 0)
    def _():
        kk = lax.broadcasted_iota(jnp.int32, (2 * n, n), 0)
        qq = lax.broadcasted_iota(jnp.int32, (2 * n, n), 1)
        dist = qq + n - kk
        valid = (dist >= 0) & (dist < A_WINDOW)
        distf = dist.astype(F32)
        for j in range(A_HEADS):
            full = jnp.where(valid, (-slopes_ref[j] * LOG2E) * distf, NEG_INF)
            bias_ref[j, 1] = full
            bias_ref[j, 0] = jnp.where(kk >= n, full, NEG_INF)

        def prep(u, carry):
            for sub in range(4):
                rows = pl.ds(pl.multiple_of((4 * u + sub) * n, n), n)
                kb_ref[4 * u + sub] = k_ref[rows, :].astype(BF16)
                vt_ref[4 * u + sub] = v_ref[rows, :].T.astype(BF16)
            return carry

        lax.fori_loop(0, seq // (4 * n), prep, 0)

    def blocks(u, carry):
        subs = [u * group + g for g in range(group)]
        ts, prevs, variants, scores = [], [], [], []
        for sb in subs:
            t = qi * n_sub + sb
            ts.append(t)
            prevs.append(jnp.maximum(t - 1, 0))
            variants.append(jnp.where(t > 0, 1, 0))
            rows = pl.ds(pl.multiple_of(sb * n, n), n)
            stack = []
            for j in range(A_HEADS):
                qj = q_ref[rows, (j // 2) * LANES:(j // 2 + 1) * LANES] * (HEAD_DIM ** -0.5 * LOG2E)
                kv = j // heads_per_kv
                if j % 2 != kv:
                    qj = pltpu.roll(qj, HEAD_DIM, axis=1)
                stack.append(jnp.where(lo_half if kv == 0 else ~lo_half, qj, 0.0))
            q_all = jnp.concatenate(stack, axis=0).astype(BF16)
            kcat = jnp.concatenate([kb_ref[prevs[-1]], kb_ref[t]], axis=0)
            scores.append(_dot_nt(kcat, q_all))
        anchor = sum(_zero_after(sc[0:1, :n]) for sc in scores[1:]) if group > 1 else None
        for g, sb in enumerate(subs):
            rows = pl.ds(pl.multiple_of(sb * n, n), n)
            vtcat = jnp.concatenate([vt_ref[prevs[g]], vt_ref[ts[g]]], axis=1)
            outs = []
            for kv in range(A_KV_HEADS):
                probs, maxes = [], []
                for j in range(kv * heads_per_kv, (kv + 1) * heads_per_kv):
                    s = scores[g][:, j * n:(j + 1) * n] + bias_ref[j, variants[g]]
                    m = jnp.maximum(jnp.max(s, axis=0, keepdims=True), sinks_ref[j] * LOG2E)
                    if g == 0 and j == 0 and anchor is not None:
                        m = m + anchor
                    probs.append(jnp.exp2(s - m).astype(BF16))
                    maxes.append(m)
                vaug = jnp.concatenate([vtcat[kv * HEAD_DIM:(kv + 1) * HEAD_DIM], ones], axis=0)
                acc = jnp.dot(vaug, jnp.concatenate(probs, axis=1), preferred_element_type=F32)
                for c, j in enumerate(range(kv * heads_per_kv, (kv + 1) * heads_per_kv)):
                    cols = slice(c * n, (c + 1) * n)
                    denom = acc[HEAD_DIM:HEAD_DIM + 1, cols] + jnp.exp2(sinks_ref[j] * LOG2E - maxes[c])
                    outs.append(acc[:HEAD_DIM, cols] * (1.0 / denom))
            for pair in range(A_HEADS // 2):
                o_ref[rows, pair * LANES:(pair + 1) * LANES] = jnp.concatenate(outs[2 * pair:2 * pair + 2], axis=0).T
        return carry

    lax.fori_loop(0, n_sub // group, blocks, 0)


def attn_a(proj, sinks, *, q_col, k_col, v_col, tq, group):
    B, S, _ = proj.shape
    n = A_BLOCK
    qw = A_HEADS * HEAD_DIM
    assert S % tq == 0 and tq % (n * group) == 0 and S % (4 * n) == 0
    assert q_col % qw == 0 and k_col % LANES == 0 and v_col % LANES == 0
    assert A_WINDOW == n and (A_HEADS // A_KV_HEADS) % 2 == 0
    kern = functools.partial(_attn_a_kernel, seq=S, n_sub=tq // n, group=group)
    return pl.pallas_call(
        kern,
        grid=(B, S // tq),
        in_specs=[_smem_spec(), _smem_spec(),
                  pl.BlockSpec((None, tq, qw), lambda b, i: (b, i, q_col // qw)),
                  pl.BlockSpec((None, S, LANES), lambda b, i: (b, 0, k_col // LANES)),
                  pl.BlockSpec((None, S, LANES), lambda b, i: (b, 0, v_col // LANES))],
        out_specs=pl.BlockSpec((None, tq, qw), lambda b, i: (b, i, 0)),
        out_shape=jax.ShapeDtypeStruct((B, S, qw), F32),
        scratch_shapes=[pltpu.VMEM((S // n, n, LANES), BF16),
                        pltpu.VMEM((S // n, LANES, n), BF16),
                        pltpu.VMEM((A_HEADS, 2, 2 * n, n), F32)],
        compiler_params=_params("parallel", "arbitrary"),
        name="attn_swa",
    )(_alibi_slopes(A_HEADS), sinks, proj, proj, proj)


def _attn_b_kernel(slopes_ref, q_ref, k_ref, v_ref, o_ref,
                   kb_ref, vt_ref, kmean_ref, kcat_ref, bias_ref, qs_ref, sel_ref, acc_ref, m_ref, s_ref,
                   cmax_ref, ot_ref, *, nb):
    mb = MOBA_BLOCK
    hp = pl.program_id(1)
    slope2 = [slopes_ref[2 * hp + h] * LOG2E for h in range(2)]

    def prepare():
        kpos = lax.broadcasted_iota(jnp.int32, (mb, mb), 0)
        qpos = lax.broadcasted_iota(jnp.int32, (mb, mb), 1)
        relf = (qpos - kpos).astype(F32)
        for h in range(2):
            alibi = (-slope2[h]) * relf
            bias_ref[h, 0] = alibi
            bias_ref[h, 1] = jnp.where(qpos >= kpos, alibi, NEG_INF)
        ones = jnp.ones((BF16_ROWS, mb), BF16)

        def prep(n, carry):
            rows = pl.ds(pl.multiple_of(n * mb, mb), mb)
            kblk = k_ref[rows, :]
            kb_ref[n] = kblk.astype(BF16)
            kmean_ref[pl.ds(n, 1), :] = jnp.mean(kblk, axis=0, keepdims=True)
            vt = v_ref[rows, :].T
            for h in range(2):
                vt_ref[h, n, :HEAD_DIM, :] = vt[h * HEAD_DIM:(h + 1) * HEAD_DIM].astype(BF16)
                vt_ref[h, n, HEAD_DIM:, :] = ones
            return carry

        lax.fori_loop(0, nb, prep, 0)
        km = kmean_ref[...]
        lo_lanes = lax.broadcasted_iota(jnp.int32, (nb, LANES), 1) < HEAD_DIM
        for h in range(2):
            kmh = jnp.where(lo_lanes if h == 0 else ~lo_lanes, km, 0.0)
            hi = kmh.astype(BF16)
            lo = (kmh - hi.astype(F32)).astype(BF16)
            kcat_ref[h * nb:(h + 1) * nb, :] = jnp.concatenate([hi, hi, lo], axis=1)

        blk = lax.broadcasted_iota(jnp.int32, (nb, mb), 0)

        def select(u, carry):
            for sub in range(GATE_UNROLL):
                i = u * GATE_UNROLL + sub
                q = q_ref[pl.ds(pl.multiple_of(i * mb, mb), mb), :]
                past = blk < i
                q_hi = q.astype(BF16)
                q_lo = (q - q_hi.astype(F32)).astype(BF16)
                gate = _dot_nt(kcat_ref[...], jnp.concatenate([q_hi, q_lo, q_hi], axis=1))
                for h in range(2):
                    g = jnp.where(past, gate[h * nb:(h + 1) * nb], NEG_INF)
                    avail = blk >= 0
                    sel = jnp.where(blk == i, 1.0, 0.0)
                    for _ in range(MOBA_TOPK):
                        mx = jnp.max(jnp.where(avail, g, NEG_INF), axis=0, keepdims=True)
                        is_max = avail & (g == mx)
                        first = jnp.min(jnp.where(is_max, blk, nb), axis=0, keepdims=True)
                        pick = blk == first
                        sel = jnp.where(pick & past, 1.0, sel)
                        avail = avail & ~pick
                    sel_ref[i, h, :nb] = sel
                    sel_ref[i, h, nb:] = jnp.zeros((SUBLANES, mb), F32)
            return carry

        lax.fori_loop(0, nb // GATE_UNROLL, select, 0)

    prepare()
    lo_half = lax.broadcasted_iota(jnp.int32, (mb, LANES), 1) < HEAD_DIM

    def scaled_queries(i):
        qc = q_ref[pl.ds(pl.multiple_of(i * mb, mb), mb), :] * (HEAD_DIM ** -0.5 * LOG2E)
        return jnp.concatenate([jnp.where(lo_half, qc, 0.0), jnp.where(lo_half, 0.0, qc)], axis=0).astype(BF16)

    def stage_scores(raw, own_first):
        variant_a = jnp.where(own_first, 1, 0)
        for h in range(2):
            cols = slice(h * mb, (h + 1) * mb)
            za = raw[:mb, cols] + bias_ref[h, variant_a]
            zb = raw[mb:, cols] + bias_ref[h, 0]
            s_ref[:mb, cols] = za
            s_ref[mb:, cols] = zb
            cmax_ref[h, 0] = jnp.max(za, axis=0, keepdims=True)
            cmax_ref[h, 1] = jnp.max(zb, axis=0, keepdims=True)

    q_first = scaled_queries(0)
    qs_ref[0] = q_first
    stage_scores(_dot_nt(jnp.concatenate([kb_ref[0], kb_ref[0]], axis=0), q_first), True)

    def query_block(i, carry0):
        slot = i % 2
        i_next = jnp.minimum(i + 1, nb - 1)
        qs_ref[1 - slot] = scaled_queries(i_next)
        m_ref[...] = jnp.full(m_ref.shape, NEG_INF, F32)
        acc_ref[...] = jnp.zeros(acc_ref.shape, F32)
        nsteps = (i + 2) // 2

        def step(p, carry):
            first = p == 0
            j0 = jnp.where(first, i, 2 * p - 1)
            j1 = jnp.minimum(2 * p, nb - 1)
            row1 = jnp.where(2 * p < i, j1, nb)
            last = p == nsteps - 1
            n0 = jnp.where(last, i_next, jnp.minimum(2 * p + 1, nb - 1))
            n1 = jnp.where(last, 0, jnp.minimum(2 * p + 2, nb - 1))
            q_slot = jnp.where(last, 1 - slot, slot)
            s_next = _dot_nt(jnp.concatenate([kb_ref[n0], kb_ref[n1]], axis=0), qs_ref[q_slot])
            dist_a = ((i - j0) * mb).astype(F32)
            dist_b = ((i - j1) * mb).astype(F32)
            for h in range(2):
                cols = slice(h * mb, (h + 1) * mb)
                shift_a = -slope2[h] * dist_a
                shift_b = -slope2[h] * dist_b
                chosen_a = sel_ref[i, h, pl.ds(j0, 1), :] > 0.5
                chosen_b = sel_ref[i, h, pl.ds(row1, 1), :] > 0.5
                m_a = jnp.where(chosen_a, cmax_ref[h, 0] + shift_a, NEG_INF)
                m_b = jnp.where(chosen_b, cmax_ref[h, 1] + shift_b, NEG_INF)
                m_old = m_ref[h]
                m_new = jnp.maximum(m_old, jnp.maximum(m_a, m_b))
                alpha = jnp.exp2(m_old - m_new)
                pa = jnp.exp2(s_ref[:mb, cols] - jnp.where(chosen_a, m_new - shift_a, POS_INF)).astype(BF16)
                pb = jnp.exp2(s_ref[mb:, cols] - jnp.where(chosen_b, m_new - shift_b, POS_INF)).astype(BF16)
                vt = jnp.concatenate([vt_ref[h, j0], vt_ref[h, j1]], axis=1)
                acc_ref[h] = alpha * acc_ref[h] + jnp.dot(vt, jnp.concatenate([pa, pb], axis=0),
                                                          preferred_element_type=F32)
                m_ref[h] = m_new
            stage_scores(s_next, last)
            return carry

        lax.fori_loop(0, nsteps, step, 0)
        outs = []
        for h in range(2):
            acc = acc_ref[h]
            outs.append(acc[:HEAD_DIM] * (1.0 / acc[HEAD_DIM:HEAD_DIM + 1]))
        ot_ref[i] = jnp.concatenate(outs, axis=0)
        return carry0

    lax.fori_loop(0, nb, query_block, 0)

    def emit(u, carry):
        for sub in range(GATE_UNROLL):
            i = u * GATE_UNROLL + sub
            o_ref[pl.ds(pl.multiple_of(i * mb, mb), mb), :] = ot_ref[i].T
        return carry

    lax.fori_loop(0, nb // GATE_UNROLL, emit, 0)


def attn_b(proj, *, q_col, k_col, v_col):
    B, S, _ = proj.shape
    mb = MOBA_BLOCK
    assert S % mb == 0
    nb = S // mb
    assert MOBA_TOPK < nb and nb % SUBLANES == 0 and nb % GATE_UNROLL == 0
    pairs = B_HEADS // 2
    kern = functools.partial(_attn_b_kernel, nb=nb)
    return pl.pallas_call(
        kern,
        grid=(B, pairs),
        in_specs=[_smem_spec(),
                  pl.BlockSpec((None, S, LANES), lambda b, p: (b, 0, q_col // LANES + p)),
                  pl.BlockSpec((None, S, LANES), lambda b, p: (b, 0, k_col // LANES + p)),
                  pl.BlockSpec((None, S, LANES), lambda b, p: (b, 0, v_col // LANES + p))],
        out_specs=pl.BlockSpec((None, S, LANES), lambda b, p: (b, 0, p)),
        out_shape=jax.ShapeDtypeStruct((B, S, B_HEADS * HEAD_DIM), F32),
        scratch_shapes=[pltpu.VMEM((nb, mb, LANES), BF16),
                        pltpu.VMEM((2, nb, VT_ROWS, mb), BF16),
                        pltpu.VMEM((nb, LANES), F32),
                        pltpu.VMEM((2 * nb, 3 * LANES), BF16),
                        pltpu.VMEM((2, 2, mb, mb), F32),
                        pltpu.VMEM((2, 2 * mb, LANES), BF16),
                        pltpu.VMEM((nb, 2, nb + SUBLANES, mb), F32),
                        pltpu.VMEM((2, VT_ROWS, mb), F32),
                        pltpu.VMEM((2, 1, mb), F32),
                        pltpu.VMEM((2 * mb, 2 * mb), F32),
                        pltpu.VMEM((2, 2, 1, mb), F32),
                        pltpu.VMEM((nb, LANES, mb), F32)],
        compiler_params=_params("parallel", "parallel"),
        name="attn_moba",
    )(_alibi_slopes(B_HEADS), proj, proj, proj)


def _attn_c_kernel(slopes_ref, q_ref, k_ref, v_ref, o_ref,
                   q4_ref, k4_ref, v4_ref, kb_ref, vt_ref, bias_ref, orun_ref, lse_ref, *, seq, group):
    n = C_STEPS
    hp = pl.program_id(1)
    ntile = seq // n
    quarter = seq // 4
    slab = n // 4
    lo_half = lax.broadcasted_iota(jnp.int32, (n, LANES), 1) < HEAD_DIM
    ones = jnp.ones((BF16_ROWS, 2 * n), BF16)
    kk = lax.broadcasted_iota(jnp.int32, (2 * n, n), 0)
    qq = lax.broadcasted_iota(jnp.int32, (2 * n, n), 1)

    def reorder(c, carry):
        r4 = c // (ntile // 4)
        src = pl.ds(r4 + 4 * n * (c % (ntile // 4)), n, stride=4)
        dst = pl.ds(pl.multiple_of(c * n, n), n)
        q4_ref[dst, :] = q_ref[src, :]
        k4_ref[dst, :] = k_ref[src, :]
        v4_ref[dst, :] = v_ref[src, :]
        return carry

    lax.fori_loop(0, ntile, reorder, 0)

    for bi, (w, d) in enumerate(C_PATTERNS):
        nblk = seq // w
        first, last = bi == 0, bi == len(C_PATTERNS) - 1
        assert d in (1, 4, 16)

        def load_tile(ref, t, d=d):
            if d == 4:
                return ref[pl.ds(pl.multiple_of(t * n, n), n), :]
            if d == 16:
                r16, b16 = t // 2, t % 2
                return ref[pl.ds((r16 % 4) * quarter + b16 * (4 * n) + r16 // 4, n, stride=4), :]
            return jnp.concatenate([ref[pl.ds(pl.multiple_of(r4 * quarter + t * slab, slab), slab), :]
                                    for r4 in range(4)], axis=0)

        def store_tile(ref, t, val, d=d):
            if d == 4:
                ref[pl.ds(pl.multiple_of(t * n, n), n), :] = val
            elif d == 16:
                r16, b16 = t // 2, t % 2
                ref[pl.ds((r16 % 4) * quarter + b16 * (4 * n) + r16 // 4, n, stride=4), :] = val
            else:
                for r4 in range(4):
                    ref[pl.ds(pl.multiple_of(r4 * quarter + t * slab, slab), slab), :] = val[r4 * slab:(r4 + 1) * slab]

        if d == 1:
            member = lambda x: 4 * (x % slab) + x // slab
            steps = member(qq) - member(kk % n) + jnp.where(kk < n, n, 0)
        else:
            steps = qq + n - kk
        valid = (steps >= 0) & (steps <= n)
        stepf = steps.astype(F32)
        for h in range(2):
            full = jnp.where(valid, (-slopes_ref[2 * hp + h] * (LOG2E * d)) * stepf, NEG_INF)
            bias_ref[h, 1] = full
            bias_ref[h, 0] = jnp.where(kk >= n, full, NEG_INF)

        def prep(u, carry, load_tile=load_tile):
            for sub in range(group):
                t = u * group + sub
                kb_ref[t] = load_tile(k4_ref, t).astype(BF16)
                vt_ref[t] = load_tile(v4_ref, t).T.astype(BF16)
            return carry

        lax.fori_loop(0, ntile // group, prep, 0)

        def tiles(u, carry, load_tile=load_tile, store_tile=store_tile, nblk=nblk, first=first):
            ts = [u * group + sub for sub in range(group)]
            prevs, variants, scores = [], [], []
            for t in ts:
                has_prev = (t % nblk) > 0
                tp = jnp.where(has_prev, t - 1, t)
                prevs.append(tp)
                variants.append(jnp.where(has_prev, 1, 0))
                q = load_tile(q4_ref, t) * (HEAD_DIM ** -0.5 * LOG2E)
                q_both = jnp.concatenate([jnp.where(lo_half, q, 0.0), jnp.where(lo_half, 0.0, q)], axis=0)
                kcat = jnp.concatenate([kb_ref[tp], kb_ref[t]], axis=0)
                scores.append(_dot_nt(kcat, q_both.astype(BF16)))
            anchor = sum(_zero_after(sc[0:1, :n]) for sc in scores[1:])
            results = []
            for g, t in enumerate(ts):
                vtcat = jnp.concatenate([vt_ref[prevs[g]], vt_ref[t]], axis=1)
                o_rows, lse_rows = [], []
                for h in range(2):
                    s = scores[g][:, h * n:(h + 1) * n] + bias_ref[h, variants[g]]
                    m = jnp.max(s, axis=0, keepdims=True)
                    if g == 0:
                        m = m + anchor
                    p = jnp.exp2(s - m).astype(BF16)
                    vaug = jnp.concatenate([vtcat[h * HEAD_DIM:(h + 1) * HEAD_DIM], ones], axis=0)
                    acc = jnp.dot(vaug, p, preferred_element_type=F32)
                    l = acc[HEAD_DIM:HEAD_DIM + 1]
                    o_rows.append(acc[:HEAD_DIM] * (1.0 / l))
                    lse_rows.append(jnp.broadcast_to(m + jnp.log2(l), (HEAD_DIM, n)))
                results.append((jnp.concatenate(o_rows, axis=0), jnp.concatenate(lse_rows, axis=0)))
            for g, t in enumerate(ts):
                o_t = results[g][0].T
                lse_t = results[g][1].T
                if first:
                    store_tile(orun_ref, t, o_t)
                    store_tile(lse_ref, t, lse_t)
                else:
                    lse_old = load_tile(lse_ref, t)
                    top = jnp.maximum(lse_old, lse_t)
                    w_old = jnp.exp2(lse_old - top)
                    w_t = jnp.exp2(lse_t - top)
                    den = w_old + w_t
                    store_tile(orun_ref, t, (w_old * load_tile(orun_ref, t) + w_t * o_t) * (1.0 / den))
                    if not last:
                        store_tile(lse_ref, t, top + jnp.log2(den))
            return carry

        lax.fori_loop(0, ntile // group, tiles, 0)

    def restore(c, carry):
        r4 = c // (ntile // 4)
        o_ref[pl.ds(r4 + 4 * n * (c % (ntile // 4)), n, stride=4), :] = orun_ref[pl.ds(pl.multiple_of(c * n, n), n), :]
        return carry

    lax.fori_loop(0, ntile, restore, 0)


def attn_c(proj, *, q_col, k_col, v_col):
    B, S, _ = proj.shape
    n = C_STEPS
    assert all(S % w == 0 and w // d == n for w, d in C_PATTERNS)
    assert (S // n) % C_GROUP == 0 and S % (16 * n) == 0
    pairs = C_HEADS // 2
    kern = functools.partial(_attn_c_kernel, seq=S, group=C_GROUP)
    return pl.pallas_call(
        kern,
        grid=(B, pairs),
        in_specs=[_smem_spec(),
                  pl.BlockSpec((None, S, LANES), lambda b, p: (b, 0, q_col // LANES + p)),
                  pl.BlockSpec((None, S, LANES), lambda b, p: (b, 0, k_col // LANES + p)),
                  pl.BlockSpec((None, S, LANES), lambda b, p: (b, 0, v_col // LANES + p))],
        out_specs=pl.BlockSpec((None, S, LANES), lambda b, p: (b, 0, p)),
        out_shape=jax.ShapeDtypeStruct((B, S, C_HEADS * HEAD_DIM), F32),
        scratch_shapes=[pltpu.VMEM((S, LANES), F32),
                        pltpu.VMEM((S, LANES), F32),
                        pltpu.VMEM((S, LANES), F32),
                        pltpu.VMEM((S // n, n, LANES), BF16),
                        pltpu.VMEM((S // n, LANES, n), BF16),
                        pltpu.VMEM((2, 2, 2 * n, n), F32),
                        pltpu.VMEM((S, LANES), F32),
                        pltpu.VMEM((S, LANES), F32)],
        compiler_params=_params("parallel", "parallel"),
        name="attn_dilated",
    )(_alibi_slopes(C_HEADS), proj, proj, proj)


def _mix_out_kernel(oa_ref, ob_ref, oc_ref, ga_ref, gb_ref, gc_ref, h_ref, w_ref, o_ref):
    y = jnp.concatenate([_rms(oa_ref[...], ga_ref[...]).astype(BF16),
                         _rms(ob_ref[...], gb_ref[...]).astype(BF16),
                         _rms(oc_ref[...], gc_ref[...]).astype(BF16)], axis=-1)
    o_ref[...] = h_ref[...] + jnp.dot(y, w_ref[...], preferred_element_type=F32)


def mix_out(oa, ob, oc, ga, gb, gc, h, w, layer, *, tm):
    M, D = h.shape
    wa, wb, wc = oa.shape[1], ob.shape[1], oc.shape[1]
    K = wa + wb + wc
    assert w.shape[1:] == (K, D) and M % tm == 0
    row = lambda width: pl.BlockSpec((tm, width), lambda i: (i, 0))
    gain = lambda width: pl.BlockSpec((1, width), lambda i: (0, 0))
    return pl.pallas_call(
        _mix_out_kernel,
        grid=(M // tm,),
        in_specs=[row(wa), row(wb), row(wc), gain(wa), gain(wb), gain(wc), row(D),
                  pl.BlockSpec((None, K, D), lambda i: (layer, 0, 0))],
        out_specs=row(D),
        out_shape=jax.ShapeDtypeStruct((M, D), F32),
        compiler_params=_params("parallel"),
        name="mix_out",
    )(oa, ob, oc, ga, gb, gc, h, w)


def _xattn_kernel(h_ref, g_ref, wq_ref, kv_ref, wo_ref, o_ref):
    xw = X_HEADS * X_HEAD_DIM
    h = h_ref[...]
    xn = _rms(h, g_ref[...]).astype(BF16)
    q = jnp.dot(xn, wq_ref[...], preferred_element_type=F32) * (X_HEAD_DIM ** -0.5)
    heads = []
    for hd in range(X_HEADS):
        lanes = slice(hd * X_HEAD_DIM, (hd + 1) * X_HEAD_DIM)
        s = _dot_nt(q[:, lanes].astype(BF16), kv_ref[:, lanes])
        m = jnp.max(s, axis=-1, keepdims=True)
        p = jnp.exp(s - m)
        l = jnp.sum(p, axis=-1, keepdims=True)
        v = kv_ref[:, xw + hd * X_HEAD_DIM:xw + (hd + 1) * X_HEAD_DIM]
        heads.append(jnp.dot(p.astype(BF16), v, preferred_element_type=F32) / l)
    o = jnp.concatenate(heads, axis=-1).astype(BF16)
    o_ref[...] = h + jnp.dot(o, wo_ref[...], preferred_element_type=F32)


def xattn(h, g, wq, kv, wo, layer, *, tq):
    B, S, D = h.shape
    Mm = kv.shape[1]
    xw = X_HEADS * X_HEAD_DIM
    assert S % tq == 0
    return pl.pallas_call(
        _xattn_kernel,
        grid=(B, S // tq),
        in_specs=[pl.BlockSpec((None, tq, D), lambda b, i: (b, i, 0)),
                  pl.BlockSpec((1, D), lambda b, i: (0, 0)),
                  pl.BlockSpec((None, D, xw), lambda b, i: (layer, 0, 0)),
                  pl.BlockSpec((None, Mm, 2 * xw), lambda b, i: (b, 0, 0)),
                  pl.BlockSpec((None, xw, D), lambda b, i: (layer, 0, 0))],
        out_specs=pl.BlockSpec((None, tq, D), lambda b, i: (b, i, 0)),
        out_shape=jax.ShapeDtypeStruct((B, S, D), F32),
        compiler_params=_params("parallel", "parallel"),
        name="xattn",
    )(h, g, wq, kv, wo)


def _ffn_kernel(h_ref, g_ref, wg_ref, wu_ref, wd_ref, gf_ref, o_ref, xn_ref, *, final_norm):
    f = pl.program_id(1)

    @pl.when(f == 0)
    def _():
        h = h_ref[...]
        xn_ref[...] = _rms(h, g_ref[...]).astype(BF16)
        o_ref[...] = h

    xn = xn_ref[...]
    gate = jnp.dot(xn, wg_ref[...], preferred_element_type=F32)
    up = jnp.dot(xn, wu_ref[...], preferred_element_type=F32)
    act = (gate / (1.0 + jnp.exp(-gate))) * up
    o_ref[...] += jnp.dot(act.astype(BF16), wd_ref[...], preferred_element_type=F32)

    if final_norm:
        @pl.when(f == pl.num_programs(1) - 1)
        def _():
            o_ref[...] = _rms(o_ref[...], gf_ref[...])


def ffn(h, g, wg, wu, wd, gf, layer, *, tm, tf, final_norm):
    M, D = h.shape
    Fd = wg.shape[2]
    assert M % tm == 0 and Fd % tf == 0
    kern = functools.partial(_ffn_kernel, final_norm=final_norm)
    return pl.pallas_call(
        kern,
        grid=(M // tm, Fd // tf),
        in_specs=[pl.BlockSpec((tm, D), lambda i, f: (i, 0)),
                  pl.BlockSpec((1, D), lambda i, f: (0, 0)),
                  pl.BlockSpec((None, D, tf), lambda i, f: (layer, 0, f)),
                  pl.BlockSpec((None, D, tf), lambda i, f: (layer, 0, f)),
                  pl.BlockSpec((None, tf, D), lambda i, f: (layer, f, 0)),
                  pl.BlockSpec((1, D), lambda i, f: (0, 0))],
        out_specs=pl.BlockSpec((tm, D), lambda i, f: (i, 0)),
        out_shape=jax.ShapeDtypeStruct((M, D), F32),
        scratch_shapes=[pltpu.VMEM((tm, D), BF16)],
        compiler_params=_params("parallel", "arbitrary"),
        name="ffn",
    )(h, g, wg, wu, wd, gf)


def kernel(x, mem, norm_mix, w_in, sinks, gain_a, gain_b, gain_c, w_out, norm_xattn, norm_mem,
           wq_x, wk_x, wv_x, wo_x, norm_ffn, w_gate, w_up, w_down, norm_final):
    B, S, D = x.shape
    depth = w_in.shape[0]
    Mm = mem.shape[1]
    qa_w, kva_w = A_HEADS * HEAD_DIM, A_KV_HEADS * HEAD_DIM
    b_w, c_w = B_HEADS * HEAD_DIM, C_HEADS * HEAD_DIM
    widths = (qa_w, kva_w, kva_w, b_w, b_w, b_w, c_w, c_w, c_w)
    cols = [int(c) for c in np.concatenate([[0], np.cumsum(widths)[:-1]])]
    in_w = sum(widths)

    row = lambda v: v.reshape(1, -1)
    h = x.reshape(B * S, D)
    mem2 = mem.reshape(B * Mm, D)
    w_in_b, w_out_b = w_in.astype(BF16), w_out.astype(BF16)
    w_kv_b = jnp.concatenate([wk_x, wv_x], axis=2).astype(BF16)
    wq_b, wo_b = wq_x.astype(BF16), wo_x.astype(BF16)
    wg_b, wu_b, wd_b = w_gate.astype(BF16), w_up.astype(BF16), w_down.astype(BF16)
    for l in range(depth):
        proj = norm_matmul(h, row(norm_mix[l]), w_in_b, l, tm=1024, tn=1792, out_dtype=F32)
        proj = proj.reshape(B, S, in_w)
        oa = attn_a(proj, sinks[l], q_col=cols[0], k_col=cols[1], v_col=cols[2], tq=1024, group=4)
        ob = attn_b(proj, q_col=cols[3], k_col=cols[4], v_col=cols[5])
        oc = attn_c(proj, q_col=cols[6], k_col=cols[7], v_col=cols[8])
        h = mix_out(oa.reshape(B * S, qa_w), ob.reshape(B * S, b_w), oc.reshape(B * S, c_w),
                    row(gain_a[l]), row(gain_b[l]), row(gain_c[l]), h, w_out_b, l, tm=512)
        kv = norm_matmul(mem2, row(norm_mem[l]), w_kv_b, l, tm=512, tn=w_kv_b.shape[2], out_dtype=BF16)
        h = xattn(h.reshape(B, S, D), row(norm_xattn[l]), wq_b, kv.reshape(B, Mm, -1), wo_b, l,
                  tq=1024).reshape(B * S, D)
        h = ffn(h, row(norm_ffn[l]), wg_b, wu_b, wd_b, row(norm_final), l, tm=1024, tf=512,
                final_norm=(l == depth - 1))
    return h.reshape(B, S, D)
```

```python
import functools

import numpy as np
import jax
import jax.numpy as jnp
from jax import lax
from jax.experimental import pallas as pl
from jax.experimental.pallas import tpu as pltpu

F32 = jnp.float32
BF16 = jnp.bfloat16
NEG_INF = float("-inf")
POS_INF = float("inf")
LOG2E = 1.4426950408889634

EPS = 1e-5
HEAD_DIM = 64
LANES = 128
BF16_ROWS = 16
A_HEADS, A_KV_HEADS, A_WINDOW, A_BLOCK = 8, 2, 128, 128
B_HEADS, MOBA_BLOCK, MOBA_TOPK = 8, 256, 3
C_HEADS = 16
C_PATTERNS = ((128, 1), (512, 4), (2048, 16))
C_STEPS = 128
C_GROUP = 32
GATE_UNROLL = 4
SUBLANES = 8
X_HEADS, X_HEAD_DIM = 4, 128
VMEM_LIMIT = 56 * 1024 * 1024
VT_ROWS = HEAD_DIM + BF16_ROWS


def _alibi_slopes(n):
    return jnp.asarray(2.0 ** (-8.0 * np.arange(1, n + 1) / n), F32)


def _params(*sem):
    return pltpu.CompilerParams(dimension_semantics=sem, vmem_limit_bytes=VMEM_LIMIT)


def _rms(x, g):
    return x * lax.rsqrt(jnp.mean(x * x, axis=-1, keepdims=True) + EPS) * g


def _dot_nt(a, b, precision=None):
    return lax.dot_general(a, b, (((1,), (1,)), ((), ())), precision=precision,
                           preferred_element_type=F32)


def _smem_spec():
    return pl.BlockSpec(memory_space=pltpu.SMEM)


def _zero_after(x):
    bits = pltpu.bitcast(x, jnp.uint32)
    return pltpu.bitcast(lax.shift_right_logical(bits, jnp.uint32(32)), F32)


def _norm_matmul_kernel(x_ref, g_ref, w_ref, o_ref, xn_ref):
    @pl.when(pl.program_id(1) == 0)
    def _():
        xn_ref[...] = _rms(x_ref[...], g_ref[...]).astype(BF16)

    o_ref[...] = jnp.dot(xn_ref[...], w_ref[...], preferred_element_type=F32).astype(o_ref.dtype)


def norm_matmul(x, g, w, layer, *, tm, tn, out_dtype):
    M, D = x.shape
    N = w.shape[2]
    assert M % tm == 0 and N % tn == 0
    return pl.pallas_call(
        _norm_matmul_kernel,
        grid=(M // tm, N // tn),
        in_specs=[pl.BlockSpec((tm, D), lambda i, j: (i, 0)),
                  pl.BlockSpec((1, D), lambda i, j: (0, 0)),
                  pl.BlockSpec((None, D, tn), lambda i, j: (layer, 0, j))],
        out_specs=pl.BlockSpec((tm, tn), lambda i, j: (i, j)),
        out_shape=jax.ShapeDtypeStruct((M, N), out_dtype),
        scratch_shapes=[pltpu.VMEM((tm, D), BF16)],
        compiler_params=_params("parallel", "arbitrary"),
        name="norm_matmul",
    )(x, g, w)


def _attn_a_kernel(slopes_ref, sinks_ref, q_ref, k_ref, v_ref, o_ref, kb_ref, vt_ref, bias_ref,
                   *, seq, n_sub, group):
    n = A_BLOCK
    qi = pl.program_id(1)
    heads_per_kv = A_HEADS // A_KV_HEADS
    lo_half = lax.broadcasted_iota(jnp.int32, (n, LANES), 1) < HEAD_DIM
    ones = jnp.ones((BF16_ROWS, 2 * n), BF16)

    @pl.when(qi == 0)
    def _():
        kk = lax.broadcasted_iota(jnp.int32, (2 * n, n), 0)
        qq = lax.broadcasted_iota(jnp.int32, (2 * n, n), 1)
        dist = qq + n - kk
        valid = (dist >= 0) & (dist < A_WINDOW)
        distf = dist.astype(F32)
        for j in range(A_HEADS):
            full = jnp.where(valid, (-slopes_ref[j] * LOG2E) * distf, NEG_INF)
            bias_ref[j, 1] = full
            bias_ref[j, 0] = jnp.where(kk >= n, full, NEG_INF)

        def prep(u, carry):
            for sub in range(4):
                rows = pl.ds(pl.multiple_of((4 * u + sub) * n, n), n)
                kb_ref[4 * u + sub] = k_ref[rows, :].astype(BF16)
                vt_ref[4 * u + sub] = v_ref[rows, :].T.astype(BF16)
            return carry

        lax.fori_loop(0, seq // (4 * n), prep, 0)

    def blocks(u, carry):
        subs = [u * group + g for g in range(group)]
        ts, prevs, variants, scores = [], [], [], []
        for sb in subs:
            t = qi * n_sub + sb
            ts.append(t)
            prevs.append(jnp.maximum(t - 1, 0))
            variants.append(jnp.where(t > 0, 1, 0))
            rows = pl.ds(pl.multiple_of(sb * n, n), n)
            stack = []
            for j in range(A_HEADS):
                qj = q_ref[rows, (j // 2) * LANES:(j // 2 + 1) * LANES] * (HEAD_DIM ** -0.5 * LOG2E)
                kv = j // heads_per_kv
                if j % 2 != kv:
                    qj = pltpu.roll(qj, HEAD_DIM, axis=1)
                stack.append(jnp.where(lo_half if kv == 0 else ~lo_half, qj, 0.0))
            q_all = jnp.concatenate(stack, axis=0).astype(BF16)
            kcat = jnp.concatenate([kb_ref[prevs[-1]], kb_ref[t]], axis=0)
            scores.append(_dot_nt(kcat, q_all))
        anchor = sum(_zero_after(sc[0:1, :n]) for sc in scores[1:]) if group > 1 else None
        for g, sb in enumerate(subs):
            rows = pl.ds(pl.multiple_of(sb * n, n), n)
            vtcat = jnp.concatenate([vt_ref[prevs[g]], vt_ref[ts[g]]], axis=1)
            outs = []
            for kv in range(A_KV_HEADS):
                probs, maxes = [], []
                for j in range(kv * heads_per_kv, (kv + 1) * heads_per_kv):
                    s = scores[g][:, j * n:(j + 1) * n] + bias_ref[j, variants[g]]
                    m = jnp.maximum(jnp.max(s, axis=0, keepdims=True), sinks_ref[j] * LOG2E)
                    if g == 0 and j == 0 and anchor is not None:
                        m = m + anchor
                    probs.append(jnp.exp2(s - m).astype(BF16))
                    maxes.append(m)
                vaug = jnp.concatenate([vtcat[kv * HEAD_DIM:(kv + 1) * HEAD_DIM], ones], axis=0)
                acc = jnp.dot(vaug, jnp.concatenate(probs, axis=1), preferred_element_type=F32)
                for c, j in enumerate(range(kv * heads_per_kv, (kv + 1) * heads_per_kv)):
                    cols = slice(c * n, (c + 1) * n)
                    denom = acc[HEAD_DIM:HEAD_DIM + 1, cols] + jnp.exp2(sinks_ref[j] * LOG2E - maxes[c])
                    outs.append(acc[:HEAD_DIM, cols] * (1.0 / denom))
            for pair in range(A_HEADS // 2):
                o_ref[rows, pair * LANES:(pair + 1) * LANES] = jnp.concatenate(outs[2 * pair:2 * pair + 2], axis=0).T
        return carry

    lax.fori_loop(0, n_sub // group, blocks, 0)


def attn_a(proj, sinks, *, q_col, k_col, v_col, tq, group):
    B, S, _ = proj.shape
    n = A_BLOCK
    qw = A_HEADS * HEAD_DIM
    assert S % tq == 0 and tq % (n * group) == 0 and S % (4 * n) == 0
    assert q_col % qw == 0 and k_col % LANES == 0 and v_col % LANES == 0
    assert A_WINDOW == n and (A_HEADS // A_KV_HEADS) % 2 == 0
    kern = functools.partial(_attn_a_kernel, seq=S, n_sub=tq // n, group=group)
    return pl.pallas_call(
        kern,
        grid=(B, S // tq),
        in_specs=[_smem_spec(), _smem_spec(),
                  pl.BlockSpec((None, tq, qw), lambda b, i: (b, i, q_col // qw)),
                  pl.BlockSpec((None, S, LANES), lambda b, i: (b, 0, k_col // LANES)),
                  pl.BlockSpec((None, S, LANES), lambda b, i: (b, 0, v_col // LANES))],
        out_specs=pl.BlockSpec((None, tq, qw), lambda b, i: (b, i, 0)),
        out_shape=jax.ShapeDtypeStruct((B, S, qw), F32),
        scratch_shapes=[pltpu.VMEM((S // n, n, LANES), BF16),
                        pltpu.VMEM((S // n, LANES, n), BF16),
                        pltpu.VMEM((A_HEADS, 2, 2 * n, n), F32)],
        compiler_params=_params("parallel", "arbitrary"),
        name="attn_swa",
    )(_alibi_slopes(A_HEADS), sinks, proj, proj, proj)


def _attn_b_kernel(slopes_ref, q_ref, k_ref, v_ref, o_ref,
                   kb_ref, vt_ref, kmean_ref, kcat_ref, bias_ref, qs_ref, sel_ref, acc_ref, m_ref, s_ref,
                   cmax_ref, ot_ref, *, nb):
    mb = MOBA_BLOCK
    hp = pl.program_id(1)
    slope2 = [slopes_ref[2 * hp + h] * LOG2E for h in range(2)]

    def prepare():
        kpos = lax.broadcasted_iota(jnp.int32, (mb, mb), 0)
        qpos = lax.broadcasted_iota(jnp.int32, (mb, mb), 1)
        relf = (qpos - kpos).astype(F32)
        for h in range(2):
            alibi = (-slope2[h]) * relf
            bias_ref[h, 0] = alibi
            bias_ref[h, 1] = jnp.where(qpos >= kpos, alibi, NEG_INF)
        ones = jnp.ones((BF16_ROWS, mb), BF16)

        def prep(n, carry):
            rows = pl.ds(pl.multiple_of(n * mb, mb), mb)
            kblk = k_ref[rows, :]
            kb_ref[n] = kblk.astype(BF16)
            kmean_ref[pl.ds(n, 1), :] = jnp.mean(kblk, axis=0, keepdims=True)
            vt = v_ref[rows, :].T
            for h in range(2):
                vt_ref[h, n, :HEAD_DIM, :] = vt[h * HEAD_DIM:(h + 1) * HEAD_DIM].astype(BF16)
                vt_ref[h, n, HEAD_DIM:, :] = ones
            return carry

        lax.fori_loop(0, nb, prep, 0)
        km = kmean_ref[...]
        lo_lanes = lax.broadcasted_iota(jnp.int32, (nb, LANES), 1) < HEAD_DIM
        for h in range(2):
            kmh = jnp.where(lo_lanes if h == 0 else ~lo_lanes, km, 0.0)
            hi = kmh.astype(BF16)
            lo = (kmh - hi.astype(F32)).astype(BF16)
            kcat_ref[h * nb:(h + 1) * nb, :] = jnp.concatenate([hi, hi, lo], axis=1)

        blk = lax.broadcasted_iota(jnp.int32, (nb, mb), 0)

        def select(u, carry):
            for sub in range(GATE_UNROLL):
                i = u * GATE_UNROLL + sub
                q = q_ref[pl.ds(pl.multiple_of(i * mb, mb), mb), :]
                past = blk < i
                q_hi = q.astype(BF16)
                q_lo = (q - q_hi.astype(F32)).astype(BF16)
                gate = _dot_nt(kcat_ref[...], jnp.concatenate([q_hi, q_lo, q_hi], axis=1))
                for h in range(2):
                    g = jnp.where(past, gate[h * nb:(h + 1) * nb], NEG_INF)
                    avail = blk >= 0
                    sel = jnp.where(blk == i, 1.0, 0.0)
                    for _ in range(MOBA_TOPK):
                        mx = jnp.max(jnp.where(avail, g, NEG_INF), axis=0, keepdims=True)
                        is_max = avail & (g == mx)
                        first = jnp.min(jnp.where(is_max, blk, nb), axis=0, keepdims=True)
                        pick = blk == first
                        sel = jnp.where(pick & past, 1.0, sel)
                        avail = avail & ~pick
                    sel_ref[i, h, :nb] = sel
                    sel_ref[i, h, nb:] = jnp.zeros((SUBLANES, mb), F32)
            return carry

        lax.fori_loop(0, nb // GATE_UNROLL, select, 0)

    prepare()
    lo_half = lax.broadcasted_iota(jnp.int32, (mb, LANES), 1) < HEAD_DIM

    def scaled_queries(i):
        qc = q_ref[pl.ds(pl.multiple_of(i * mb, mb), mb), :] * (HEAD_DIM ** -0.5 * LOG2E)
        return jnp.concatenate([jnp.where(lo_half, qc, 0.0), jnp.where(lo_half, 0.0, qc)], axis=0).astype(BF16)

    def stage_scores(raw, own_first):
        variant_a = jnp.where(own_first, 1, 0)
        for h in range(2):
            cols = slice(h * mb, (h + 1) * mb)
            za = raw[:mb, cols] + bias_ref[h, variant_a]
            zb = raw[mb:, cols] + bias_ref[h, 0]
            s_ref[:mb, cols] = za
            s_ref[mb:, cols] = zb
            cmax_ref[h, 0] = jnp.max(za, axis=0, keepdims=True)
            cmax_ref[h, 1] = jnp.max(zb, axis=0, keepdims=True)

    q_first = scaled_queries(0)
    qs_ref[0] = q_first
    stage_scores(_dot_nt(jnp.concatenate([kb_ref[0], kb_ref[0]], axis=0), q_first), True)

    def query_block(i, carry0):
        slot = i % 2
        i_next = jnp.minimum(i + 1, nb - 1)
        qs_ref[1 - slot] = scaled_queries(i_next)
        m_ref[...] = jnp.full(m_ref.shape, NEG_INF, F32)
        acc_ref[...] = jnp.zeros(acc_ref.shape, F32)
        nsteps = (i + 2) // 2

        def step(p, carry):
            first = p == 0
            j0 = jnp.where(first, i, 2 * p - 1)
            j1 = jnp.minimum(2 * p, nb - 1)
            row1 = jnp.where(2 * p < i, j1, nb)
            last = p == nsteps - 1
            n0 = jnp.where(last, i_next, jnp.minimum(2 * p + 1, nb - 1))
            n1 = jnp.where(last, 0, jnp.minimum(2 * p + 2, nb - 1))
            q_slot = jnp.where(last, 1 - slot, slot)
            s_next = _dot_nt(jnp.concatenate([kb_ref[n0], kb_ref[n1]], axis=0), qs_ref[q_slot])
            dist_a = ((i - j0) * mb).astype(F32)
            dist_b = ((i - j1) * mb).astype(F32)
            for h in range(2):
                cols = slice(h * mb, (h + 1) * mb)
                shift_a = -slope2[h] * dist_a
                shift_b = -slope2[h] * dist_b
                chosen_a = sel_ref[i, h, pl.ds(j0, 1), :] > 0.5
                chosen_b = sel_ref[i, h, pl.ds(row1, 1), :] > 0.5
                m_a = jnp.where(chosen_a, cmax_ref[h, 0] + shift_a, NEG_INF)
                m_b = jnp.where(chosen_b, cmax_ref[h, 1] + shift_b, NEG_INF)
                m_old = m_ref[h]
                m_new = jnp.maximum(m_old, jnp.maximum(m_a, m_b))
                alpha = jnp.exp2(m_old - m_new)
                pa = jnp.exp2(s_ref[:mb, cols] - jnp.where(chosen_a, m_new - shift_a, POS_INF)).astype(BF16)
                pb = jnp.exp2(s_ref[mb:, cols] - jnp.where(chosen_b, m_new - shift_b, POS_INF)).astype(BF16)
                vt = jnp.concatenate([vt_ref[h, j0], vt_ref[h, j1]], axis=1)
                acc_ref[h] = alpha * acc_ref[h] + jnp.dot(vt, jnp.concatenate([pa, pb], axis=0),
                                                          preferred_element_type=F32)
                m_ref[h] = m_new
            stage_scores(s_next, last)
            return carry

        lax.fori_loop(0, nsteps, step, 0)
        outs = []
        for h in range(2):
            acc = acc_ref[h]
            outs.append(acc[:HEAD_DIM] * (1.0 / acc[HEAD_DIM:HEAD_DIM + 1]))
        ot_ref[i] = jnp.concatenate(outs, axis=0)
        return carry0

    lax.fori_loop(0, nb, query_block, 0)

    def emit(u, carry):
        for sub in range(GATE_UNROLL):
            i = u * GATE_UNROLL + sub
            o_ref[pl.ds(pl.multiple_of(i * mb, mb), mb), :] = ot_ref[i].T
        return carry

    lax.fori_loop(0, nb // GATE_UNROLL, emit, 0)


def attn_b(proj, *, q_col, k_col, v_col):
    B, S, _ = proj.shape
    mb = MOBA_BLOCK
    assert S % mb == 0
    nb = S // mb
    assert MOBA_TOPK < nb and nb % SUBLANES == 0 and nb % GATE_UNROLL == 0
    pairs = B_HEADS // 2
    kern = functools.partial(_attn_b_kernel, nb=nb)
    return pl.pallas_call(
        kern,
        grid=(B, pairs),
        in_specs=[_smem_spec(),
                  pl.BlockSpec((None, S, LANES), lambda b, p: (b, 0, q_col // LANES + p)),
                  pl.BlockSpec((None, S, LANES), lambda b, p: (b, 0, k_col // LANES + p)),
                  pl.BlockSpec((None, S, LANES), lambda b, p: (b, 0, v_col // LANES + p))],
        out_specs=pl.BlockSpec((None, S, LANES), lambda b, p: (b, 0, p)),
        out_shape=jax.ShapeDtypeStruct((B, S, B_HEADS * HEAD_DIM), F32),
        scratch_shapes=[pltpu.VMEM((nb, mb, LANES), BF16),
                        pltpu.VMEM((2, nb, VT_ROWS, mb), BF16),
                        pltpu.VMEM((nb, LANES), F32),
                        pltpu.VMEM((2 * nb, 3 * LANES), BF16),
                        pltpu.VMEM((2, 2, mb, mb), F32),
                        pltpu.VMEM((2, 2 * mb, LANES), BF16),
                        pltpu.VMEM((nb, 2, nb + SUBLANES, mb), F32),
                        pltpu.VMEM((2, VT_ROWS, mb), F32),
                        pltpu.VMEM((2, 1, mb), F32),
                        pltpu.VMEM((2 * mb, 2 * mb), F32),
                        pltpu.VMEM((2, 2, 1, mb), F32),
                        pltpu.VMEM((nb, LANES, mb), F32)],
        compiler_params=_params("parallel", "parallel"),
        name="attn_moba",
    )(_alibi_slopes(B_HEADS), proj, proj, proj)


def _attn_c_kernel(slopes_ref, q_ref, k_ref, v_ref, o_ref,
                   q4_ref, k4_ref, v4_ref, kb_ref, vt_ref, bias_ref, orun_ref, lse_ref, *, seq, group):
    n = C_STEPS
    hp = pl.program_id(1)
    ntile = seq // n
    quarter = seq // 4
    slab = n // 4
    lo_half = lax.broadcasted_iota(jnp.int32, (n, LANES), 1) < HEAD_DIM
    ones = jnp.ones((BF16_ROWS, 2 * n), BF16)
    kk = lax.broadcasted_iota(jnp.int32, (2 * n, n), 0)
    qq = lax.broadcasted_iota(jnp.int32, (2 * n, n), 1)

    def reorder(c, carry):
        r4 = c // (ntile // 4)
        src = pl.ds(r4 + 4 * n * (c % (ntile // 4)), n, stride=4)
        dst = pl.ds(pl.multiple_of(c * n, n), n)
        q4_ref[dst, :] = q_ref[src, :]
        k4_ref[dst, :] = k_ref[src, :]
        v4_ref[dst, :] = v_ref[src, :]
        return carry

    lax.fori_loop(0, ntile, reorder, 0)

    for bi, (w, d) in enumerate(C_PATTERNS):
        nblk = seq // w
        first, last = bi == 0, bi == len(C_PATTERNS) - 1
        assert d in (1, 4, 16)

        def load_tile(ref, t, d=d):
            if d == 4:
                return ref[pl.ds(pl.multiple_of(t * n, n), n), :]
            if d == 16:
                r16, b16 = t // 2, t % 2
                return ref[pl.ds((r16 % 4) * quarter + b16 * (4 * n) + r16 // 4, n, stride=4), :]
            return jnp.concatenate([ref[pl.ds(pl.multiple_of(r4 * quarter + t * slab, slab), slab), :]
                                    for r4 in range(4)], axis=0)

        def store_tile(ref, t, val, d=d):
            if d == 4:
                ref[pl.ds(pl.multiple_of(t * n, n), n), :] = val
            elif d == 16:
                r16, b16 = t // 2, t % 2
                ref[pl.ds((r16 % 4) * quarter + b16 * (4 * n) + r16 // 4, n, stride=4), :] = val
            else:
                for r4 in range(4):
                    ref[pl.ds(pl.multiple_of(r4 * quarter + t * slab, slab), slab), :] = val[r4 * slab:(r4 + 1) * slab]

        if d == 1:
            member = lambda x: 4 * (x % slab) + x // slab
            steps = member(qq) - member(kk % n) + jnp.where(kk < n, n, 0)
        else:
            steps = qq + n - kk
        valid = (steps >= 0) & (steps <= n)
        stepf = steps.astype(F32)
        for h in range(2):
            full = jnp.where(valid, (-slopes_ref[2 * hp + h] * (LOG2E * d)) * stepf, NEG_INF)
            bias_ref[h, 1] = full
            bias_ref[h, 0] = jnp.where(kk >= n, full, NEG_INF)

        def prep(u, carry, load_tile=load_tile):
            for sub in range(group):
                t = u * group + sub
                kb_ref[t] = load_tile(k4_ref, t).astype(BF16)
                vt_ref[t] = load_tile(v4_ref, t).T.astype(BF16)
            return carry

        lax.fori_loop(0, ntile // group, prep, 0)

        def tiles(u, carry, load_tile=load_tile, store_tile=store_tile, nblk=nblk, first=first):
            ts = [u * group + sub for sub in range(group)]
            prevs, variants, scores = [], [], []
            for t in ts:
                has_prev = (t % nblk) > 0
                tp = jnp.where(has_prev, t - 1, t)
                prevs.append(tp)
                variants.append(jnp.where(has_prev, 1, 0))
                q = load_tile(q4_ref, t) * (HEAD_DIM ** -0.5 * LOG2E)
                q_both = jnp.concatenate([jnp.where(lo_half, q, 0.0), jnp.where(lo_half, 0.0, q)], axis=0)
                kcat = jnp.concatenate([kb_ref[tp], kb_ref[t]], axis=0)
                scores.append(_dot_nt(kcat, q_both.astype(BF16)))
            anchor = sum(_zero_after(sc[0:1, :n]) for sc in scores[1:])
            results = []
            for g, t in enumerate(ts):
                vtcat = jnp.concatenate([vt_ref[prevs[g]], vt_ref[t]], axis=1)
                o_rows, lse_rows = [], []
                for h in range(2):
                    s = scores[g][:, h * n:(h + 1) * n] + bias_ref[h, variants[g]]
                    m = jnp.max(s, axis=0, keepdims=True)
                    if g == 0:
                        m = m + anchor
                    p = jnp.exp2(s - m).astype(BF16)
                    vaug = jnp.concatenate([vtcat[h * HEAD_DIM:(h + 1) * HEAD_DIM], ones], axis=0)
                    acc = jnp.dot(vaug, p, preferred_element_type=F32)
                    l = acc[HEAD_DIM:HEAD_DIM + 1]
                    o_rows.append(acc[:HEAD_DIM] * (1.0 / l))
                    lse_rows.append(jnp.broadcast_to(m + jnp.log2(l), (HEAD_DIM, n)))
                results.append((jnp.concatenate(o_rows, axis=0), jnp.concatenate(lse_rows, axis=0)))
            for g, t in enumerate(ts):
                o_t = results[g][0].T
                lse_t = results[g][1].T
                if first:
                    store_tile(orun_ref, t, o_t)
                    store_tile(lse_ref, t, lse_t)
                else:
                    lse_old = load_tile(lse_ref, t)
                    top = jnp.maximum(lse_old, lse_t)
                    w_old = jnp.exp2(lse_old - top)
                    w_t = jnp.exp2(lse_t - top)
                    den = w_old + w_t
                    store_tile(orun_ref, t, (w_old * load_tile(orun_ref, t) + w_t * o_t) * (1.0 / den))
                    if not last:
                        store_tile(lse_ref, t, top + jnp.log2(den))
            return carry

        lax.fori_loop(0, ntile // group, tiles, 0)

    def restore(c, carry):
        r4 = c // (ntile // 4)
        o_ref[pl.ds(r4 + 4 * n * (c % (ntile // 4)), n, stride=4), :] = orun_ref[pl.ds(pl.multiple_of(c * n, n), n), :]
        return carry

    lax.fori_loop(0, ntile, restore, 0)


def attn_c(proj, *, q_col, k_col, v_col):
    B, S, _ = proj.shape
    n = C_STEPS
    assert all(S % w == 0 and w // d == n for w, d in C_PATTERNS)
    assert (S // n) % C_GROUP == 0 and S % (16 * n) == 0
    pairs = C_HEADS // 2
    kern = functools.partial(_attn_c_kernel, seq=S, group=C_GROUP)
    return pl.pallas_call(
        kern,
        grid=(B, pairs),
        in_specs=[_smem_spec(),
                  pl.BlockSpec((None, S, LANES), lambda b, p: (b, 0, q_col // LANES + p)),
                  pl.BlockSpec((None, S, LANES), lambda b, p: (b, 0, k_col // LANES + p)),
                  pl.BlockSpec((None, S, LANES), lambda b, p: (b, 0, v_col // LANES + p))],
        out_specs=pl.BlockSpec((None, S, LANES), lambda b, p: (b, 0, p)),
        out_shape=jax.ShapeDtypeStruct((B, S, C_HEADS * HEAD_DIM), F32),
        scratch_shapes=[pltpu.VMEM((S, LANES), F32),
                        pltpu.VMEM((S, LANES), F32),
                        pltpu.VMEM((S, LANES), F32),
                        pltpu.VMEM((S // n, n, LANES), BF16),
                        pltpu.VMEM((S // n, LANES, n), BF16),
                        pltpu.VMEM((2, 2, 2 * n, n), F32),
                        pltpu.VMEM((S, LANES), F32),
                        pltpu.VMEM((S, LANES), F32)],
        compiler_params=_params("parallel", "parallel"),
        name="attn_dilated",
    )(_alibi_slopes(C_HEADS), proj, proj, proj)


def _mix_out_kernel(oa_ref, ob_ref, oc_ref, ga_ref, gb_ref, gc_ref, h_ref, w_ref, o_ref):
    y = jnp.concatenate([_rms(oa_ref[...], ga_ref[...]).astype(BF16),
                         _rms(ob_ref[...], gb_ref[...]).astype(BF16),
                         _rms(oc_ref[...], gc_ref[...]).astype(BF16)], axis=-1)
    o_ref[...] = h_ref[...] + jnp.dot(y, w_ref[...], preferred_element_type=F32)


def mix_out(oa, ob, oc, ga, gb, gc, h, w, layer, *, tm):
    M, D = h.shape
    wa, wb, wc = oa.shape[1], ob.shape[1], oc.shape[1]
    K = wa + wb + wc
    assert w.shape[1:] == (K, D) and M % tm == 0
    row = lambda width: pl.BlockSpec((tm, width), lambda i: (i, 0))
    gain = lambda width: pl.BlockSpec((1, width), lambda i: (0, 0))
    return pl.pallas_call(
        _mix_out_kernel,
        grid=(M // tm,),
        in_specs=[row(wa), row(wb), row(wc), gain(wa), gain(wb), gain(wc), row(D),
                  pl.BlockSpec((None, K, D), lambda i: (layer, 0, 0))],
        out_specs=row(D),
        out_shape=jax.ShapeDtypeStruct((M, D), F32),
        compiler_params=_params("parallel"),
        name="mix_out",
    )(oa, ob, oc, ga, gb, gc, h, w)


def _xattn_kernel(h_ref, g_ref, wq_ref, kv_ref, wo_ref, o_ref):
    xw = X_HEADS * X_HEAD_DIM
    h = h_ref[...]
    xn = _rms(h, g_ref[...]).astype(BF16)
    q = jnp.dot(xn, wq_ref[...], preferred_element_type=F32) * (X_HEAD_DIM ** -0.5)
    heads = []
    for hd in range(X_HEADS):
        lanes = slice(hd * X_HEAD_DIM, (hd + 1) * X_HEAD_DIM)
        s = _dot_nt(q[:, lanes].astype(BF16), kv_ref[:, lanes])
        m = jnp.max(s, axis=-1, keepdims=True)
        p = jnp.exp(s - m)
        l = jnp.sum(p, axis=-1, keepdims=True)
        v = kv_ref[:, xw + hd * X_HEAD_DIM:xw + (hd + 1) * X_HEAD_DIM]
        heads.append(jnp.dot(p.astype(BF16), v, preferred_element_type=F32) / l)
    o = jnp.concatenate(heads, axis=-1).astype(BF16)
    o_ref[...] = h + jnp.dot(o, wo_ref[...], preferred_element_type=F32)


def xattn(h, g, wq, kv, wo, layer, *, tq):
    B, S, D = h.shape
    Mm = kv.shape[1]
    xw = X_HEADS * X_HEAD_DIM
    assert S % tq == 0
    return pl.pallas_call(
        _xattn_kernel,
        grid=(B, S // tq),
        in_specs=[pl.BlockSpec((None, tq, D), lambda b, i: (b, i, 0)),
                  pl.BlockSpec((1, D), lambda b, i: (0, 0)),
                  pl.BlockSpec((None, D, xw), lambda b, i: (layer, 0, 0)),
                  pl.BlockSpec((None, Mm, 2 * xw), lambda b, i: (b, 0, 0)),
                  pl.BlockSpec((None, xw, D), lambda b, i: (layer, 0, 0))],
        out_specs=pl.BlockSpec((None, tq, D), lambda b, i: (b, i, 0)),
        out_shape=jax.ShapeDtypeStruct((B, S, D), F32),
        compiler_params=_params("parallel", "parallel"),
        name="xattn",
    )(h, g, wq, kv, wo)


def _ffn_kernel(h_ref, g_ref, wg_ref, wu_ref, wd_ref, gf_ref, o_ref, xn_ref, *, final_norm):
    f = pl.program_id(1)

    @pl.when(f == 0)
    def _():
        h = h_ref[...]
        xn_ref[...] = _rms(h, g_ref[...]).astype(BF16)
        o_ref[...] = h

    xn = xn_ref[...]
    gate = jnp.dot(xn, wg_ref[...], preferred_element_type=F32)
    up = jnp.dot(xn, wu_ref[...], preferred_element_type=F32)
    act = (gate / (1.0 + jnp.exp(-gate))) * up
    o_ref[...] += jnp.dot(act.astype(BF16), wd_ref[...], preferred_element_type=F32)

    if final_norm:
        @pl.when(f == pl.num_programs(1) - 1)
        def _():
            o_ref[...] = _rms(o_ref[...], gf_ref[...])


def ffn(h, g, wg, wu, wd, gf, layer, *, tm, tf, final_norm):
    M, D = h.shape
    Fd = wg.shape[2]
    assert M % tm == 0 and Fd % tf == 0
    kern = functools.partial(_ffn_kernel, final_norm=final_norm)
    return pl.pallas_call(
        kern,
        grid=(M // tm, Fd // tf),
        in_specs=[pl.BlockSpec((tm, D), lambda i, f: (i, 0)),
                  pl.BlockSpec((1, D), lambda i, f: (0, 0)),
                  pl.BlockSpec((None, D, tf), lambda i, f: (layer, 0, f)),
                  pl.BlockSpec((None, D, tf), lambda i, f: (layer, 0, f)),
                  pl.BlockSpec((None, tf, D), lambda i, f: (layer, f, 0)),
                  pl.BlockSpec((1, D), lambda i, f: (0, 0))],
        out_specs=pl.BlockSpec((tm, D), lambda i, f: (i, 0)),
        out_shape=jax.ShapeDtypeStruct((M, D), F32),
        scratch_shapes=[pltpu.VMEM((tm, D), BF16)],
        compiler_params=_params("parallel", "arbitrary"),
        name="ffn",
    )(h, g, wg, wu, wd, gf)


def kernel(x, mem, norm_mix, w_in, sinks, gain_a, gain_b, gain_c, w_out, norm_xattn, norm_mem,
           wq_x, wk_x, wv_x, wo_x, norm_ffn, w_gate, w_up, w_down, norm_final):
    B, S, D = x.shape
    depth = w_in.shape[0]
    Mm = mem.shape[1]
    qa_w, kva_w = A_HEADS * HEAD_DIM, A_KV_HEADS * HEAD_DIM
    b_w, c_w = B_HEADS * HEAD_DIM, C_HEADS * HEAD_DIM
    widths = (qa_w, kva_w, kva_w, b_w, b_w, b_w, c_w, c_w, c_w)
    cols = [int(c) for c in np.concatenate([[0], np.cumsum(widths)[:-1]])]
    in_w = sum(widths)

    row = lambda v: v.reshape(1, -1)
    h = x.reshape(B * S, D)
    mem2 = mem.reshape(B * Mm, D)
    w_in_b, w_out_b = w_in.astype(BF16), w_out.astype(BF16)
    w_kv_b = jnp.concatenate([wk_x, wv_x], axis=2).astype(BF16)
    wq_b, wo_b = wq_x.astype(BF16), wo_x.astype(BF16)
    wg_b, wu_b, wd_b = w_gate.astype(BF16), w_up.astype(BF16), w_down.astype(BF16)
    for l in range(depth):
        proj = norm_matmul(h, row(norm_mix[l]), w_in_b, l, tm=1024, tn=1792, out_dtype=F32)
        proj = proj.reshape(B, S, in_w)
        oa = attn_a(proj, sinks[l], q_col=cols[0], k_col=cols[1], v_col=cols[2], tq=1024, group=4)
        ob = attn_b(proj, q_col=cols[3], k_col=cols[4], v_col=cols[5])
        oc = attn_c(proj, q_col=cols[6], k_col=cols[7], v_col=cols[8])
        h = mix_out(oa.reshape(B * S, qa_w), ob.reshape(B * S, b_w), oc.reshape(B * S, c_w),
                    row(gain_a[l]), row(gain_b[l]), row(gain_c[l]), h, w_out_b, l, tm=512)
        kv = norm_matmul(mem2, row(norm_mem[l]), w_kv_b, l, tm=512, tn=w_kv_b.shape[2], out_dtype=BF16)
        h = xattn(h.reshape(B, S, D), row(norm_xattn[l]), wq_b, kv.reshape(B, Mm, -1), wo_b, l,
                  tq=1024).reshape(B * S, D)
        h = ffn(h, row(norm_ffn[l]), wg_b, wu_b, wd_b, row(norm_final), l, tm=1024, tf=512,
                final_norm=(l == depth - 1))
    return h.reshape(B, S, D)
```

```python
import functools

import numpy as np
import jax
import jax.numpy as jnp
from jax import lax
from jax.experimental import pallas as pl
from jax.experimental.pallas import tpu as pltpu

F32 = jnp.float32
BF16 = jnp.bfloat16
NEG_INF = float("-inf")
POS_INF = float("inf")
LOG2E = 1.4426950408889634

EPS = 1e-5
HEAD_DIM = 64
LANES = 128
BF16_ROWS = 16
A_HEADS, A_KV_HEADS, A_WINDOW, A_BLOCK = 8, 2, 128, 128
B_HEADS, MOBA_BLOCK, MOBA_TOPK = 8, 256, 3
C_HEADS = 16
C_PATTERNS = ((128, 1), (512, 4), (2048, 16))
C_STEPS = 128
C_GROUP = 32
GATE_UNROLL = 4
SUBLANES = 8
X_HEADS, X_HEAD_DIM = 4, 128
VMEM_LIMIT = 56 * 1024 * 1024
VT_ROWS = HEAD_DIM + BF16_ROWS


def _alibi_slopes(n):
    return jnp.asarray(2.0 ** (-8.0 * np.arange(1, n + 1) / n), F32)


def _params(*sem):
    return pltpu.CompilerParams(dimension_semantics=sem, vmem_limit_bytes=VMEM_LIMIT)


def _rms(x, g):
    return x * lax.rsqrt(jnp.mean(x * x, axis=-1, keepdims=True) + EPS) * g


def _dot_nt(a, b, precision=None):
    return lax.dot_general(a, b, (((1,), (1,)), ((), ())), precision=precision,
                           preferred_element_type=F32)


def _smem_spec():
    return pl.BlockSpec(memory_space=pltpu.SMEM)


def _zero_after(x):
    bits = pltpu.bitcast(x, jnp.uint32)
    return pltpu.bitcast(lax.shift_right_logical(bits, jnp.uint32(32)), F32)


def _norm_matmul_kernel(x_ref, g_ref, w_ref, o_ref, xn_ref):
    @pl.when(pl.program_id(1) == 0)
    def _():
        xn_ref[...] = _rms(x_ref[...], g_ref[...]).astype(BF16)

    o_ref[...] = jnp.dot(xn_ref[...], w_ref[...], preferred_element_type=F32).astype(o_ref.dtype)


def norm_matmul(x, g, w, layer, *, tm, tn, out_dtype):
    M, D = x.shape
    N = w.shape[2]
    assert M % tm == 0 and N % tn == 0
    return pl.pallas_call(
        _norm_matmul_kernel,
        grid=(M // tm, N // tn),
        in_specs=[pl.BlockSpec((tm, D), lambda i, j: (i, 0)),
                  pl.BlockSpec((1, D), lambda i, j: (0, 0)),
                  pl.BlockSpec((None, D, tn), lambda i, j: (layer, 0, j))],
        out_specs=pl.BlockSpec((tm, tn), lambda i, j: (i, j)),
        out_shape=jax.ShapeDtypeStruct((M, N), out_dtype),
        scratch_shapes=[pltpu.VMEM((tm, D), BF16)],
        compiler_params=_params("parallel", "arbitrary"),
        name="norm_matmul",
    )(x, g, w)


def _norm_matmul_slab_kernel(x_ref, g_ref, w_ref, o_ref, xn_ref):
    @pl.when(pl.program_id(1) == 0)
    def _():
        xn_ref[...] = _rms(x_ref[...], g_ref[...]).astype(BF16)

    y = jnp.dot(xn_ref[...], w_ref[...], preferred_element_type=F32)
    for c in range(o_ref.shape[0]):
        o_ref[c] = y[:, c * LANES:(c + 1) * LANES]


def norm_matmul_slabs(x, g, w, layer, *, tm, tn):
    M, D = x.shape
    N = w.shape[2]
    assert M % tm == 0 and N % tn == 0 and tn % LANES == 0
    return pl.pallas_call(
        _norm_matmul_slab_kernel,
        grid=(M // tm, N // tn),
        in_specs=[pl.BlockSpec((tm, D), lambda i, j: (i, 0)),
                  pl.BlockSpec((1, D), lambda i, j: (0, 0)),
                  pl.BlockSpec((None, D, tn), lambda i, j: (layer, 0, j))],
        out_specs=pl.BlockSpec((tn // LANES, tm, LANES), lambda i, j: (j, i, 0)),
        out_shape=jax.ShapeDtypeStruct((N // LANES, M, LANES), F32),
        scratch_shapes=[pltpu.VMEM((tm, D), BF16)],
        compiler_params=_params("parallel", "arbitrary"),
        name="norm_matmul_slabs",
    )(x, g, w)


def _attn_a_kernel(slopes_ref, sinks_ref, q_ref, k_ref, v_ref, o_ref, kb_ref, vt_ref, bias_ref,
                   *, seq, n_sub, group):
    n = A_BLOCK
    qi = pl.program_id(1)
    heads_per_kv = A_HEADS // A_KV_HEADS
    lo_half = lax.broadcasted_iota(jnp.int32, (n, LANES), 1) < HEAD_DIM
    ones = jnp.ones((BF16_ROWS, 2 * n), BF16)

    @pl.when(qi == 0)
    def _():
        kk = lax.broadcasted_iota(jnp.int32, (2 * n, n), 0)
        qq = lax.broadcasted_iota(jnp.int32, (2 * n, n), 1)
        dist = qq + n - kk
        valid = (dist >= 0) & (dist < A_WINDOW)
        distf = dist.astype(F32)
        for j in range(A_HEADS):
            full = jnp.where(valid, (-slopes_ref[j] * LOG2E) * distf, NEG_INF)
            bias_ref[j, 1] = full
            bias_ref[j, 0] = jnp.where(kk >= n, full, NEG_INF)

        def prep(u, carry):
            for sub in range(4):
                rows = pl.ds(pl.multiple_of((4 * u + sub) * n, n), n)
                kb_ref[4 * u + sub] = k_ref[rows, :].astype(BF16)
                vt_ref[4 * u + sub] = v_ref[rows, :].T.astype(BF16)
            return carry

        lax.fori_loop(0, seq // (4 * n), prep, 0)

    def blocks(u, carry):
        subs = [u * group + g for g in range(group)]
        ts, prevs, variants, scores = [], [], [], []
        for sb in subs:
            t = qi * n_sub + sb
            ts.append(t)
            prevs.append(jnp.maximum(t - 1, 0))
            variants.append(jnp.where(t > 0, 1, 0))
            rows = pl.ds(pl.multiple_of(sb * n, n), n)
            stack = []
            for j in range(A_HEADS):
                qj = q_ref[j // 2, rows, :] * (HEAD_DIM ** -0.5 * LOG2E)
                kv = j // heads_per_kv
                if j % 2 != kv:
                    qj = pltpu.roll(qj, HEAD_DIM, axis=1)
                stack.append(jnp.where(lo_half if kv == 0 else ~lo_half, qj, 0.0))
            q_all = jnp.concatenate(stack, axis=0).astype(BF16)
            kcat = jnp.concatenate([kb_ref[prevs[-1]], kb_ref[t]], axis=0)
            scores.append(_dot_nt(kcat, q_all))
        anchor = sum(_zero_after(sc[0:1, :n]) for sc in scores[1:]) if group > 1 else None
        for g, sb in enumerate(subs):
            rows = pl.ds(pl.multiple_of(sb * n, n), n)
            vtcat = jnp.concatenate([vt_ref[prevs[g]], vt_ref[ts[g]]], axis=1)
            outs = []
            for kv in range(A_KV_HEADS):
                probs, maxes = [], []
                for j in range(kv * heads_per_kv, (kv + 1) * heads_per_kv):
                    s = scores[g][:, j * n:(j + 1) * n] + bias_ref[j, variants[g]]
                    m = jnp.maximum(jnp.max(s, axis=0, keepdims=True), sinks_ref[j] * LOG2E)
                    if g == 0 and j == 0 and anchor is not None:
                        m = m + anchor
                    probs.append(jnp.exp2(s - m).astype(BF16))
                    maxes.append(m)
                vaug = jnp.concatenate([vtcat[kv * HEAD_DIM:(kv + 1) * HEAD_DIM], ones], axis=0)
                acc = jnp.dot(vaug, jnp.concatenate(probs, axis=1), preferred_element_type=F32)
                for c, j in enumerate(range(kv * heads_per_kv, (kv + 1) * heads_per_kv)):
                    cols = slice(c * n, (c + 1) * n)
                    denom = acc[HEAD_DIM:HEAD_DIM + 1, cols] + jnp.exp2(sinks_ref[j] * LOG2E - maxes[c])
                    outs.append(acc[:HEAD_DIM, cols] * (1.0 / denom))
            for pair in range(A_HEADS // 2):
                o_ref[rows, pair * LANES:(pair + 1) * LANES] = jnp.concatenate(outs[2 * pair:2 * pair + 2], axis=0).T
        return carry

    lax.fori_loop(0, n_sub // group, blocks, 0)


def attn_a(proj, sinks, *, q_col, k_col, v_col, tq, group):
    _, B, S, _ = proj.shape
    n = A_BLOCK
    qw = A_HEADS * HEAD_DIM
    assert S % tq == 0 and tq % (n * group) == 0 and S % (4 * n) == 0
    assert q_col % qw == 0 and k_col % LANES == 0 and v_col % LANES == 0
    assert A_WINDOW == n and (A_HEADS // A_KV_HEADS) % 2 == 0
    kern = functools.partial(_attn_a_kernel, seq=S, n_sub=tq // n, group=group)
    return pl.pallas_call(
        kern,
        grid=(B, S // tq),
        in_specs=[_smem_spec(), _smem_spec(),
                  pl.BlockSpec((qw // LANES, None, tq, LANES), lambda b, i: (q_col // qw, b, i, 0)),
                  pl.BlockSpec((None, None, S, LANES), lambda b, i: (k_col // LANES, b, 0, 0)),
                  pl.BlockSpec((None, None, S, LANES), lambda b, i: (v_col // LANES, b, 0, 0))],
        out_specs=pl.BlockSpec((None, tq, qw), lambda b, i: (b, i, 0)),
        out_shape=jax.ShapeDtypeStruct((B, S, qw), F32),
        scratch_shapes=[pltpu.VMEM((S // n, n, LANES), BF16),
                        pltpu.VMEM((S // n, LANES, n), BF16),
                        pltpu.VMEM((A_HEADS, 2, 2 * n, n), F32)],
        compiler_params=_params("parallel", "arbitrary"),
        name="attn_swa",
    )(_alibi_slopes(A_HEADS), sinks, proj, proj, proj)


def _attn_b_kernel(slopes_ref, q_ref, k_ref, v_ref, o_ref,
                   kb_ref, vt_ref, kmean_ref, kcat_ref, bias_ref, qs_ref, sel_ref, acc_ref, m_ref, s_ref,
                   cmax_ref, ot_ref, *, nb):
    mb = MOBA_BLOCK
    hp = pl.program_id(1)
    slope2 = [slopes_ref[2 * hp + h] * LOG2E for h in range(2)]

    def prepare():
        kpos = lax.broadcasted_iota(jnp.int32, (mb, mb), 0)
        qpos = lax.broadcasted_iota(jnp.int32, (mb, mb), 1)
        relf = (qpos - kpos).astype(F32)
        for h in range(2):
            alibi = (-slope2[h]) * relf
            bias_ref[h, 0] = alibi
            bias_ref[h, 1] = jnp.where(qpos >= kpos, alibi, NEG_INF)
        ones = jnp.ones((BF16_ROWS, mb), BF16)

        def prep(n, carry):
            rows = pl.ds(pl.multiple_of(n * mb, mb), mb)
            kblk = k_ref[rows, :]
            kb_ref[n] = kblk.astype(BF16)
            kmean_ref[pl.ds(n, 1), :] = jnp.mean(kblk, axis=0, keepdims=True)
            vt = v_ref[rows, :].T
            for h in range(2):
                vt_ref[h, n, :HEAD_DIM, :] = vt[h * HEAD_DIM:(h + 1) * HEAD_DIM].astype(BF16)
                vt_ref[h, n, HEAD_DIM:, :] = ones
            return carry

        lax.fori_loop(0, nb, prep, 0)
        km = kmean_ref[...]
        lo_lanes = lax.broadcasted_iota(jnp.int32, (nb, LANES), 1) < HEAD_DIM
        for h in range(2):
            kmh = jnp.where(lo_lanes if h == 0 else ~lo_lanes, km, 0.0)
            hi = kmh.astype(BF16)
            lo = (kmh - hi.astype(F32)).astype(BF16)
            kcat_ref[h * nb:(h + 1) * nb, :] = jnp.concatenate([hi, hi, lo], axis=1)

        blk = lax.broadcasted_iota(jnp.int32, (nb, mb), 0)

        def select(u, carry):
            for sub in range(GATE_UNROLL):
                i = u * GATE_UNROLL + sub
                q = q_ref[pl.ds(pl.multiple_of(i * mb, mb), mb), :]
                past = blk < i
                q_hi = q.astype(BF16)
                q_lo = (q - q_hi.astype(F32)).astype(BF16)
                gate = _dot_nt(kcat_ref[...], jnp.concatenate([q_hi, q_lo, q_hi], axis=1))
                for h in range(2):
                    g = jnp.where(past, gate[h * nb:(h + 1) * nb], NEG_INF)
                    avail = blk >= 0
                    sel = jnp.where(blk == i, 1.0, 0.0)
                    for _ in range(MOBA_TOPK):
                        mx = jnp.max(jnp.where(avail, g, NEG_INF), axis=0, keepdims=True)
                        is_max = avail & (g == mx)
                        first = jnp.min(jnp.where(is_max, blk, nb), axis=0, keepdims=True)
                        pick = blk == first
                        sel = jnp.where(pick & past, 1.0, sel)
                        avail = avail & ~pick
                    sel_ref[i, h, :nb] = sel
                    sel_ref[i, h, nb:] = jnp.zeros((SUBLANES, mb), F32)
            return carry

        lax.fori_loop(0, nb // GATE_UNROLL, select, 0)

    prepare()
    lo_half = lax.broadcasted_iota(jnp.int32, (mb, LANES), 1) < HEAD_DIM

    def scaled_queries(i):
        qc = q_ref[pl.ds(pl.multiple_of(i * mb, mb), mb), :] * (HEAD_DIM ** -0.5 * LOG2E)
        return jnp.concatenate([jnp.where(lo_half, qc, 0.0), jnp.where(lo_half, 0.0, qc)], axis=0).astype(BF16)

    def stage_scores(raw, own_first):
        variant_a = jnp.where(own_first, 1, 0)
        for h in range(2):
            cols = slice(h * mb, (h + 1) * mb)
            za = raw[:mb, cols] + bias_ref[h, variant_a]
            zb = raw[mb:, cols] + bias_ref[h, 0]
            s_ref[:mb, cols] = za
            s_ref[mb:, cols] = zb
            cmax_ref[h, 0] = jnp.max(za, axis=0, keepdims=True)
            cmax_ref[h, 1] = jnp.max(zb, axis=0, keepdims=True)

    q_first = scaled_queries(0)
    qs_ref[0] = q_first
    stage_scores(_dot_nt(jnp.concatenate([kb_ref[0], kb_ref[0]], axis=0), q_first), True)

    def query_block(i, carry0):
        slot = i % 2
        i_next = jnp.minimum(i + 1, nb - 1)
        qs_ref[1 - slot] = scaled_queries(i_next)
        m_ref[...] = jnp.full(m_ref.shape, NEG_INF, F32)
        acc_ref[...] = jnp.zeros(acc_ref.shape, F32)
        nsteps = (i + 2) // 2

        def step(p, carry):
            first = p == 0
            j0 = jnp.where(first, i, 2 * p - 1)
            j1 = jnp.minimum(2 * p, nb - 1)
            row1 = jnp.where(2 * p < i, j1, nb)
            last = p == nsteps - 1
            n0 = jnp.where(last, i_next, jnp.minimum(2 * p + 1, nb - 1))
            n1 = jnp.where(last, 0, jnp.minimum(2 * p + 2, nb - 1))
            q_slot = jnp.where(last, 1 - slot, slot)
            s_next = _dot_nt(jnp.concatenate([kb_ref[n0], kb_ref[n1]], axis=0), qs_ref[q_slot])
            dist_a = ((i - j0) * mb).astype(F32)
            dist_b = ((i - j1) * mb).astype(F32)
            for h in range(2):
                cols = slice(h * mb, (h + 1) * mb)
                shift_a = -slope2[h] * dist_a
                shift_b = -slope2[h] * dist_b
                chosen_a = sel_ref[i, h, pl.ds(j0, 1), :] > 0.5
                chosen_b = sel_ref[i, h, pl.ds(row1, 1), :] > 0.5
                m_a = jnp.where(chosen_a, cmax_ref[h, 0] + shift_a, NEG_INF)
                m_b = jnp.where(chosen_b, cmax_ref[h, 1] + shift_b, NEG_INF)
                m_old = m_ref[h]
                m_new = jnp.maximum(m_old, jnp.maximum(m_a, m_b))
                alpha = jnp.exp2(m_old - m_new)
                pa = jnp.exp2(s_ref[:mb, cols] - jnp.where(chosen_a, m_new - shift_a, POS_INF)).astype(BF16)
                pb = jnp.exp2(s_ref[mb:, cols] - jnp.where(chosen_b, m_new - shift_b, POS_INF)).astype(BF16)
                vt = jnp.concatenate([vt_ref[h, j0], vt_ref[h, j1]], axis=1)
                acc_ref[h] = alpha * acc_ref[h] + jnp.dot(vt, jnp.concatenate([pa, pb], axis=0),
                                                          preferred_element_type=F32)
                m_ref[h] = m_new
            stage_scores(s_next, last)
            return carry

        lax.fori_loop(0, nsteps, step, 0)
        outs = []
        for h in range(2):
            acc = acc_ref[h]
            outs.append(acc[:HEAD_DIM] * (1.0 / acc[HEAD_DIM:HEAD_DIM + 1]))
        ot_ref[i] = jnp.concatenate(outs, axis=0)
        return carry0

    lax.fori_loop(0, nb, query_block, 0)

    def emit(u, carry):
        for sub in range(GATE_UNROLL):
            i = u * GATE_UNROLL + sub
            o_ref[pl.ds(pl.multiple_of(i * mb, mb), mb), :] = ot_ref[i].T
        return carry

    lax.fori_loop(0, nb // GATE_UNROLL, emit, 0)


def attn_b(proj, *, q_col, k_col, v_col):
    _, B, S, _ = proj.shape
    mb = MOBA_BLOCK
    assert S % mb == 0
    nb = S // mb
    assert MOBA_TOPK < nb and nb % SUBLANES == 0 and nb % GATE_UNROLL == 0
    pairs = B_HEADS // 2
    kern = functools.partial(_attn_b_kernel, nb=nb)
    return pl.pallas_call(
        kern,
        grid=(B, pairs),
        in_specs=[_smem_spec(),
                  pl.BlockSpec((None, None, S, LANES), lambda b, p: (q_col // LANES + p, b, 0, 0)),
                  pl.BlockSpec((None, None, S, LANES), lambda b, p: (k_col // LANES + p, b, 0, 0)),
                  pl.BlockSpec((None, None, S, LANES), lambda b, p: (v_col // LANES + p, b, 0, 0))],
        out_specs=pl.BlockSpec((None, S, LANES), lambda b, p: (b, 0, p)),
        out_shape=jax.ShapeDtypeStruct((B, S, B_HEADS * HEAD_DIM), F32),
        scratch_shapes=[pltpu.VMEM((nb, mb, LANES), BF16),
                        pltpu.VMEM((2, nb, VT_ROWS, mb), BF16),
                        pltpu.VMEM((nb, LANES), F32),
                        pltpu.VMEM((2 * nb, 3 * LANES), BF16),
                        pltpu.VMEM((2, 2, mb, mb), F32),
                        pltpu.VMEM((2, 2 * mb, LANES), BF16),
                        pltpu.VMEM((nb, 2, nb + SUBLANES, mb), F32),
                        pltpu.VMEM((2, VT_ROWS, mb), F32),
                        pltpu.VMEM((2, 1, mb), F32),
                        pltpu.VMEM((2 * mb, 2 * mb), F32),
                        pltpu.VMEM((2, 2, 1, mb), F32),
                        pltpu.VMEM((nb, LANES, mb), F32)],
        compiler_params=_params("parallel", "parallel"),
        name="attn_moba",
    )(_alibi_slopes(B_HEADS), proj, proj, proj)


def _attn_c_kernel(slopes_ref, q_ref, k_ref, v_ref, o_ref,
                   q4_ref, k4_ref, v4_ref, kb_ref, vt_ref, bias_ref, orun_ref, lse_ref, *, seq, group):
    n = C_STEPS
    hp = pl.program_id(1)
    ntile = seq // n
    quarter = seq // 4
    slab = n // 4
    lo_half = lax.broadcasted_iota(jnp.int32, (n, LANES), 1) < HEAD_DIM
    ones = jnp.ones((BF16_ROWS, 2 * n), BF16)
    kk = lax.broadcasted_iota(jnp.int32, (2 * n, n), 0)
    qq = lax.broadcasted_iota(jnp.int32, (2 * n, n), 1)

    def reorder(c, carry):
        r4 = c // (ntile // 4)
        src = pl.ds(r4 + 4 * n * (c % (ntile // 4)), n, stride=4)
        dst = pl.ds(pl.multiple_of(c * n, n), n)
        q4_ref[dst, :] = q_ref[src, :]
        k4_ref[dst, :] = k_ref[src, :]
        v4_ref[dst, :] = v_ref[src, :]
        return carry

    lax.fori_loop(0, ntile, reorder, 0)

    for bi, (w, d) in enumerate(C_PATTERNS):
        nblk = seq // w
        first, last = bi == 0, bi == len(C_PATTERNS) - 1
        assert d in (1, 4, 16)

        def load_tile(ref, t, d=d):
            if d == 4:
                return ref[pl.ds(pl.multiple_of(t * n, n), n), :]
            if d == 16:
                r16, b16 = t // 2, t % 2
                return ref[pl.ds((r16 % 4) * quarter + b16 * (4 * n) + r16 // 4, n, stride=4), :]
            return jnp.concatenate([ref[pl.ds(pl.multiple_of(r4 * quarter + t * slab, slab), slab), :]
                                    for r4 in range(4)], axis=0)

        def store_tile(ref, t, val, d=d):
            if d == 4:
                ref[pl.ds(pl.multiple_of(t * n, n), n), :] = val
            elif d == 16:
                r16, b16 = t // 2, t % 2
                ref[pl.ds((r16 % 4) * quarter + b16 * (4 * n) + r16 // 4, n, stride=4), :] = val
            else:
                for r4 in range(4):
                    ref[pl.ds(pl.multiple_of(r4 * quarter + t * slab, slab), slab), :] = val[r4 * slab:(r4 + 1) * slab]

        if d == 1:
            member = lambda x: 4 * (x % slab) + x // slab
            steps = member(qq) - member(kk % n) + jnp.where(kk < n, n, 0)
        else:
            steps = qq + n - kk
        valid = (steps >= 0) & (steps <= n)
        stepf = steps.astype(F32)
        for h in range(2):
            full = jnp.where(valid, (-slopes_ref[2 * hp + h] * (LOG2E * d)) * stepf, NEG_INF)
            bias_ref[h, 1] = full
            bias_ref[h, 0] = jnp.where(kk >= n, full, NEG_INF)

        def prep(u, carry, load_tile=load_tile):
            for sub in range(group):
                t = u * group + sub
                kb_ref[t] = load_tile(k4_ref, t).astype(BF16)
                vt_ref[t] = load_tile(v4_ref, t).T.astype(BF16)
            return carry

        lax.fori_loop(0, ntile // group, prep, 0)

        def tiles(u, carry, load_tile=load_tile, store_tile=store_tile, nblk=nblk, first=first):
            ts = [u * group + sub for sub in range(group)]
            prevs, variants, scores = [], [], []
            for t in ts:
                has_prev = (t % nblk) > 0
                tp = jnp.where(has_prev, t - 1, t)
                prevs.append(tp)
                variants.append(jnp.where(has_prev, 1, 0))
                q = load_tile(q4_ref, t) * (HEAD_DIM ** -0.5 * LOG2E)
                q_both = jnp.concatenate([jnp.where(lo_half, q, 0.0), jnp.where(lo_half, 0.0, q)], axis=0)
                kcat = jnp.concatenate([kb_ref[tp], kb_ref[t]], axis=0)
                scores.append(_dot_nt(kcat, q_both.astype(BF16)))
            anchor = sum(_zero_after(sc[0:1, :n]) for sc in scores[1:])
            results = []
            for g, t in enumerate(ts):
                vtcat = jnp.concatenate([vt_ref[prevs[g]], vt_ref[t]], axis=1)
                o_rows, lse_rows = [], []
                for h in range(2):
                    s = scores[g][:, h * n:(h + 1) * n] + bias_ref[h, variants[g]]
                    m = jnp.max(s, axis=0, keepdims=True)
                    if g == 0:
                        m = m + anchor
                    p = jnp.exp2(s - m).astype(BF16)
                    vaug = jnp.concatenate([vtcat[h * HEAD_DIM:(h + 1) * HEAD_DIM], ones], axis=0)
                    acc = jnp.dot(vaug, p, preferred_element_type=F32)
                    l = acc[HEAD_DIM:HEAD_DIM + 1]
                    o_rows.append(acc[:HEAD_DIM] * (1.0 / l))
                    lse_rows.append(jnp.broadcast_to(m + jnp.log2(l), (HEAD_DIM, n)))
                results.append((jnp.concatenate(o_rows, axis=0), jnp.concatenate(lse_rows, axis=0)))
            for g, t in enumerate(ts):
                o_t = results[g][0].T
                lse_t = results[g][1].T
                if first:
                    store_tile(orun_ref, t, o_t)
                    store_tile(lse_ref, t, lse_t)
                else:
                    lse_old = load_tile(lse_ref, t)
                    top = jnp.maximum(lse_old, lse_t)
                    w_old = jnp.exp2(lse_old - top)
                    w_t = jnp.exp2(lse_t - top)
                    den = w_old + w_t
                    store_tile(orun_ref, t, (w_old * load_tile(orun_ref, t) + w_t * o_t) * (1.0 / den))
                    if not last:
                        store_tile(lse_ref, t, top + jnp.log2(den))
            return carry

        lax.fori_loop(0, ntile // group, tiles, 0)

    def restore(c, carry):
        r4 = c // (ntile // 4)
        o_ref[pl.ds(r4 + 4 * n * (c % (ntile // 4)), n, stride=4), :] = orun_ref[pl.ds(pl.multiple_of(c * n, n), n), :]
        return carry

    lax.fori_loop(0, ntile, restore, 0)


def attn_c(proj, *, q_col, k_col, v_col):
    _, B, S, _ = proj.shape
    n = C_STEPS
    assert all(S % w == 0 and w // d == n for w, d in C_PATTERNS)
    assert (S // n) % C_GROUP == 0 and S % (16 * n) == 0
    pairs = C_HEADS // 2
    kern = functools.partial(_attn_c_kernel, seq=S, group=C_GROUP)
    return pl.pallas_call(
        kern,
        grid=(B, pairs),
        in_specs=[_smem_spec(),
                  pl.BlockSpec((None, None, S, LANES), lambda b, p: (q_col // LANES + p, b, 0, 0)),
                  pl.BlockSpec((None, None, S, LANES), lambda b, p: (k_col // LANES + p, b, 0, 0)),
                  pl.BlockSpec((None, None, S, LANES), lambda b, p: (v_col // LANES + p, b, 0, 0))],
        out_specs=pl.BlockSpec((None, S, LANES), lambda b, p: (b, 0, p)),
        out_shape=jax.ShapeDtypeStruct((B, S, C_HEADS * HEAD_DIM), F32),
        scratch_shapes=[pltpu.VMEM((S, LANES), F32),
                        pltpu.VMEM((S, LANES), F32),
                        pltpu.VMEM((S, LANES), F32),
                        pltpu.VMEM((S // n, n, LANES), BF16),
                        pltpu.VMEM((S // n, LANES, n), BF16),
                        pltpu.VMEM((2, 2, 2 * n, n), F32),
                        pltpu.VMEM((S, LANES), F32),
                        pltpu.VMEM((S, LANES), F32)],
        compiler_params=_params("parallel", "parallel"),
        name="attn_dilated",
    )(_alibi_slopes(C_HEADS), proj, proj, proj)


def _mix_out_kernel(oa_ref, ob_ref, oc_ref, ga_ref, gb_ref, gc_ref, h_ref, w_ref, o_ref):
    y = jnp.concatenate([_rms(oa_ref[...], ga_ref[...]).astype(BF16),
                         _rms(ob_ref[...], gb_ref[...]).astype(BF16),
                         _rms(oc_ref[...], gc_ref[...]).astype(BF16)], axis=-1)
    o_ref[...] = h_ref[...] + jnp.dot(y, w_ref[...], preferred_element_type=F32)


def mix_out(oa, ob, oc, ga, gb, gc, h, w, layer, *, tm):
    M, D = h.shape
    wa, wb, wc = oa.shape[1], ob.shape[1], oc.shape[1]
    K = wa + wb + wc
    assert w.shape[1:] == (K, D) and M % tm == 0
    row = lambda width: pl.BlockSpec((tm, width), lambda i: (i, 0))
    gain = lambda width: pl.BlockSpec((1, width), lambda i: (0, 0))
    return pl.pallas_call(
        _mix_out_kernel,
        grid=(M // tm,),
        in_specs=[row(wa), row(wb), row(wc), gain(wa), gain(wb), gain(wc), row(D),
                  pl.BlockSpec((None, K, D), lambda i: (layer, 0, 0))],
        out_specs=row(D),
        out_shape=jax.ShapeDtypeStruct((M, D), F32),
        compiler_params=_params("parallel"),
        name="mix_out",
    )(oa, ob, oc, ga, gb, gc, h, w)


def _xattn_kernel(h_ref, g_ref, wq_ref, kv_ref, wo_ref, o_ref):
    xw = X_HEADS * X_HEAD_DIM
    h = h_ref[...]
    xn = _rms(h, g_ref[...]).astype(BF16)
    q = jnp.dot(xn, wq_ref[...], preferred_element_type=F32) * (X_HEAD_DIM ** -0.5)
    heads = []
    for hd in range(X_HEADS):
        lanes = slice(hd * X_HEAD_DIM, (hd + 1) * X_HEAD_DIM)
        s = _dot_nt(q[:, lanes].astype(BF16), kv_ref[:, lanes])
        m = jnp.max(s, axis=-1, keepdims=True)
        p = jnp.exp(s - m)
        l = jnp.sum(p, axis=-1, keepdims=True)
        v = kv_ref[:, xw + hd * X_HEAD_DIM:xw + (hd + 1) * X_HEAD_DIM]
        heads.append(jnp.dot(p.astype(BF16), v, preferred_element_type=F32) / l)
    o = jnp.concatenate(heads, axis=-1).astype(BF16)
    o_ref[...] = h + jnp.dot(o, wo_ref[...], preferred_element_type=F32)


def xattn(h, g, wq, kv, wo, layer, *, tq):
    B, S, D = h.shape
    Mm = kv.shape[1]
    xw = X_HEADS * X_HEAD_DIM
    assert S % tq == 0
    return pl.pallas_call(
        _xattn_kernel,
        grid=(B, S // tq),
        in_specs=[pl.BlockSpec((None, tq, D), lambda b, i: (b, i, 0)),
                  pl.BlockSpec((1, D), lambda b, i: (0, 0)),
                  pl.BlockSpec((None, D, xw), lambda b, i: (layer, 0, 0)),
                  pl.BlockSpec((None, Mm, 2 * xw), lambda b, i: (b, 0, 0)),
                  pl.BlockSpec((None, xw, D), lambda b, i: (layer, 0, 0))],
        out_specs=pl.BlockSpec((None, tq, D), lambda b, i: (b, i, 0)),
        out_shape=jax.ShapeDtypeStruct((B, S, D), F32),
        compiler_params=_params("parallel", "parallel"),
        name="xattn",
    )(h, g, wq, kv, wo)


def _ffn_kernel(h_ref, g_ref, wg_ref, wu_ref, wd_ref, gf_ref, o_ref, xn_ref, *, final_norm):
    f = pl.program_id(1)

    @pl.when(f == 0)
    def _():
        h = h_ref[...]
        xn_ref[...] = _rms(h, g_ref[...]).astype(BF16)
        o_ref[...] = h

    xn = xn_ref[...]
    gate = jnp.dot(xn, wg_ref[...], preferred_element_type=F32)
    up = jnp.dot(xn, wu_ref[...], preferred_element_type=F32)
    act = (gate / (1.0 + jnp.exp(-gate))) * up
    o_ref[...] += jnp.dot(act.astype(BF16), wd_ref[...], preferred_element_type=F32)

    if final_norm:
        @pl.when(f == pl.num_programs(1) - 1)
        def _():
            o_ref[...] = _rms(o_ref[...], gf_ref[...])


def ffn(h, g, wg, wu, wd, gf, layer, *, tm, tf, final_norm):
    M, D = h.shape
    Fd = wg.shape[2]
    assert M % tm == 0 and Fd % tf == 0
    kern = functools.partial(_ffn_kernel, final_norm=final_norm)
    return pl.pallas_call(
        kern,
        grid=(M // tm, Fd // tf),
        in_specs=[pl.BlockSpec((tm, D), lambda i, f: (i, 0)),
                  pl.BlockSpec((1, D), lambda i, f: (0, 0)),
                  pl.BlockSpec((None, D, tf), lambda i, f: (layer, 0, f)),
                  pl.BlockSpec((None, D, tf), lambda i, f: (layer, 0, f)),
                  pl.BlockSpec((None, tf, D), lambda i, f: (layer, f, 0)),
                  pl.BlockSpec((1, D), lambda i, f: (0, 0))],
        out_specs=pl.BlockSpec((tm, D), lambda i, f: (i, 0)),
        out_shape=jax.ShapeDtypeStruct((M, D), F32),
        scratch_shapes=[pltpu.VMEM((tm, D), BF16)],
        compiler_params=_params("parallel", "arbitrary"),
        name="ffn",
    )(h, g, wg, wu, wd, gf)


def kernel(x, mem, norm_mix, w_in, sinks, gain_a, gain_b, gain_c, w_out, norm_xattn, norm_mem,
           wq_x, wk_x, wv_x, wo_x, norm_ffn, w_gate, w_up, w_down, norm_final):
    B, S, D = x.shape
    depth = w_in.shape[0]
    Mm = mem.shape[1]
    qa_w, kva_w = A_HEADS * HEAD_DIM, A_KV_HEADS * HEAD_DIM
    b_w, c_w = B_HEADS * HEAD_DIM, C_HEADS * HEAD_DIM
    widths = (qa_w, kva_w, kva_w, b_w, b_w, b_w, c_w, c_w, c_w)
    cols = [int(c) for c in np.concatenate([[0], np.cumsum(widths)[:-1]])]
    in_w = sum(widths)

    row = lambda v: v.reshape(1, -1)
    h = x.reshape(B * S, D)
    mem2 = mem.reshape(B * Mm, D)
    w_in_b, w_out_b = w_in.astype(BF16), w_out.astype(BF16)
    w_kv_b = jnp.concatenate([wk_x, wv_x], axis=2).astype(BF16)
    wq_b, wo_b = wq_x.astype(BF16), wo_x.astype(BF16)
    wg_b, wu_b, wd_b = w_gate.astype(BF16), w_up.astype(BF16), w_down.astype(BF16)
    for l in range(depth):
        proj = norm_matmul_slabs(h, row(norm_mix[l]), w_in_b, l, tm=1024, tn=1792)
        proj = proj.reshape(in_w // LANES, B, S, LANES)
        oa = attn_a(proj, sinks[l], q_col=cols[0], k_col=cols[1], v_col=cols[2], tq=1024, group=4)
        ob = attn_b(proj, q_col=cols[3], k_col=cols[4], v_col=cols[5])
        oc = attn_c(proj, q_col=cols[6], k_col=cols[7], v_col=cols[8])
        h = mix_out(oa.reshape(B * S, qa_w), ob.reshape(B * S, b_w), oc.reshape(B * S, c_w),
                    row(gain_a[l]), row(gain_b[l]), row(gain_c[l]), h, w_out_b, l, tm=512)
        kv = norm_matmul(mem2, row(norm_mem[l]), w_kv_b, l, tm=512, tn=w_kv_b.shape[2], out_dtype=BF16)
        h = xattn(h.reshape(B, S, D), row(norm_xattn[l]), wq_b, kv.reshape(B, Mm, -1), wo_b, l,
                  tq=1024).reshape(B * S, D)
        h = ffn(h, row(norm_ffn[l]), wg_b, wu_b, wd_b, row(norm_final), l, tm=1024, tf=512,
                final_norm=(l == depth - 1))
    return h.reshape(B, S, D)
```
